```python
import math
import jax, jax.numpy as jnp
from jax import lax
import numpy as np


D_MODEL = 1024
BATCH = 4
SEQ = 4096
DEPTH = 2

HY_WIDTH = 512
HY_ORDER = 2
HY_EMB = 33
HY_FFN = 64
HY_SHORT = 3
HY_FAST_DECAY = 0.3
HY_SLOW_DECAY = 1.5
HY_TARGET = 1e-2
HY_FILTER_SCALE = 0.03
DA_HEADS = 4
DA_HEAD_DIM = 64
DA_V_DIM = 2 * DA_HEAD_DIM
DA_QK_WIDTH = DA_HEADS * 2 * DA_HEAD_DIM
DA_V_WIDTH = DA_HEADS * DA_V_DIM
Q_BLOCK = 128
ROPE_THETA = 10000.0
N_BRANCH = 2
IN_WIDTH = (HY_ORDER + 1) * HY_WIDTH + 2 * DA_QK_WIDTH + DA_V_WIDTH + N_BRANCH * D_MODEL
N_EXPERTS = 16
EC_FACTOR = 2
D_EXPERT = 2048
NORM_EPS = 1e-6
SUBLN_EPS = 1e-5

kernel_name = "hybrid_hyena_diffattn_ec_moe_encoder"


def rms_norm(x, g, eps=NORM_EPS):
    xf = x.astype(jnp.float32)
    y = xf * lax.rsqrt(jnp.mean(xf * xf, axis=-1, keepdims=True) + eps) * g.astype(jnp.float32)
    return y.astype(x.dtype)


def rope(x, pos):
    dh = x.shape[-1]
    half = dh // 2
    inv = ROPE_THETA ** (-jnp.arange(half, dtype=jnp.float32) * 2.0 / dh)
    ang = pos[:, None] * inv[None, :]
    cos = jnp.cos(ang)[None, :, None, :]
    sin = jnp.sin(ang)[None, :, None, :]
    xf = x.astype(jnp.float32)
    x1, x2 = xf[..., :half], xf[..., half:]
    out = jnp.concatenate([x1 * cos - x2 * sin, x2 * cos + x1 * sin], axis=-1)
    return out.astype(x.dtype)


def hyena_filters(L, w1, b1, f1, w2, b2, f2, w3):
    t = jnp.linspace(0.0, 1.0, L, dtype=jnp.float32)
    bands = (HY_EMB - 1) // 2
    w = 2.0 * math.pi * jnp.arange(L, dtype=jnp.float32) / L
    fr = jnp.linspace(1e-4, bands - 1, bands, dtype=jnp.float32)
    ang = w[:, None] * fr[None, :]
    z = jnp.concatenate([t[:, None], jnp.cos(ang), -jnp.sin(ang)], axis=-1)
    h = jnp.sin(f1.astype(jnp.float32) * (z @ w1.astype(jnp.float32) + b1.astype(jnp.float32)))
    h = jnp.sin(f2.astype(jnp.float32) * (h @ w2.astype(jnp.float32) + b2.astype(jnp.float32)))
    h = h @ w3.astype(jnp.float32)
    h = h.reshape(L, HY_ORDER, 2, HY_WIDTH).transpose(1, 2, 0, 3)
    min_decay = math.log(HY_TARGET) / HY_FAST_DECAY
    max_decay = math.log(HY_TARGET) / HY_SLOW_DECAY
    deltas = jnp.abs(jnp.linspace(min_decay, max_decay, HY_WIDTH, dtype=jnp.float32))
    decay = jnp.exp(-t[:, None] * deltas[None, :])
    return h * decay[None, None]


def bidir_fftconv(z, h_fwd, h_bwd, d_bias):
    L, C = z.shape[1], z.shape[2]
    h_full = jnp.concatenate([h_fwd, jnp.zeros((1, C), jnp.float32), h_bwd[1:][::-1]], axis=0)
    hf = jnp.fft.rfft(h_full, n=2 * L, axis=0)
    zf = jnp.fft.rfft(z.astype(jnp.float32), n=2 * L, axis=1)
    y = jnp.fft.irfft(zf * hf[None], n=2 * L, axis=1)[:, :L]
    y = y + z.astype(jnp.float32) * d_bias.astype(jnp.float32)
    return y.astype(z.dtype)


def hyena_branch(u, conv_w, conv_b, filters, d_bias):
    up = jnp.pad(u, ((0, 0), (1, 1), (0, 0)))
    uc = up[:, :-2] * conv_w[0] + up[:, 1:-1] * conv_w[1] + up[:, 2:] * conv_w[2] + conv_b
    v, x1, x2 = jnp.split(uc, HY_ORDER + 1, axis=-1)
    z = v
    for o, gate in enumerate((x1, x2)):
        z = gate * bidir_fftconv(z, filters[o, 0], filters[o, 1], d_bias[o])
    return z


def diff_attention(q, k, v, lam, lam_init, subln_g):
    B, S = q.shape[0], q.shape[1]
    nb = S // Q_BLOCK
    scale = DA_HEAD_DIM ** -0.5
    qb = (q * scale).reshape(B, nb, Q_BLOCK, DA_HEADS, 2, DA_HEAD_DIM).transpose(1, 0, 3, 4, 2, 5)
    kh = k.reshape(B, S, DA_HEADS, 2, DA_HEAD_DIM).transpose(0, 2, 3, 1, 4)
    vh = v.transpose(0, 2, 1, 3)

    def block(qblk):
        s = jnp.einsum('bhcqd,bhckd->bhcqk', qblk, kh).astype(jnp.float32)
        p = jax.nn.softmax(s, axis=-1)
        a = p[:, :, 0] - lam * p[:, :, 1]
        return jnp.einsum('bhqk,bhkv->bhqv', a.astype(vh.dtype), vh)

    o = lax.map(block, qb)
    o = o.transpose(1, 0, 3, 2, 4).reshape(B, S, DA_HEADS, DA_V_DIM)
    o = rms_norm(o, subln_g, SUBLN_EPS) * (1.0 - lam_init)
    return o.reshape(B, S, DA_V_WIDTH)


def ec_moe(n, w_r, b_r, w_gate, w_up, w_down):
    B, S, D = n.shape
    cap = EC_FACTOR * S // N_EXPERTS
    logits = (n @ w_r + b_r).astype(jnp.float32)
    aff = jax.nn.softmax(logits, axis=-1)
    g, idx = lax.top_k(aff.transpose(0, 2, 1), cap)
    xg = jax.vmap(lambda xb, ib: xb[ib])(n, idx)
    hdn = jax.nn.silu(jnp.einsum('becd,edf->becf', xg, w_gate)) * jnp.einsum('becd,edf->becf', xg, w_up)
    ye = jnp.einsum('becf,efd->becd', hdn, w_down) * g[..., None].astype(n.dtype)
    flat = (jnp.arange(B, dtype=jnp.int32)[:, None, None] * S + idx).reshape(-1)
    out = jnp.zeros((B * S, D), n.dtype).at[flat].add(ye.reshape(-1, D))
    return out.reshape(B, S, D)


def setup_inputs(seed: int = 0) -> dict:
    key = jax.random.key(seed)
    ks = jax.random.split(key, 32)

    def nrm(k, shape, scale):
        return jax.random.normal(k, shape, jnp.float32) * scale

    C = HY_WIDTH
    E = N_EXPERTS
    F = D_EXPERT
    return {
        "x": nrm(ks[0], (BATCH, SEQ, D_MODEL), 1.0),
        "norm_mix": 1.0 + nrm(ks[1], (DEPTH, D_MODEL), 0.02),
        "w_in": nrm(ks[2], (DEPTH, D_MODEL, IN_WIDTH), D_MODEL ** -0.5),
        "b_in": nrm(ks[3], (DEPTH, IN_WIDTH), 0.02),
        "hy_conv_w": nrm(ks[4], (DEPTH, HY_SHORT, (HY_ORDER + 1) * C), HY_SHORT ** -0.5),
        "hy_conv_b": nrm(ks[5], (DEPTH, (HY_ORDER + 1) * C), 0.02),
        "hy_ffn_w1": nrm(ks[6], (DEPTH, HY_EMB, HY_FFN), HY_EMB ** -0.5),
        "hy_ffn_b1": nrm(ks[7], (DEPTH, HY_FFN), 0.02),
        "hy_ffn_f1": 1.0 + nrm(ks[8], (DEPTH, HY_FFN), 0.02),
        "hy_ffn_w2": nrm(ks[9], (DEPTH, HY_FFN, HY_FFN), HY_FFN ** -0.5),
        "hy_ffn_b2": nrm(ks[10], (DEPTH, HY_FFN), 0.02),
        "hy_ffn_f2": 1.0 + nrm(ks[11], (DEPTH, HY_FFN), 0.02),
        "hy_ffn_w3": nrm(ks[12], (DEPTH, HY_FFN, HY_ORDER * 2 * C), HY_FILTER_SCALE * HY_FFN ** -0.5),
        "hy_bias": nrm(ks[13], (DEPTH, HY_ORDER, C), 0.2),
        "lambda_q1": nrm(ks[14], (DEPTH, DA_HEAD_DIM), 0.1),
        "lambda_k1": nrm(ks[15], (DEPTH, DA_HEAD_DIM), 0.1),
        "lambda_q2": nrm(ks[16], (DEPTH, DA_HEAD_DIM), 0.1),
        "lambda_k2": nrm(ks[17], (DEPTH, DA_HEAD_DIM), 0.1),
        "subln_g": 1.0 + nrm(ks[18], (DEPTH, DA_V_DIM), 0.02),
        "w_up_hyena": nrm(ks[19], (DEPTH, C, D_MODEL), C ** -0.5),
        "w_up_attn": nrm(ks[20], (DEPTH, DA_V_WIDTH, D_MODEL), DA_V_WIDTH ** -0.5),
        "w_out": nrm(ks[21], (DEPTH, D_MODEL, D_MODEL), D_MODEL ** -0.5),
        "norm_ffn": 1.0 + nrm(ks[22], (DEPTH, D_MODEL), 0.02),
        "w_router": nrm(ks[23], (DEPTH, D_MODEL, E), D_MODEL ** -0.5),
        "b_router": nrm(ks[24], (DEPTH, E), 0.01),
        "w_e_gate": nrm(ks[25], (DEPTH, E, D_MODEL, F), D_MODEL ** -0.5),
        "w_e_up": nrm(ks[26], (DEPTH, E, D_MODEL, F), D_MODEL ** -0.5),
        "w_e_down": nrm(ks[27], (DEPTH, E, F, D_MODEL), F ** -0.5),
        "norm_final": 1.0 + nrm(ks[28], (D_MODEL,), 0.02),
    }


def reference(x, norm_mix, w_in, b_in, hy_conv_w, hy_conv_b, hy_ffn_w1, hy_ffn_b1, hy_ffn_f1,
              hy_ffn_w2, hy_ffn_b2, hy_ffn_f2, hy_ffn_w3, hy_bias, lambda_q1, lambda_k1, lambda_q2,
              lambda_k2, subln_g, w_up_hyena, w_up_attn, w_out, norm_ffn, w_router, b_router,
              w_e_gate, w_e_up, w_e_down, norm_final):
    B, S, D = x.shape
    pos = jnp.arange(S, dtype=jnp.float32)
    splits = np.cumsum([(HY_ORDER + 1) * HY_WIDTH, DA_QK_WIDTH, DA_QK_WIDTH, DA_V_WIDTH, D_MODEL]).tolist()
    for l in range(DEPTH):
        n = rms_norm(x, norm_mix[l])
        p = n @ w_in[l] + b_in[l]
        u_hy, q, k, v, g_hy, g_da = jnp.split(p, splits, axis=-1)

        filt = hyena_filters(S, hy_ffn_w1[l], hy_ffn_b1[l], hy_ffn_f1[l], hy_ffn_w2[l],
                             hy_ffn_b2[l], hy_ffn_f2[l], hy_ffn_w3[l])
        y_hy = hyena_branch(u_hy, hy_conv_w[l], hy_conv_b[l], filt, hy_bias[l])

        q = rope(q.reshape(B, S, 2 * DA_HEADS, DA_HEAD_DIM), pos)
        k = rope(k.reshape(B, S, 2 * DA_HEADS, DA_HEAD_DIM), pos)
        v = v.reshape(B, S, DA_HEADS, DA_V_DIM)
        lam_init = 0.8 - 0.6 * math.exp(-0.3 * l)
        lam = (jnp.exp(jnp.sum(lambda_q1[l].astype(jnp.float32) * lambda_k1[l].astype(jnp.float32)))
               - jnp.exp(jnp.sum(lambda_q2[l].astype(jnp.float32) * lambda_k2[l].astype(jnp.float32)))
               + lam_init)
        y_da = diff_attention(q, k, v, lam, lam_init, subln_g[l])

        gh = jax.nn.sigmoid(g_hy.astype(jnp.float32)).astype(x.dtype)
        ga = jax.nn.sigmoid(g_da.astype(jnp.float32)).astype(x.dtype)
        merged = gh * (y_hy @ w_up_hyena[l]) + ga * (y_da @ w_up_attn[l])
        x = x + merged @ w_out[l]

        x = x + ec_moe(rms_norm(x, norm_ffn[l]), w_router[l], b_router[l],
                       w_e_gate[l], w_e_up[l], w_e_down[l])
    return rms_norm(x, norm_final)
```

```python
import functools
import math

import numpy as np
import jax
import jax.numpy as jnp
from jax import lax
from jax.experimental import pallas as pl
from jax.experimental.pallas import tpu as pltpu

F32 = jnp.float32
BF16 = jnp.bfloat16

NORM_EPS = 1e-6
SUBLN_EPS = 1e-5
ROPE_THETA = 10000.0
HY_FAST_DECAY = 0.3
HY_SLOW_DECAY = 1.5
HY_TARGET = 1e-2
EC_FACTOR = 2

VMEM_LIMIT_V7X = 56 * 1024 * 1024
LANES = 128

DFT_N1 = 64
DFT_N2 = 128


def _cparams(*sem):
    return pltpu.CompilerParams(dimension_semantics=sem, vmem_limit_bytes=VMEM_LIMIT_V7X)


def _single(block_shape, index_map):
    return pl.BlockSpec(block_shape, index_map, pipeline_mode=pl.Buffered(1))


def _inproj_kernel(x_ref, g_ref, w_ref, b_ref, o_ref, n_sc):
    @pl.when(pl.program_id(1) == 0)
    def _():
        x = x_ref[...]
        n = x * lax.rsqrt(jnp.mean(x * x, axis=-1, keepdims=True) + NORM_EPS) * g_ref[...]
        n_sc[...] = n.astype(BF16)

    o_ref[...] = jnp.dot(n_sc[...], w_ref[...], preferred_element_type=F32) + b_ref[...]


def _inproj(x2, g, w_bf, b, tm=1024, tn=1024):
    t, d = x2.shape
    width = w_bf.shape[1]
    return pl.pallas_call(
        _inproj_kernel,
        grid=(t // tm, width // tn),
        in_specs=[
            pl.BlockSpec((tm, d), lambda i, j: (i, 0)),
            pl.BlockSpec((1, d), lambda i, j: (0, 0)),
            pl.BlockSpec((d, tn), lambda i, j: (0, j)),
            pl.BlockSpec((1, tn), lambda i, j: (0, j)),
        ],
        out_specs=pl.BlockSpec((tm, tn), lambda i, j: (i, j)),
        out_shape=jax.ShapeDtypeStruct((t, width), F32),
        scratch_shapes=[pltpu.VMEM((tm, d), BF16)],
        compiler_params=_cparams("parallel", "arbitrary"),
        name="inproj",
    )(x2, g, w_bf, b)


def _shortconv_kernel(u_ref, w_ref, b_ref, o_ref, pad_sc, *, seq, n2, nh):
    zeros = jnp.zeros((8, LANES), F32)
    pad_sc[pl.ds(0, 8), :] = zeros
    pad_sc[pl.ds(8 + seq, 8), :] = zeros
    pad_sc[pl.ds(8, seq), :] = u_ref[...]
    w = w_ref[...]
    bias = b_ref[...]

    def body(r, carry):
        prev = pad_sc[pl.ds(7 + r, nh, stride=n2), :]
        cur = pad_sc[pl.ds(8 + r, nh, stride=n2), :]
        nxt = pad_sc[pl.ds(9 + r, nh, stride=n2), :]
        o_ref[pl.ds(pl.multiple_of(r * nh, nh), nh), :] = prev * w[0:1] + cur * w[1:2] + nxt * w[2:3] + bias
        return carry

    lax.fori_loop(0, n2, body, 0)


def _shortconv(p3, conv_w, conv_b, c):
    bsz, seq, _ = p3.shape
    parts = conv_w.shape[1] // c
    cb_per_part = c // LANES
    nh = seq // DFT_N2
    return pl.pallas_call(
        functools.partial(_shortconv_kernel, seq=seq, n2=DFT_N2, nh=nh),
        grid=(bsz, parts * cb_per_part),
        in_specs=[
            pl.BlockSpec((None, seq, LANES), lambda b, j: (b, 0, j)),
            pl.BlockSpec((3, LANES), lambda b, j: (0, j)),
            pl.BlockSpec((1, LANES), lambda b, j: (0, j)),
        ],
        out_specs=pl.BlockSpec((None, None, seq, LANES), lambda b, j: (j // cb_per_part, b, 0, j % cb_per_part)),
        out_shape=jax.ShapeDtypeStruct((parts, bsz, seq, c), F32),
        scratch_shapes=[pltpu.VMEM((seq + 16, LANES), F32)],
        compiler_params=_cparams("parallel", "parallel"),
        name="shortconv",
    )(p3, conv_w, conv_b)


def _filter_kernel(w1t_ref, w1c_ref, w1s_ref, b1_ref, f1_ref, w2_ref, b2_ref, f2_ref, w3_ref, bias_ref, o_ref,
                   *, seq, tl, c, bands):
    hi = lax.Precision.HIGHEST
    row = lax.broadcasted_iota(jnp.int32, (tl, 1), 0) + pl.program_id(0) * tl
    pos = row.astype(F32)
    t = pos / (seq - 1.0)
    w = (2.0 * math.pi) * pos / float(seq)
    band = lax.broadcasted_iota(jnp.int32, (1, bands), 1).astype(F32)
    fr = 1e-4 + band * ((bands - 1 - 1e-4) / (bands - 1))
    ang = w * fr
    pre = (t * w1t_ref[...]
           + jnp.dot(jnp.cos(ang), w1c_ref[...], preferred_element_type=F32, precision=hi)
           - jnp.dot(jnp.sin(ang), w1s_ref[...], preferred_element_type=F32, precision=hi)
           + b1_ref[...])
    h = jnp.sin(f1_ref[...] * pre)
    h = jnp.sin(f2_ref[...] * (jnp.dot(h, w2_ref[...], preferred_element_type=F32, precision=hi) + b2_ref[...]))
    h = jnp.dot(h, w3_ref[...], preferred_element_type=F32, precision=hi)
    min_decay = math.log(HY_TARGET) / HY_FAST_DECAY
    max_decay = math.log(HY_TARGET) / HY_SLOW_DECAY
    ch = lax.broadcasted_iota(jnp.int32, (1, c), 1).astype(F32)
    deltas = jnp.abs(min_decay + ch * ((max_decay - min_decay) / (c - 1)))
    decay = jnp.exp(-t * deltas)
    first = row == 0
    orders = o_ref.shape[0]
    for o in range(orders):
        fwd = h[:, (2 * o) * c:(2 * o + 1) * c] * decay
        bwd = h[:, (2 * o + 1) * c:(2 * o + 2) * c] * decay
        o_ref[o, 0] = jnp.where(first, fwd + bias_ref[o:o + 1, :], fwd)
        o_ref[o, 1] = jnp.where(first, 0.0, bwd)


def _filters(w1, b1, f1, w2, b2, f2, w3, bias, seq, tl=512):
    emb, ffn = w1.shape
    bands = (emb - 1) // 2
    orders, c = bias.shape
    full = lambda a: pl.BlockSpec(a.shape, lambda i: (0,) * a.ndim)
    args = (w1[0:1], w1[1:1 + bands], w1[1 + bands:], b1[None], f1[None], w2, b2[None], f2[None], w3, bias)
    return pl.pallas_call(
        functools.partial(_filter_kernel, seq=seq, tl=tl, c=c, bands=bands),
        grid=(seq // tl,),
        in_specs=[full(a) for a in args],
        out_specs=pl.BlockSpec((orders, 2, tl, c), lambda i: (0, 0, i, 0)),
        out_shape=jax.ShapeDtypeStruct((orders, 2, seq, c), F32),
        compiler_params=_cparams("parallel"),
        name="hyena_filters",
    )(*args)


def _stack_complex(m):
    return np.block([[m.real, -m.imag], [m.imag, m.real]])


@functools.lru_cache(maxsize=None)
def _dft_constants(seq):
    n1, n2 = DFT_N1, DFT_N2
    n = n1 * n2
    assert n == 2 * seq
    nh = seq // n2
    k1 = np.arange(n1)[:, None].astype(np.float64)
    q = np.arange(nh)[None, :].astype(np.float64)
    ma = np.empty((n2, 2 * n1, 2 * nh), np.float64)
    mai = np.empty((n2, 2 * nh, 2 * n1), np.float64)
    for r in range(n2):
        e = np.exp(-2j * np.pi * (q * k1 / n1 + r * k1 / n))
        ma[r] = _stack_complex(e)
        mai[r] = _stack_complex(np.conj(e).T / n)
    kk = np.arange(n2)[:, None].astype(np.float64)
    rr = np.arange(n2)[None, :].astype(np.float64)
    f = np.exp(-2j * np.pi * kk * rr / n2)
    mb = _stack_complex(f)
    mbi = _stack_complex(np.conj(f).T)
    return tuple(np.asarray(a, np.float32) for a in (ma, mb, mbi, mai))


def _stage_a(load_x, ma_ref, a_sc):
    rows = 2 * DFT_N1

    def body(r, carry):
        a_sc[pl.ds(pl.multiple_of(r * rows, rows), rows), :] = jnp.dot(
            ma_ref[r], load_x(r), preferred_element_type=F32)
        return carry

    lax.fori_loop(0, DFT_N2, body, 0)


def _load_a_column(a_sc, k1):
    ar = a_sc[pl.ds(k1, DFT_N2, stride=2 * DFT_N1), :]
    ai = a_sc[pl.ds(DFT_N1 + k1, DFT_N2, stride=2 * DFT_N1), :]
    return jnp.concatenate([ar, ai], axis=0).astype(BF16)


def _spectrum_kernel(h_ref, ma_ref, mb_ref, o_ref, a_sc, *, nh):
    cb = o_ref.shape[-1]
    rows = 2 * DFT_N2
    sign = jnp.where(lax.broadcasted_iota(jnp.int32, (rows, 1), 0) < DFT_N2, 1.0, -1.0).astype(F32)
    for sig in range(2):
        def load_x(r, sig=sig):
            xr = h_ref[sig, pl.ds(r, nh, stride=DFT_N2), :]
            return jnp.concatenate([xr, jnp.zeros_like(xr)], axis=0).astype(BF16)

        _stage_a(load_x, ma_ref, a_sc)

        def body(k1, carry, sig=sig):
            z = jnp.dot(mb_ref[...], _load_a_column(a_sc, k1), preferred_element_type=F32)
            dst = pl.ds(pl.multiple_of(k1 * rows, rows), rows)
            if sig == 0:
                o_ref[dst, :] = z
            else:
                o_ref[dst, :] = o_ref[dst, :] + sign * z
            return carry

        lax.fori_loop(0, DFT_N1, body, 0)


def _spectrum(hsig, ma, mb):
    orders, _, seq, c = hsig.shape
    nh = seq // DFT_N2
    rows = 2 * DFT_N1 * DFT_N2
    return pl.pallas_call(
        functools.partial(_spectrum_kernel, nh=nh),
        grid=(orders, c // LANES),
        in_specs=[
            pl.BlockSpec((None, 2, seq, LANES), lambda o, j: (o, 0, 0, j)),
            _single(ma.shape, lambda o, j: (0, 0, 0)),
            _single(mb.shape, lambda o, j: (0, 0)),
        ],
        out_specs=pl.BlockSpec((None, rows, LANES), lambda o, j: (o, 0, j)),
        out_shape=jax.ShapeDtypeStruct((orders, rows, c), F32),
        scratch_shapes=[pltpu.VMEM((rows, LANES), F32)],
        compiler_params=_cparams("parallel", "parallel"),
        name="hyena_spectrum",
    )(hsig, ma, mb)


def _longconv_kernel(z_ref, gate_ref, h_ref, ma_ref, mb_ref, mbi_ref, mai_ref, o_ref, a_sc, v_sc,
                     *, nh, natural_out):
    def load_x(r):
        src = pl.ds(pl.multiple_of(r * nh, nh), nh)
        return jnp.concatenate([z_ref[0, src, :], z_ref[1, src, :]], axis=0).astype(BF16)

    _stage_a(load_x, ma_ref, a_sc)

    rows = 2 * DFT_N2

    def freq(k1, carry):
        zf = jnp.dot(mb_ref[...], _load_a_column(a_sc, k1), preferred_element_type=F32)
        base = pl.multiple_of(k1 * rows, rows)
        hr = h_ref[pl.ds(base, DFT_N2), :]
        hi = h_ref[pl.ds(base + DFT_N2, DFT_N2), :]
        zr, zi = zf[:DFT_N2], zf[DFT_N2:]
        y = jnp.concatenate([zr * hr - zi * hi, zr * hi + zi * hr], axis=0).astype(BF16)
        v_sc[pl.ds(base, rows), :] = jnp.dot(mbi_ref[...], y, preferred_element_type=F32)
        return carry

    lax.fori_loop(0, DFT_N1, freq, 0)

    def back(r, carry):
        vr = v_sc[pl.ds(r, DFT_N1, stride=rows), :]
        vi = v_sc[pl.ds(DFT_N2 + r, DFT_N1, stride=rows), :]
        y = jnp.dot(mai_ref[r], jnp.concatenate([vr, vi], axis=0).astype(BF16), preferred_element_type=F32)
        src = pl.ds(pl.multiple_of(r * nh, nh), nh)
        dst = pl.ds(r, nh, stride=DFT_N2) if natural_out else src
        o_ref[0, dst, :] = y[:nh] * gate_ref[0, src, :]
        o_ref[1, dst, :] = y[nh:] * gate_ref[1, src, :]
        return carry

    lax.fori_loop(0, DFT_N2, back, 0)


def _longconv(z4, z_part, gate4, gate_part, h3, order, consts, natural_out):
    _, bsz, seq, c = z4.shape
    ma, mb, mbi, mai = consts
    nh = seq // DFT_N2
    rows = 2 * DFT_N1 * DFT_N2
    pair = lambda part: (lambda j, p: (part, p, 0, j))
    return pl.pallas_call(
        functools.partial(_longconv_kernel, nh=nh, natural_out=natural_out),
        grid=(c // LANES, bsz // 2),
        in_specs=[
            _single((None, 2, seq, LANES), pair(z_part)),
            _single((None, 2, seq, LANES), pair(gate_part)),
            _single((None, rows, LANES), lambda j, p: (order, 0, j)),
            _single(ma.shape, lambda j, p: (0, 0, 0)),
            _single(mb.shape, lambda j, p: (0, 0)),
            _single(mbi.shape, lambda j, p: (0, 0)),
            _single(mai.shape, lambda j, p: (0, 0, 0)),
        ],
        out_specs=pl.BlockSpec((None, 2, seq, LANES), pair(0)),
        out_shape=jax.ShapeDtypeStruct((1, bsz, seq, c), F32),
        scratch_shapes=[pltpu.VMEM((rows, LANES), F32), pltpu.VMEM((rows, LANES), F32)],
        compiler_params=_cparams("parallel", "parallel"),
        name="hyena_longconv",
    )(z4, gate4, h3, ma, mb, mbi, mai)


def _rope_kernel(q_ref, k_ref, v_ref, cos_ref, sin_ref, qo_ref, kt_ref, vo_ref, *, half, scale):
    cos = cos_ref[...]
    sin = sin_ref[...]
    lane = lax.broadcasted_iota(jnp.int32, (1, LANES), 1)
    first_half = (lane % (2 * half)) < half

    def rot(x):
        outs = []
        for j in range(x.shape[1] // LANES):
            xb = x[:, j * LANES:(j + 1) * LANES]
            partner = jnp.where(first_half, pltpu.roll(xb, LANES - half, axis=1), pltpu.roll(xb, half, axis=1))
            outs.append(xb * cos + partner * sin)
        return jnp.concatenate(outs, axis=1)

    qo_ref[...] = (rot(q_ref[...]) * scale).astype(BF16)
    kt_ref[...] = rot(k_ref[...]).T.astype(BF16)
    vo_ref[...] = v_ref[...].astype(BF16)


def _rope(p2, bsz, seq, qk_width, v_width, head_dim, col_q, tm=512):
    t = p2.shape[0]
    assert qk_width == v_width and col_q % qk_width == 0
    jq = col_q // qk_width
    half = head_dim // 2
    inv = ROPE_THETA ** (-jnp.arange(half, dtype=F32) * 2.0 / head_dim)
    ang = jnp.arange(seq, dtype=F32)[:, None] * inv[None, :]
    cos, sin = jnp.cos(ang), jnp.sin(ang)
    reps = LANES // head_dim
    cos_t = jnp.tile(jnp.concatenate([cos, cos], axis=1), (1, reps))
    sin_t = jnp.tile(jnp.concatenate([-sin, sin], axis=1), (1, reps))
    ns = seq // tm
    return pl.pallas_call(
        functools.partial(_rope_kernel, half=half, scale=head_dim ** -0.5),
        grid=(t // tm,),
        in_specs=[
            pl.BlockSpec((tm, qk_width), lambda i: (i, jq)),
            pl.BlockSpec((tm, qk_width), lambda i: (i, jq + 1)),
            pl.BlockSpec((tm, v_width), lambda i: (i, jq + 2)),
            pl.BlockSpec((tm, LANES), lambda i: (i % ns, 0)),
            pl.BlockSpec((tm, LANES), lambda i: (i % ns, 0)),
        ],
        out_specs=[
            pl.BlockSpec((tm, qk_width), lambda i: (i, 0)),
            pl.BlockSpec((None, qk_width, tm), lambda i: (i // ns, 0, i % ns)),
            pl.BlockSpec((tm, v_width), lambda i: (i, 0)),
        ],
        out_shape=[
            jax.ShapeDtypeStruct((t, qk_width), BF16),
            jax.ShapeDtypeStruct((bsz, qk_width, seq), BF16),
            jax.ShapeDtypeStruct((t, v_width), BF16),
        ],
        compiler_params=_cparams("parallel"),
        name="rope",
    )(p2, p2, p2, cos_t, sin_t)


def _attn_kernel(q_ref, kt_ref, v_ref, lq1_ref, lk1_ref, lq2_ref, lk2_ref, g_ref, o_ref, *, head_dim, lam_init):
    lam = (jnp.exp(jnp.sum(lq1_ref[...] * lk1_ref[...], axis=-1, keepdims=True))
           - jnp.exp(jnp.sum(lq2_ref[...] * lk2_ref[...], axis=-1, keepdims=True)) + lam_init)
    q = q_ref[...]
    kt = kt_ref[...]
    v = v_ref[...]
    lane = lax.broadcasted_iota(jnp.int32, (1, q.shape[1]), 1)

    def component(mask):
        s = jnp.dot(jnp.where(mask, q, jnp.zeros_like(q)), kt, preferred_element_type=F32)
        e = jnp.exp(s - jnp.max(s, axis=-1, keepdims=True))
        l = jnp.sum(e, axis=-1, keepdims=True)
        return jnp.dot(e.astype(BF16), v, preferred_element_type=F32) / l

    o = component(lane < head_dim) - lam * component(lane >= head_dim)
    o = o * lax.rsqrt(jnp.mean(o * o, axis=-1, keepdims=True) + SUBLN_EPS) * g_ref[...]
    o_ref[...] = o * (1.0 - lam_init)


def _attention(q, kt, v, lq1, lk1, lq2, lk2, subln_g, head_dim, lam_init, tq=256):
    bsz, seq, width = q.shape
    v_dim = subln_g.shape[-1]
    assert v_dim == 2 * head_dim == LANES
    heads = width // v_dim
    vec = lambda a: pl.BlockSpec((1, a.shape[-1]), lambda b, h, i: (0, 0))
    lams = [a[None] for a in (lq1, lk1, lq2, lk2)]
    return pl.pallas_call(
        functools.partial(_attn_kernel, head_dim=head_dim, lam_init=lam_init),
        grid=(bsz, heads, seq // tq),
        in_specs=[
            pl.BlockSpec((None, tq, v_dim), lambda b, h, i: (b, i, h)),
            pl.BlockSpec((None, v_dim, seq), lambda b, h, i: (b, h, 0)),
            pl.BlockSpec((None, seq, v_dim), lambda b, h, i: (b, 0, h)),
            *[vec(a) for a in lams],
            vec(subln_g[None]),
        ],
        out_specs=pl.BlockSpec((None, tq, v_dim), lambda b, h, i: (b, i, h)),
        out_shape=jax.ShapeDtypeStruct((bsz, seq, width), F32),
        compiler_params=_cparams("parallel", "parallel", "parallel"),
        name="diff_attention",
    )(q, kt, v, *lams, subln_g[None])


def _merge_kernel(x_ref, yh_ref, ya_ref, gh_ref, ga_ref, wuh_ref, wua_ref, wo_ref, g_ref, wr_ref, br_ref,
                  xo_ref, n_ref, aff_ref):
    mh = jnp.dot(yh_ref[...].astype(BF16), wuh_ref[...], preferred_element_type=F32)
    ma = jnp.dot(ya_ref[...].astype(BF16), wua_ref[...], preferred_element_type=F32)
    merged = jax.nn.sigmoid(gh_ref[...]) * mh + jax.nn.sigmoid(ga_ref[...]) * ma
    x = x_ref[...] + jnp.dot(merged.astype(BF16), wo_ref[...], preferred_element_type=F32)
    xo_ref[...] = x
    n = x * lax.rsqrt(jnp.mean(x * x, axis=-1, keepdims=True) + NORM_EPS) * g_ref[...]
    n_ref[...] = n.astype(BF16)
    logits = lax.dot_general(wr_ref[...], n, (((1,), (1,)), ((), ())), preferred_element_type=F32,
                             precision=lax.Precision.HIGHEST) + br_ref[...]
    e = jnp.exp(logits - jnp.max(logits, axis=0, keepdims=True))
    aff_ref[...] = e / jnp.sum(e, axis=0, keepdims=True)


def _merge(x2, yh2, ya2, p2, col_gate, wuh, wua, wo, g, wr_t, br, bsz, seq, tm=512):
    t, d = x2.shape
    c = yh2.shape[1]
    e = wr_t.shape[0]
    jg = col_gate // d
    ns = seq // tm
    const = lambda a: pl.BlockSpec(a.shape, lambda i: (0,) * a.ndim)
    return pl.pallas_call(
        _merge_kernel,
        grid=(t // tm,),
        in_specs=[
            pl.BlockSpec((tm, d), lambda i: (i, 0)),
            pl.BlockSpec((tm, c), lambda i: (i, 0)),
            pl.BlockSpec((tm, ya2.shape[1]), lambda i: (i, 0)),
            pl.BlockSpec((tm, d), lambda i: (i, jg)),
            pl.BlockSpec((tm, d), lambda i: (i, jg + 1)),
            const(wuh), const(wua), const(wo), const(g), const(wr_t), const(br),
        ],
        out_specs=[
            pl.BlockSpec((tm, d), lambda i: (i, 0)),
            pl.BlockSpec((tm, d), lambda i: (i, 0)),
            pl.BlockSpec((None, e, tm), lambda i: (i // ns, 0, i % ns)),
        ],
        out_shape=[
            jax.ShapeDtypeStruct((t, d), F32),
            jax.ShapeDtypeStruct((t, d), BF16),
            jax.ShapeDtypeStruct((bsz, e, seq), F32),
        ],
        compiler_params=_cparams("parallel"),
        name="merge_router",
    )(x2, yh2, ya2, p2, p2, wuh, wua, wo, g, wr_t, br)


def _select_kernel(aff_ref, pos_ref, *, cap):
    a = aff_ref[...]
    rows, seq = a.shape
    as_f32 = lambda b: lax.bitcast_convert_type(b, F32)
    count = lambda m: jnp.sum(jnp.where(m, 1.0, 0.0), axis=-1, keepdims=True)
    thr = jnp.zeros((rows, 1), jnp.int32)
    for bit in range(30, -1, -1):
        cand = thr | (1 << bit)
        thr = jnp.where(count(a >= as_f32(cand)) >= cap, cand, thr)
    gt = a >= as_f32(thr + 1)
    eq = (a >= as_f32(thr)) & jnp.logical_not(gt)
    need = cap - count(gt)
    tri = jnp.where(lax.broadcasted_iota(jnp.int32, (LANES, LANES), 0)
                    <= lax.broadcasted_iota(jnp.int32, (LANES, LANES), 1), 1.0, 0.0).astype(BF16)

    def exclusive_cumsum(mask):
        ones = jnp.where(mask, 1.0, 0.0)
        carry = jnp.zeros((rows, 1), F32)
        chunks = []
        for j in range(seq // LANES):
            blk = ones[:, j * LANES:(j + 1) * LANES]
            incl = jnp.dot(blk.astype(BF16), tri, preferred_element_type=F32)
            chunks.append(incl - blk + carry)
            carry = carry + jnp.sum(blk, axis=-1, keepdims=True)
        return jnp.concatenate(chunks, axis=1)

    sel = gt | (eq & (exclusive_cumsum(eq) < need))
    pos_ref[...] = jnp.where(sel, exclusive_cumsum(sel), -1.0).astype(jnp.int32)


def _select(aff_rows, cap):
    return pl.pallas_call(
        functools.partial(_select_kernel, cap=cap),
        out_shape=jax.ShapeDtypeStruct(aff_rows.shape, jnp.int32),
        compiler_params=pltpu.CompilerParams(vmem_limit_bytes=VMEM_LIMIT_V7X),
        name="expert_select",
    )(aff_rows)


def _gather_kernel(pos_ref, n_ref, o_ref, *, ts):
    cap = o_ref.shape[0]
    seq = n_ref.shape[0]
    slot = lax.broadcasted_iota(jnp.int32, (cap, ts), 0)
    acc = jnp.zeros(o_ref.shape, F32)
    for j in range(seq // ts):
        onehot = (slot == pos_ref[:, j * ts:(j + 1) * ts]).astype(BF16)
        acc = acc + jnp.dot(onehot, n_ref[pl.ds(j * ts, ts), :], preferred_element_type=F32)
    o_ref[...] = acc.astype(BF16)


def _gather(pos4, n3, cap, ts=512):
    bsz, e, _, seq = pos4.shape
    d = n3.shape[-1]
    return pl.pallas_call(
        functools.partial(_gather_kernel, ts=ts),
        grid=(bsz, e),
        in_specs=[
            pl.BlockSpec((None, None, 1, seq), lambda b, x: (b, x, 0, 0)),
            pl.BlockSpec((None, seq, d), lambda b, x: (b, 0, 0)),
        ],
        out_specs=pl.BlockSpec((None, None, cap, d), lambda b, x: (b, x, 0, 0)),
        out_shape=jax.ShapeDtypeStruct((bsz, e, cap, d), BF16),
        compiler_params=_cparams("parallel", "parallel"),
        name="expert_gather",
    )(pos4, n3)


def _expert_kernel(x_ref, wg_ref, wu_ref, wd_ref, o_ref):
    x = x_ref[...]
    g = jnp.dot(x, wg_ref[...], preferred_element_type=F32)
    u = jnp.dot(x, wu_ref[...], preferred_element_type=F32)
    h = (g * jax.nn.sigmoid(g) * u).astype(BF16)
    o_ref[...] = jnp.dot(h, wd_ref[...], preferred_element_type=F32).astype(BF16)


def _experts(xg, wg, wu, wd):
    bsz, e, cap, d = xg.shape
    f = wg.shape[-1]
    tok = pl.BlockSpec((None, None, cap, d), lambda x, b: (b, x, 0, 0))
    return pl.pallas_call(
        _expert_kernel,
        grid=(e, bsz),
        in_specs=[
            tok,
            pl.BlockSpec((None, d, f), lambda x, b: (x, 0, 0)),
            pl.BlockSpec((None, d, f), lambda x, b: (x, 0, 0)),
            pl.BlockSpec((None, f, d), lambda x, b: (x, 0, 0)),
        ],
        out_specs=tok,
        out_shape=jax.ShapeDtypeStruct((bsz, e, cap, d), BF16),
        compiler_params=_cparams("parallel", "parallel"),
        name="expert_ffn",
    )(xg, wg, wu, wd)


def _combine_kernel(x_ref, pos_ref, aff_ref, ye_ref, g_ref, o_ref, *, final):
    ts = x_ref.shape[0]
    e, cap, _ = ye_ref.shape
    slot = lax.broadcasted_iota(jnp.int32, (ts, cap), 1)
    pos = pos_ref[...]
    aff = aff_ref[...]
    acc = x_ref[...]
    for x in range(e):
        onehot = (slot == pos[:, x:x + 1]).astype(BF16)
        acc = acc + aff[:, x:x + 1] * jnp.dot(onehot, ye_ref[x], preferred_element_type=F32)
    if final:
        acc = acc * lax.rsqrt(jnp.mean(acc * acc, axis=-1, keepdims=True) + NORM_EPS) * g_ref[...]
    o_ref[...] = acc


def _combine(x3, pos_t, aff_t, ye, g, final, ts=512):
    bsz, seq, d = x3.shape
    e, cap = ye.shape[1], ye.shape[2]
    return pl.pallas_call(
        functools.partial(_combine_kernel, final=final),
        grid=(bsz, seq // ts),
        in_specs=[
            pl.BlockSpec((None, ts, d), lambda b, i: (b, i, 0)),
            pl.BlockSpec((None, ts, e), lambda b, i: (b, i, 0)),
            pl.BlockSpec((None, ts, e), lambda b, i: (b, i, 0)),
            _single((None, e, cap, d), lambda b, i: (b, 0, 0, 0)),
            pl.BlockSpec((1, d), lambda b, i: (0, 0)),
        ],
        out_specs=pl.BlockSpec((None, ts, d), lambda b, i: (b, i, 0)),
        out_shape=jax.ShapeDtypeStruct((bsz, seq, d), F32),
        compiler_params=_cparams("parallel", "parallel"),
        name="expert_combine",
    )(x3, pos_t, aff_t, ye, g)


def kernel(x, norm_mix, w_in, b_in, hy_conv_w, hy_conv_b, hy_ffn_w1, hy_ffn_b1, hy_ffn_f1, hy_ffn_w2, hy_ffn_b2, hy_ffn_f2, hy_ffn_w3, hy_bias, lambda_q1, lambda_k1, lambda_q2, lambda_k2, subln_g, w_up_hyena, w_up_attn, w_out, norm_ffn, w_router, b_router, w_e_gate, w_e_up, w_e_down, norm_final):
    bsz, seq, d = x.shape
    depth = w_in.shape[0]
    orders, c = hy_bias.shape[1], hy_bias.shape[2]
    head_dim = lambda_q1.shape[1]
    v_width = w_up_attn.shape[1]
    qk_width = v_width
    e = w_router.shape[2]
    cap = EC_FACTOR * seq // e
    col_q = (orders + 1) * c
    col_gate = col_q + 2 * qk_width + v_width
    assert orders == 2 and bsz % 2 == 0 and col_gate % d == 0

    consts = tuple(jnp.asarray(a, F32).astype(BF16) for a in _dft_constants(seq))
    ma, mb = consts[0], consts[1]

    xs = x.reshape(bsz * seq, d)
    out = None
    for l in range(depth):
        p2 = _inproj(xs, norm_mix[l][None], w_in[l].astype(BF16), b_in[l][None])
        p3 = p2.reshape(bsz, seq, -1)

        uc = _shortconv(p3, hy_conv_w[l], hy_conv_b[l][None], c)
        hsig = _filters(hy_ffn_w1[l], hy_ffn_b1[l], hy_ffn_f1[l], hy_ffn_w2[l], hy_ffn_b2[l], hy_ffn_f2[l],
                        hy_ffn_w3[l], hy_bias[l], seq)
        hspec = _spectrum(hsig, ma, mb)
        z = _longconv(uc, 0, uc, 1, hspec, 0, consts, natural_out=False)
        y_hy = _longconv(z, 0, uc, 2, hspec, 1, consts, natural_out=True)

        q_r, k_t, v_b = _rope(p2, bsz, seq, qk_width, v_width, head_dim, col_q)
        lam_init = 0.8 - 0.6 * math.exp(-0.3 * l)
        y_da = _attention(q_r.reshape(bsz, seq, qk_width), k_t, v_b.reshape(bsz, seq, v_width),
                          lambda_q1[l], lambda_k1[l], lambda_q2[l], lambda_k2[l], subln_g[l], head_dim, lam_init)

        xs, n2, aff = _merge(xs, y_hy.reshape(bsz * seq, c), y_da.reshape(bsz * seq, v_width), p2, col_gate,
                             w_up_hyena[l].astype(BF16), w_up_attn[l].astype(BF16), w_out[l].astype(BF16),
                             norm_ffn[l][None], w_router[l].T, b_router[l][:, None], bsz, seq)

        pos = _select(aff.reshape(bsz * e, seq), cap).reshape(bsz, e, seq)
        xg = _gather(pos.reshape(bsz, e, 1, seq), n2.reshape(bsz, seq, d), cap)
        ye = _experts(xg, w_e_gate[l].astype(BF16), w_e_up[l].astype(BF16), w_e_down[l].astype(BF16))
        final = l == depth - 1
        out = _combine(xs.reshape(bsz, seq, d), pos.transpose(0, 2, 1), aff.transpose(0, 2, 1), ye,
                       norm_final[None], final)
        xs = out.reshape(bsz * seq, d)
    return out
```

```python
import functools
import math

import numpy as np
import jax
import jax.numpy as jnp
from jax import lax
from jax.experimental import pallas as pl
from jax.experimental.pallas import tpu as pltpu

F32 = jnp.float32
BF16 = jnp.bfloat16

NORM_EPS = 1e-6
SUBLN_EPS = 1e-5
ROPE_THETA = 10000.0
HY_FAST_DECAY = 0.3
HY_SLOW_DECAY = 1.5
HY_TARGET = 1e-2
EC_FACTOR = 2

VMEM_LIMIT_V7X = 56 * 1024 * 1024
LANES = 128

DFT_N1 = 64
DFT_N2 = 128
UNROLL_STAGE_A = 4
UNROLL_STAGE_B = 2
STAGE_B_COLS = 2
UNROLL_SHORTCONV = 4


def _cparams(*sem):
    return pltpu.CompilerParams(dimension_semantics=sem, vmem_limit_bytes=VMEM_LIMIT_V7X)


def _single(block_shape, index_map):
    return pl.BlockSpec(block_shape, index_map, pipeline_mode=pl.Buffered(1))


def _inproj_kernel(x_ref, g_ref, w_ref, b_ref, o_ref, n_sc):
    @pl.when(pl.program_id(1) == 0)
    def _():
        x = x_ref[...]
        n = x * lax.rsqrt(jnp.mean(x * x, axis=-1, keepdims=True) + NORM_EPS) * g_ref[...]
        n_sc[...] = n.astype(BF16)

    o_ref[...] = jnp.dot(n_sc[...], w_ref[...], preferred_element_type=F32) + b_ref[...]


def _inproj(x2, g, w_bf, b, tm=1024, tn=1024):
    t, d = x2.shape
    width = w_bf.shape[1]
    return pl.pallas_call(
        _inproj_kernel,
        grid=(t // tm, width // tn),
        in_specs=[
            pl.BlockSpec((tm, d), lambda i, j: (i, 0)),
            pl.BlockSpec((1, d), lambda i, j: (0, 0)),
            pl.BlockSpec((d, tn), lambda i, j: (0, j)),
            pl.BlockSpec((1, tn), lambda i, j: (0, j)),
        ],
        out_specs=pl.BlockSpec((tm, tn), lambda i, j: (i, j)),
        out_shape=jax.ShapeDtypeStruct((t, width), F32),
        scratch_shapes=[pltpu.VMEM((tm, d), BF16)],
        compiler_params=_cparams("parallel", "arbitrary"),
        name="inproj",
    )(x2, g, w_bf, b)


def _shortconv_kernel(u_ref, w_ref, b_ref, o_ref, pad_sc, *, seq, n2, nh):
    zeros = jnp.zeros((8, LANES), F32)
    pad_sc[pl.ds(0, 8), :] = zeros
    pad_sc[pl.ds(8 + seq, 8), :] = zeros
    pad_sc[pl.ds(8, seq), :] = u_ref[...]
    w = w_ref[...]
    bias = b_ref[...]

    def body(r, carry):
        prev = pad_sc[pl.ds(7 + r, nh, stride=n2), :]
        cur = pad_sc[pl.ds(8 + r, nh, stride=n2), :]
        nxt = pad_sc[pl.ds(9 + r, nh, stride=n2), :]
        o_ref[pl.ds(pl.multiple_of(r * nh, nh), nh), :] = prev * w[0:1] + cur * w[1:2] + nxt * w[2:3] + bias
        return carry

    lax.fori_loop(0, n2, body, 0, unroll=UNROLL_SHORTCONV)


def _shortconv(p3, conv_w, conv_b, c):
    bsz, seq, _ = p3.shape
    parts = conv_w.shape[1] // c
    cb_per_part = c // LANES
    nh = seq // DFT_N2
    return pl.pallas_call(
        functools.partial(_shortconv_kernel, seq=seq, n2=DFT_N2, nh=nh),
        grid=(bsz, parts * cb_per_part),
        in_specs=[
            pl.BlockSpec((None, seq, LANES), lambda b, j: (b, 0, j)),
            pl.BlockSpec((3, LANES), lambda b, j: (0, j)),
            pl.BlockSpec((1, LANES), lambda b, j: (0, j)),
        ],
        out_specs=pl.BlockSpec((None, None, seq, LANES), lambda b, j: (j // cb_per_part, b, 0, j % cb_per_part)),
        out_shape=jax.ShapeDtypeStruct((parts, bsz, seq, c), F32),
        scratch_shapes=[pltpu.VMEM((seq + 16, LANES), F32)],
        compiler_params=_cparams("parallel", "parallel"),
        name="shortconv",
    )(p3, conv_w, conv_b)


def _filter_kernel(w1t_ref, w1c_ref, w1s_ref, b1_ref, f1_ref, w2_ref, b2_ref, f2_ref, w3_ref, bias_ref, o_ref,
                   *, seq, tl, c, bands):
    hi = lax.Precision.HIGHEST
    row = lax.broadcasted_iota(jnp.int32, (tl, 1), 0) + pl.program_id(0) * tl
    pos = jnp.where(row < seq, row, 2 * seq - row).astype(F32)
    t = pos / (seq - 1.0)
    w = (2.0 * math.pi) * pos / float(seq)
    band = lax.broadcasted_iota(jnp.int32, (1, bands), 1).astype(F32)
    fr = 1e-4 + band * ((bands - 1 - 1e-4) / (bands - 1))
    ang = w * fr
    pre = (t * w1t_ref[...]
           + jnp.dot(jnp.cos(ang), w1c_ref[...], preferred_element_type=F32, precision=hi)
           - jnp.dot(jnp.sin(ang), w1s_ref[...], preferred_element_type=F32, precision=hi)
           + b1_ref[...])
    h = jnp.sin(f1_ref[...] * pre)
    h = jnp.sin(f2_ref[...] * (jnp.dot(h, w2_ref[...], preferred_element_type=F32, precision=hi) + b2_ref[...]))
    h = jnp.dot(h, w3_ref[...], preferred_element_type=F32, precision=hi)
    min_decay = math.log(HY_TARGET) / HY_FAST_DECAY
    max_decay = math.log(HY_TARGET) / HY_SLOW_DECAY
    ch = lax.broadcasted_iota(jnp.int32, (1, c), 1).astype(F32)
    deltas = jnp.abs(min_decay + ch * ((max_decay - min_decay) / (c - 1)))
    decay = jnp.exp(-t * deltas)
    orders = o_ref.shape[0]
    for o in range(orders):
        taps = h[:, o * c:(o + 1) * c] * decay
        taps = jnp.where(row == 0, taps + bias_ref[o:o + 1, :], taps)
        o_ref[o] = jnp.where(row == seq, 0.0, taps)


def _filters(w1, b1, f1, w2, b2, f2, w3, bias, seq, tl=512):
    emb, ffn = w1.shape
    bands = (emb - 1) // 2
    orders, c = bias.shape
    tiles = seq // tl
    w3_dir = w3.reshape(ffn, orders, 2, c).transpose(2, 0, 1, 3).reshape(2, ffn, orders * c)
    full = lambda a: pl.BlockSpec(a.shape, lambda i: (0,) * a.ndim)
    args = (w1[0:1], w1[1:1 + bands], w1[1 + bands:], b1[None], f1[None], w2, b2[None], f2[None], w3_dir, bias)
    in_specs = [full(a) for a in args]
    in_specs[8] = pl.BlockSpec((None, ffn, orders * c), lambda i: (i // tiles, 0, 0))
    return pl.pallas_call(
        functools.partial(_filter_kernel, seq=seq, tl=tl, c=c, bands=bands),
        grid=(2 * tiles,),
        in_specs=in_specs,
        out_specs=pl.BlockSpec((orders, tl, c), lambda i: (0, i, 0)),
        out_shape=jax.ShapeDtypeStruct((orders, 2 * seq, c), F32),
        compiler_params=_cparams("parallel"),
        name="hyena_filters",
    )(*args)


def _stack_complex(m):
    return np.block([[m.real, -m.imag], [m.imag, m.real]])


@functools.lru_cache(maxsize=None)
def _dft_constants(seq):
    n1, n2 = DFT_N1, DFT_N2
    n = n1 * n2
    assert n == 2 * seq
    nh = seq // n2
    k1 = np.arange(n1)[:, None].astype(np.float64)
    q = np.arange(nh)[None, :].astype(np.float64)
    qf = np.arange(n1)[None, :].astype(np.float64)
    ma = np.empty((n2, 2 * n1, 2 * nh), np.float64)
    mai = np.empty((n2, 2 * nh, 2 * n1), np.float64)
    maf = np.empty((n2, 2 * n1, n1), np.float64)
    for r in range(n2):
        e = np.exp(-2j * np.pi * (q * k1 / n1 + r * k1 / n))
        ma[r] = _stack_complex(e)
        mai[r] = _stack_complex(np.conj(e).T / n)
        ef = np.exp(-2j * np.pi * (qf * k1 / n1 + r * k1 / n))
        maf[r] = np.concatenate([ef.real, ef.imag], axis=0)
    kk = np.arange(n2)[:, None].astype(np.float64)
    rr = np.arange(n2)[None, :].astype(np.float64)
    f = np.exp(-2j * np.pi * kk * rr / n2)
    mb = _stack_complex(f)
    mbi = _stack_complex(np.conj(f).T)
    return tuple(np.asarray(a, np.float32) for a in (ma, mb, mbi, mai, maf))


def _stage_a(load_x, ma_ref, a_sc):
    rows = 2 * DFT_N1

    def body(r, carry):
        a_sc[pl.ds(pl.multiple_of(r * rows, rows), rows), :] = jnp.dot(
            ma_ref[r], load_x(r), preferred_element_type=F32)
        return carry

    lax.fori_loop(0, DFT_N2, body, 0, unroll=UNROLL_STAGE_A)


def _load_a_columns(a_sc, k1):
    cols = []
    for j in range(STAGE_B_COLS):
        ar = a_sc[pl.ds(k1 + j, DFT_N2, stride=2 * DFT_N1), :]
        ai = a_sc[pl.ds(DFT_N1 + k1 + j, DFT_N2, stride=2 * DFT_N1), :]
        cols.append(jnp.concatenate([ar, ai], axis=0))
    return jnp.concatenate(cols, axis=1).astype(BF16)


def _spectrum_kernel(h_ref, maf_ref, mb_ref, o_ref, a_sc):
    rows = 2 * DFT_N2

    def load_x(r):
        return h_ref[pl.ds(r, DFT_N1, stride=DFT_N2), :].astype(BF16)

    _stage_a(load_x, maf_ref, a_sc)

    cb = o_ref.shape[-1]

    def body(kp, carry):
        k1 = kp * STAGE_B_COLS
        z = jnp.dot(mb_ref[...], _load_a_columns(a_sc, k1), preferred_element_type=F32)
        for j in range(STAGE_B_COLS):
            o_ref[pl.ds(pl.multiple_of((k1 + j) * rows, rows), rows), :] = z[:, j * cb:(j + 1) * cb]
        return carry

    lax.fori_loop(0, DFT_N1 // STAGE_B_COLS, body, 0, unroll=UNROLL_STAGE_B)


def _spectrum(hfull, maf, mb):
    orders, n, c = hfull.shape
    rows = 2 * DFT_N1 * DFT_N2
    return pl.pallas_call(
        _spectrum_kernel,
        grid=(orders, c // LANES),
        in_specs=[
            pl.BlockSpec((None, n, LANES), lambda o, j: (o, 0, j)),
            _single(maf.shape, lambda o, j: (0, 0, 0)),
            _single(mb.shape, lambda o, j: (0, 0)),
        ],
        out_specs=pl.BlockSpec((None, rows, LANES), lambda o, j: (o, 0, j)),
        out_shape=jax.ShapeDtypeStruct((orders, rows, c), F32),
        scratch_shapes=[pltpu.VMEM((rows, LANES), F32)],
        compiler_params=_cparams("parallel", "parallel"),
        name="hyena_spectrum",
    )(hfull, maf, mb)


def _longconv_kernel(z_ref, gate_ref, h_ref, ma_ref, mb_ref, mbi_ref, mai_ref, o_ref, a_sc, v_sc,
                     *, nh, natural_out):
    def load_x(r):
        src = pl.ds(pl.multiple_of(r * nh, nh), nh)
        return jnp.concatenate([z_ref[0, src, :], z_ref[1, src, :]], axis=0).astype(BF16)

    _stage_a(load_x, ma_ref, a_sc)

    rows = 2 * DFT_N2

    cb = o_ref.shape[-1]

    def freq(kp, carry):
        k1 = kp * STAGE_B_COLS
        zf = jnp.dot(mb_ref[...], _load_a_columns(a_sc, k1), preferred_element_type=F32)
        ys = []
        for j in range(STAGE_B_COLS):
            base = pl.multiple_of((k1 + j) * rows, rows)
            hr = h_ref[pl.ds(base, DFT_N2), :]
            hi = h_ref[pl.ds(base + DFT_N2, DFT_N2), :]
            zr = zf[:DFT_N2, j * cb:(j + 1) * cb]
            zi = zf[DFT_N2:, j * cb:(j + 1) * cb]
            ys.append(jnp.concatenate([zr * hr - zi * hi, zr * hi + zi * hr], axis=0))
        v = jnp.dot(mbi_ref[...], jnp.concatenate(ys, axis=1).astype(BF16), preferred_element_type=F32)
        for j in range(STAGE_B_COLS):
            v_sc[pl.ds(pl.multiple_of((k1 + j) * rows, rows), rows), :] = v[:, j * cb:(j + 1) * cb]
        return carry

    lax.fori_loop(0, DFT_N1 // STAGE_B_COLS, freq, 0, unroll=UNROLL_STAGE_B)

    def back(r, carry):
        vr = v_sc[pl.ds(r, DFT_N1, stride=rows), :]
        vi = v_sc[pl.ds(DFT_N2 + r, DFT_N1, stride=rows), :]
        y = jnp.dot(mai_ref[r], jnp.concatenate([vr, vi], axis=0).astype(BF16), preferred_element_type=F32)
        src = pl.ds(pl.multiple_of(r * nh, nh), nh)
        dst = pl.ds(r, nh, stride=DFT_N2) if natural_out else src
        o_ref[0, dst, :] = y[:nh] * gate_ref[0, src, :]
        o_ref[1, dst, :] = y[nh:] * gate_ref[1, src, :]
        return carry

    lax.fori_loop(0, DFT_N2, back, 0, unroll=UNROLL_STAGE_A)


def _longconv(z4, z_part, gate4, gate_part, h3, order, consts, natural_out):
    _, bsz, seq, c = z4.shape
    ma, mb, mbi, mai = consts[:4]
    nh = seq // DFT_N2
    rows = 2 * DFT_N1 * DFT_N2
    pair = lambda part: (lambda j, p: (part, p, 0, j))
    return pl.pallas_call(
        functools.partial(_longconv_kernel, nh=nh, natural_out=natural_out),
        grid=(c // LANES, bsz // 2),
        in_specs=[
            _single((None, 2, seq, LANES), pair(z_part)),
            _single((None, 2, seq, LANES), pair(gate_part)),
            _single((None, rows, LANES), lambda j, p: (order, 0, j)),
            _single(ma.shape, lambda j, p: (0, 0, 0)),
            _single(mb.shape, lambda j, p: (0, 0)),
            _single(mbi.shape, lambda j, p: (0, 0)),
            _single(mai.shape, lambda j, p: (0, 0, 0)),
        ],
        out_specs=pl.BlockSpec((None, 2, seq, LANES), pair(0)),
        out_shape=jax.ShapeDtypeStruct((1, bsz, seq, c), F32),
        scratch_shapes=[pltpu.VMEM((rows, LANES), F32), pltpu.VMEM((rows, LANES), F32)],
        compiler_params=_cparams("parallel", "parallel"),
        name="hyena_longconv",
    )(z4, gate4, h3, ma, mb, mbi, mai)


def _rope_kernel(q_ref, k_ref, v_ref, cos_ref, sin_ref, qo_ref, kt_ref, vo_ref, *, half, scale):
    cos = cos_ref[...]
    sin = sin_ref[...]
    lane = lax.broadcasted_iota(jnp.int32, (1, LANES), 1)
    first_half = (lane % (2 * half)) < half

    def rot(x):
        outs = []
        for j in range(x.shape[1] // LANES):
            xb = x[:, j * LANES:(j + 1) * LANES]
            partner = jnp.where(first_half, pltpu.roll(xb, LANES - half, axis=1), pltpu.roll(xb, half, axis=1))
            outs.append(xb * cos + partner * sin)
        return jnp.concatenate(outs, axis=1)

    qo_ref[...] = (rot(q_ref[...]) * scale).astype(BF16)
    kt_ref[...] = rot(k_ref[...]).T.astype(BF16)
    vo_ref[...] = v_ref[...].astype(BF16)


def _rope(p2, bsz, seq, qk_width, v_width, head_dim, col_q, tm=512):
    t = p2.shape[0]
    assert qk_width == v_width and col_q % qk_width == 0
    jq = col_q // qk_width
    half = head_dim // 2
    inv = ROPE_THETA ** (-jnp.arange(half, dtype=F32) * 2.0 / head_dim)
    ang = jnp.arange(seq, dtype=F32)[:, None] * inv[None, :]
    cos, sin = jnp.cos(ang), jnp.sin(ang)
    reps = LANES // head_dim
    cos_t = jnp.tile(jnp.concatenate([cos, cos], axis=1), (1, reps))
    sin_t = jnp.tile(jnp.concatenate([-sin, sin], axis=1), (1, reps))
    ns = seq // tm
    return pl.pallas_call(
        functools.partial(_rope_kernel, half=half, scale=head_dim ** -0.5),
        grid=(t // tm,),
        in_specs=[
            pl.BlockSpec((tm, qk_width), lambda i: (i, jq)),
            pl.BlockSpec((tm, qk_width), lambda i: (i, jq + 1)),
            pl.BlockSpec((tm, v_width), lambda i: (i, jq + 2)),
            pl.BlockSpec((tm, LANES), lambda i: (i % ns, 0)),
            pl.BlockSpec((tm, LANES), lambda i: (i % ns, 0)),
        ],
        out_specs=[
            pl.BlockSpec((tm, qk_width), lambda i: (i, 0)),
            pl.BlockSpec((None, qk_width, tm), lambda i: (i // ns, 0, i % ns)),
            pl.BlockSpec((tm, v_width), lambda i: (i, 0)),
        ],
        out_shape=[
            jax.ShapeDtypeStruct((t, qk_width), BF16),
            jax.ShapeDtypeStruct((bsz, qk_width, seq), BF16),
            jax.ShapeDtypeStruct((t, v_width), BF16),
        ],
        compiler_params=_cparams("parallel"),
        name="rope",
    )(p2, p2, p2, cos_t, sin_t)


def _attn_kernel(q_ref, kt_ref, v_ref, lq1_ref, lk1_ref, lq2_ref, lk2_ref, g_ref, o_ref, *, head_dim, lam_init):
    lam = (jnp.exp(jnp.sum(lq1_ref[...] * lk1_ref[...], axis=-1, keepdims=True))
           - jnp.exp(jnp.sum(lq2_ref[...] * lk2_ref[...], axis=-1, keepdims=True)) + lam_init)
    q = q_ref[...]
    kt = kt_ref[...]
    v = v_ref[...]
    lane = lax.broadcasted_iota(jnp.int32, (1, q.shape[1]), 1)

    def component(mask):
        s = jnp.dot(jnp.where(mask, q, jnp.zeros_like(q)), kt, preferred_element_type=F32)
        e = jnp.exp(s - jnp.max(s, axis=-1, keepdims=True))
        l = jnp.sum(e, axis=-1, keepdims=True)
        return jnp.dot(e.astype(BF16), v, preferred_element_type=F32) / l

    o = component(lane < head_dim) - lam * component(lane >= head_dim)
    o = o * lax.rsqrt(jnp.mean(o * o, axis=-1, keepdims=True) + SUBLN_EPS) * g_ref[...]
    o_ref[...] = o * (1.0 - lam_init)


def _attention(q, kt, v, lq1, lk1, lq2, lk2, subln_g, head_dim, lam_init, tq=256):
    bsz, seq, width = q.shape
    v_dim = subln_g.shape[-1]
    assert v_dim == 2 * head_dim == LANES
    heads = width // v_dim
    vec = lambda a: pl.BlockSpec((1, a.shape[-1]), lambda b, h, i: (0, 0))
    lams = [a[None] for a in (lq1, lk1, lq2, lk2)]
    return pl.pallas_call(
        functools.partial(_attn_kernel, head_dim=head_dim, lam_init=lam_init),
        grid=(bsz, heads, seq // tq),
        in_specs=[
            pl.BlockSpec((None, tq, v_dim), lambda b, h, i: (b, i, h)),
            pl.BlockSpec((None, v_dim, seq), lambda b, h, i: (b, h, 0)),
            pl.BlockSpec((None, seq, v_dim), lambda b, h, i: (b, 0, h)),
            *[vec(a) for a in lams],
            vec(subln_g[None]),
        ],
        out_specs=pl.BlockSpec((None, tq, v_dim), lambda b, h, i: (b, i, h)),
        out_shape=jax.ShapeDtypeStruct((bsz, seq, width), F32),
        compiler_params=_cparams("parallel", "parallel", "parallel"),
        name="diff_attention",
    )(q, kt, v, *lams, subln_g[None])


def _merge_kernel(x_ref, yh_ref, ya_ref, gh_ref, ga_ref, wuh_ref, wua_ref, wo_ref, g_ref, wr_ref, br_ref,
                  xo_ref, n_ref, aff_ref):
    mh = jnp.dot(yh_ref[...].astype(BF16), wuh_ref[...], preferred_element_type=F32)
    ma = jnp.dot(ya_ref[...].astype(BF16), wua_ref[...], preferred_element_type=F32)
    merged = jax.nn.sigmoid(gh_ref[...]) * mh + jax.nn.sigmoid(ga_ref[...]) * ma
    x = x_ref[...] + jnp.dot(merged.astype(BF16), wo_ref[...], preferred_element_type=F32)
    xo_ref[...] = x
    n = x * lax.rsqrt(jnp.mean(x * x, axis=-1, keepdims=True) + NORM_EPS) * g_ref[...]
    n_ref[...] = n.astype(BF16)
    logits = lax.dot_general(wr_ref[...], n, (((1,), (1,)), ((), ())), preferred_element_type=F32,
                             precision=lax.Precision.HIGHEST) + br_ref[...]
    e = jnp.exp(logits - jnp.max(logits, axis=0, keepdims=True))
    aff_ref[...] = e / jnp.sum(e, axis=0, keepdims=True)


def _merge(x2, yh2, ya2, p2, col_gate, wuh, wua, wo, g, wr_t, br, bsz, seq, tm=512):
    t, d = x2.shape
    c = yh2.shape[1]
    e = wr_t.shape[0]
    jg = col_gate // d
    ns = seq // tm
    const = lambda a: pl.BlockSpec(a.shape, lambda i: (0,) * a.ndim)
    return pl.pallas_call(
        _merge_kernel,
        grid=(t // tm,),
        in_specs=[
            pl.BlockSpec((tm, d), lambda i: (i, 0)),
            pl.BlockSpec((tm, c), lambda i: (i, 0)),
            pl.BlockSpec((tm, ya2.shape[1]), lambda i: (i, 0)),
            pl.BlockSpec((tm, d), lambda i: (i, jg)),
            pl.BlockSpec((tm, d), lambda i: (i, jg + 1)),
            const(wuh), const(wua), const(wo), const(g), const(wr_t), const(br),
        ],
        out_specs=[
            pl.BlockSpec((tm, d), lambda i: (i, 0)),
            pl.BlockSpec((tm, d), lambda i: (i, 0)),
            pl.BlockSpec((None, e, tm), lambda i: (i // ns, 0, i % ns)),
        ],
        out_shape=[
            jax.ShapeDtypeStruct((t, d), F32),
            jax.ShapeDtypeStruct((t, d), BF16),
            jax.ShapeDtypeStruct((bsz, e, seq), F32),
        ],
        compiler_params=_cparams("parallel"),
        name="merge_router",
    )(x2, yh2, ya2, p2, p2, wuh, wua, wo, g, wr_t, br)


def _select_kernel(aff_ref, pos_ref, *, cap):
    a = aff_ref[...]
    rows, seq = a.shape
    as_f32 = lambda b: lax.bitcast_convert_type(b, F32)
    count = lambda m: jnp.sum(jnp.where(m, 1.0, 0.0), axis=-1, keepdims=True)
    thr = jnp.zeros((rows, 1), jnp.int32)
    for bit in range(30, -1, -1):
        cand = thr | (1 << bit)
        thr = jnp.where(count(a >= as_f32(cand)) >= cap, cand, thr)
    gt = a >= as_f32(thr + 1)
    eq = (a >= as_f32(thr)) & jnp.logical_not(gt)
    need = cap - count(gt)
    tri = jnp.where(lax.broadcasted_iota(jnp.int32, (LANES, LANES), 0)
                    <= lax.broadcasted_iota(jnp.int32, (LANES, LANES), 1), 1.0, 0.0).astype(BF16)

    def exclusive_cumsum(mask):
        ones = jnp.where(mask, 1.0, 0.0)
        carry = jnp.zeros((rows, 1), F32)
        chunks = []
        for j in range(seq // LANES):
            blk = ones[:, j * LANES:(j + 1) * LANES]
            incl = jnp.dot(blk.astype(BF16), tri, preferred_element_type=F32)
            chunks.append(incl - blk + carry)
            carry = carry + jnp.sum(blk, axis=-1, keepdims=True)
        return jnp.concatenate(chunks, axis=1)

    sel = gt | (eq & (exclusive_cumsum(eq) < need))
    pos_ref[...] = jnp.where(sel, exclusive_cumsum(sel), -1.0).astype(jnp.int32)


def _select(aff_rows, cap):
    return pl.pallas_call(
        functools.partial(_select_kernel, cap=cap),
        out_shape=jax.ShapeDtypeStruct(aff_rows.shape, jnp.int32),
        compiler_params=pltpu.CompilerParams(vmem_limit_bytes=VMEM_LIMIT_V7X),
        name="expert_select",
    )(aff_rows)


def _gather_kernel(pos_ref, n_ref, o_ref, *, ts):
    cap = o_ref.shape[0]
    seq = n_ref.shape[0]
    slot = lax.broadcasted_iota(jnp.int32, (cap, ts), 0)
    acc = jnp.zeros(o_ref.shape, F32)
    for j in range(seq // ts):
        onehot = (slot == pos_ref[:, j * ts:(j + 1) * ts]).astype(BF16)
        acc = acc + jnp.dot(onehot, n_ref[pl.ds(j * ts, ts), :], preferred_element_type=F32)
    o_ref[...] = acc.astype(BF16)


def _gather(pos4, n3, cap, ts=512):
    bsz, e, _, seq = pos4.shape
    d = n3.shape[-1]
    return pl.pallas_call(
        functools.partial(_gather_kernel, ts=ts),
        grid=(bsz, e),
        in_specs=[
            pl.BlockSpec((None, None, 1, seq), lambda b, x: (b, x, 0, 0)),
            pl.BlockSpec((None, seq, d), lambda b, x: (b, 0, 0)),
        ],
        out_specs=pl.BlockSpec((None, None, cap, d), lambda b, x: (b, x, 0, 0)),
        out_shape=jax.ShapeDtypeStruct((bsz, e, cap, d), BF16),
        compiler_params=_cparams("parallel", "parallel"),
        name="expert_gather",
    )(pos4, n3)


def _expert_kernel(x_ref, wg_ref, wu_ref, wd_ref, o_ref, wg_sc, wu_sc, wd_sc, acc_sc):
    s = pl.program_id(1)
    b = pl.program_id(2)

    @pl.when(b == 0)
    def _():
        wg_sc[...] = wg_ref[...].astype(BF16)
        wu_sc[...] = wu_ref[...].astype(BF16)
        wd_sc[...] = wd_ref[...].astype(BF16)

    x = x_ref[...]
    g = jnp.dot(x, wg_sc[...], preferred_element_type=F32)
    u = jnp.dot(x, wu_sc[...], preferred_element_type=F32)
    h = (g * jax.nn.sigmoid(g) * u).astype(BF16)
    y = jnp.dot(h, wd_sc[...], preferred_element_type=F32)

    @pl.when(s == 0)
    def _():
        acc_sc[b] = y
        o_ref[...] = y.astype(BF16)

    @pl.when(s > 0)
    def _():
        total = acc_sc[b] + y
        acc_sc[b] = total
        o_ref[...] = total.astype(BF16)


def _experts(xg, wg4, wu4, wd4, layer, f_slices=2):
    bsz, e, cap, d = xg.shape
    f = wg4.shape[-1]
    fs = f // f_slices
    last = f_slices - 1
    out_idx = lambda x, s, b: (jnp.where(s == last, b, 0), x, 0, 0)
    return pl.pallas_call(
        _expert_kernel,
        grid=(e, f_slices, bsz),
        in_specs=[
            pl.BlockSpec((None, None, cap, d), lambda x, s, b: (b, x, 0, 0)),
            pl.BlockSpec((None, None, d, fs), lambda x, s, b: (layer, x, 0, s)),
            pl.BlockSpec((None, None, d, fs), lambda x, s, b: (layer, x, 0, s)),
            pl.BlockSpec((None, None, fs, d), lambda x, s, b: (layer, x, s, 0)),
        ],
        out_specs=pl.BlockSpec((None, None, cap, d), out_idx),
        out_shape=jax.ShapeDtypeStruct((bsz, e, cap, d), BF16),
        scratch_shapes=[pltpu.VMEM((d, fs), BF16), pltpu.VMEM((d, fs), BF16), pltpu.VMEM((fs, d), BF16),
                        pltpu.VMEM((bsz, cap, d), F32)],
        compiler_params=_cparams("arbitrary", "arbitrary", "arbitrary"),
        name="expert_ffn",
    )(xg, wg4, wu4, wd4)


def _combine_kernel(x_ref, pos_ref, aff_ref, ye_ref, g_ref, o_ref, *, final):
    ts = x_ref.shape[0]
    e, cap, _ = ye_ref.shape
    slot = lax.broadcasted_iota(jnp.int32, (ts, cap), 1)
    pos = pos_ref[...]
    aff = aff_ref[...]
    acc = x_ref[...]
    for x in range(e):
        onehot = (slot == pos[:, x:x + 1]).astype(BF16)
        acc = acc + aff[:, x:x + 1] * jnp.dot(onehot, ye_ref[x], preferred_element_type=F32)
    if final:
        acc = acc * lax.rsqrt(jnp.mean(acc * acc, axis=-1, keepdims=True) + NORM_EPS) * g_ref[...]
    o_ref[...] = acc


def _combine(x3, pos_t, aff_t, ye, g, final, ts=512):
    bsz, seq, d = x3.shape
    e, cap = ye.shape[1], ye.shape[2]
    return pl.pallas_call(
        functools.partial(_combine_kernel, final=final),
        grid=(bsz, seq // ts),
        in_specs=[
            pl.BlockSpec((None, ts, d), lambda b, i: (b, i, 0)),
            pl.BlockSpec((None, ts, e), lambda b, i: (b, i, 0)),
            pl.BlockSpec((None, ts, e), lambda b, i: (b, i, 0)),
            _single((None, e, cap, d), lambda b, i: (b, 0, 0, 0)),
            pl.BlockSpec((1, d), lambda b, i: (0, 0)),
        ],
        out_specs=pl.BlockSpec((None, ts, d), lambda b, i: (b, i, 0)),
        out_shape=jax.ShapeDtypeStruct((bsz, seq, d), F32),
        compiler_params=_cparams("parallel", "parallel"),
        name="expert_combine",
    )(x3, pos_t, aff_t, ye, g)


def kernel(x, norm_mix, w_in, b_in, hy_conv_w, hy_conv_b, hy_ffn_w1, hy_ffn_b1, hy_ffn_f1, hy_ffn_w2, hy_ffn_b2, hy_ffn_f2, hy_ffn_w3, hy_bias, lambda_q1, lambda_k1, lambda_q2, lambda_k2, subln_g, w_up_hyena, w_up_attn, w_out, norm_ffn, w_router, b_router, w_e_gate, w_e_up, w_e_down, norm_final):
    bsz, seq, d = x.shape
    depth = w_in.shape[0]
    orders, c = hy_bias.shape[1], hy_bias.shape[2]
    head_dim = lambda_q1.shape[1]
    v_width = w_up_attn.shape[1]
    qk_width = v_width
    e = w_router.shape[2]
    cap = EC_FACTOR * seq // e
    col_q = (orders + 1) * c
    col_gate = col_q + 2 * qk_width + v_width
    assert orders == 2 and bsz % 2 == 0 and col_gate % d == 0

    consts = tuple(jnp.asarray(a, F32).astype(BF16) for a in _dft_constants(seq))

    xs = x.reshape(bsz * seq, d)
    out = None
    for l in range(depth):
        p2 = _inproj(xs, norm_mix[l][None], w_in[l].astype(BF16), b_in[l][None])
        p3 = p2.reshape(bsz, seq, -1)

        uc = _shortconv(p3, hy_conv_w[l], hy_conv_b[l][None], c)
        hfull = _filters(hy_ffn_w1[l], hy_ffn_b1[l], hy_ffn_f1[l], hy_ffn_w2[l], hy_ffn_b2[l], hy_ffn_f2[l],
                        hy_ffn_w3[l], hy_bias[l], seq)
        hspec = _spectrum(hfull, consts[4], consts[1])
        z = _longconv(uc, 0, uc, 1, hspec, 0, consts, natural_out=False)
        y_hy = _longconv(z, 0, uc, 2, hspec, 1, consts, natural_out=True)

        q_r, k_t, v_b = _rope(p2, bsz, seq, qk_width, v_width, head_dim, col_q)
        lam_init = 0.8 - 0.6 * math.exp(-0.3 * l)
        y_da = _attention(q_r.reshape(bsz, seq, qk_width), k_t, v_b.reshape(bsz, seq, v_width),
                          lambda_q1[l], lambda_k1[l], lambda_q2[l], lambda_k2[l], subln_g[l], head_dim, lam_init)

        xs, n2, aff = _merge(xs, y_hy.reshape(bsz * seq, c), y_da.reshape(bsz * seq, v_width), p2, col_gate,
                             w_up_hyena[l].astype(BF16), w_up_attn[l].astype(BF16), w_out[l].astype(BF16),
                             norm_ffn[l][None], w_router[l].T, b_router[l][:, None], bsz, seq)

        pos = _select(aff.reshape(bsz * e, seq), cap).reshape(bsz, e, seq)
        xg = _gather(pos.reshape(bsz, e, 1, seq), n2.reshape(bsz, seq, d), cap)
        ye = _experts(xg, w_e_gate, w_e_up, w_e_down, l)
        final = l == depth - 1
        out = _combine(xs.reshape(bsz, seq, d), pos.transpose(0, 2, 1), aff.transpose(0, 2, 1), ye,
                       norm_final[None], final)
        xs = out.reshape(bsz * seq, d)
    return out
```

```python
import functools
import math

import numpy as np
import jax
import jax.numpy as jnp
from jax import lax
from jax.experimental import pallas as pl
from jax.experimental.pallas import tpu as pltpu

F32 = jnp.float32
BF16 = jnp.bfloat16

NORM_EPS = 1e-6
SUBLN_EPS = 1e-5
ROPE_THETA = 10000.0
HY_FAST_DECAY = 0.3
HY_SLOW_DECAY = 1.5
HY_TARGET = 1e-2
EC_FACTOR = 2

VMEM_LIMIT_V7X = 56 * 1024 * 1024
LANES = 128
BF16_ROWS = 16
COMBINE_TILE = 512
ATTN_SUB_ROWS = 128

DFT_N1 = 64
DFT_N2 = 128
UNROLL_STAGE_A = 4
UNROLL_STAGE_B = 2
STAGE_B_COLS = 2
UNROLL_SHORTCONV = 4


def _cparams(*sem):
    return pltpu.CompilerParams(dimension_semantics=sem, vmem_limit_bytes=VMEM_LIMIT_V7X)


def _single(block_shape, index_map):
    return pl.BlockSpec(block_shape, index_map, pipeline_mode=pl.Buffered(1))


def _inproj_kernel(x_ref, g_ref, w_ref, b_ref, o_ref, n_sc):
    @pl.when(pl.program_id(1) == 0)
    def _():
        x = x_ref[...]
        n = x * lax.rsqrt(jnp.mean(x * x, axis=-1, keepdims=True) + NORM_EPS) * g_ref[...]
        n_sc[...] = n.astype(BF16)

    o_ref[...] = jnp.dot(n_sc[...], w_ref[...], preferred_element_type=F32) + b_ref[...]


def _inproj(x2, g, w_bf, b, tm=1024, tn=1024):
    t, d = x2.shape
    width = w_bf.shape[1]
    return pl.pallas_call(
        _inproj_kernel,
        grid=(t // tm, width // tn),
        in_specs=[
            pl.BlockSpec((tm, d), lambda i, j: (i, 0)),
            pl.BlockSpec((1, d), lambda i, j: (0, 0)),
            pl.BlockSpec((d, tn), lambda i, j: (0, j)),
            pl.BlockSpec((1, tn), lambda i, j: (0, j)),
        ],
        out_specs=pl.BlockSpec((tm, tn), lambda i, j: (i, j)),
        out_shape=jax.ShapeDtypeStruct((t, width), F32),
        scratch_shapes=[pltpu.VMEM((tm, d), BF16)],
        compiler_params=_cparams("parallel", "arbitrary"),
        name="inproj",
    )(x2, g, w_bf, b)


def _shortconv_kernel(u_ref, w_ref, b_ref, o_ref, pad_sc, *, seq, n2, nh):
    zeros = jnp.zeros((8, LANES), F32)
    pad_sc[pl.ds(0, 8), :] = zeros
    pad_sc[pl.ds(8 + seq, 8), :] = zeros
    pad_sc[pl.ds(8, seq), :] = u_ref[...]
    w = w_ref[...]
    bias = b_ref[...]

    def body(r, carry):
        prev = pad_sc[pl.ds(7 + r, nh, stride=n2), :]
        cur = pad_sc[pl.ds(8 + r, nh, stride=n2), :]
        nxt = pad_sc[pl.ds(9 + r, nh, stride=n2), :]
        o_ref[pl.ds(pl.multiple_of(r * nh, nh), nh), :] = prev * w[0:1] + cur * w[1:2] + nxt * w[2:3] + bias
        return carry

    lax.fori_loop(0, n2, body, 0, unroll=UNROLL_SHORTCONV)


def _shortconv(p3, conv_w, conv_b, c):
    bsz, seq, _ = p3.shape
    parts = conv_w.shape[1] // c
    cb_per_part = c // LANES
    nh = seq // DFT_N2
    return pl.pallas_call(
        functools.partial(_shortconv_kernel, seq=seq, n2=DFT_N2, nh=nh),
        grid=(bsz, parts * cb_per_part),
        in_specs=[
            pl.BlockSpec((None, seq, LANES), lambda b, j: (b, 0, j)),
            pl.BlockSpec((3, LANES), lambda b, j: (0, j)),
            pl.BlockSpec((1, LANES), lambda b, j: (0, j)),
        ],
        out_specs=pl.BlockSpec((None, None, seq, LANES), lambda b, j: (j // cb_per_part, b, 0, j % cb_per_part)),
        out_shape=jax.ShapeDtypeStruct((parts, bsz, seq, c), F32),
        scratch_shapes=[pltpu.VMEM((seq + 16, LANES), F32)],
        compiler_params=_cparams("parallel", "parallel"),
        name="shortconv",
    )(p3, conv_w, conv_b)


def _filter_kernel(w1t_ref, w1c_ref, w1s_ref, b1_ref, f1_ref, w2_ref, b2_ref, f2_ref, w3_ref, bias_ref, o_ref,
                   *, seq, tl, c, bands):
    hi = lax.Precision.HIGHEST
    row = lax.broadcasted_iota(jnp.int32, (tl, 1), 0) + pl.program_id(0) * tl
    pos = jnp.where(row < seq, row, 2 * seq - row).astype(F32)
    t = pos / (seq - 1.0)
    w = (2.0 * math.pi) * pos / float(seq)
    band = lax.broadcasted_iota(jnp.int32, (1, bands), 1).astype(F32)
    fr = 1e-4 + band * ((bands - 1 - 1e-4) / (bands - 1))
    ang = w * fr
    pre = (t * w1t_ref[...]
           + jnp.dot(jnp.cos(ang), w1c_ref[...], preferred_element_type=F32, precision=hi)
           - jnp.dot(jnp.sin(ang), w1s_ref[...], preferred_element_type=F32, precision=hi)
           + b1_ref[...])
    h = jnp.sin(f1_ref[...] * pre)
    h = jnp.sin(f2_ref[...] * (jnp.dot(h, w2_ref[...], preferred_element_type=F32, precision=hi) + b2_ref[...]))
    h = jnp.dot(h, w3_ref[...], preferred_element_type=F32, precision=hi)
    min_decay = math.log(HY_TARGET) / HY_FAST_DECAY
    max_decay = math.log(HY_TARGET) / HY_SLOW_DECAY
    ch = lax.broadcasted_iota(jnp.int32, (1, c), 1).astype(F32)
    deltas = jnp.abs(min_decay + ch * ((max_decay - min_decay) / (c - 1)))
    decay = jnp.exp(-t * deltas)
    orders = o_ref.shape[0]
    for o in range(orders):
        taps = h[:, o * c:(o + 1) * c] * decay
        taps = jnp.where(row == 0, taps + bias_ref[o:o + 1, :], taps)
        o_ref[o] = jnp.where(row == seq, 0.0, taps)


def _filters(w1, b1, f1, w2, b2, f2, w3, bias, seq, tl=512):
    emb, ffn = w1.shape
    bands = (emb - 1) // 2
    orders, c = bias.shape
    tiles = seq // tl
    w3_dir = w3.reshape(ffn, orders, 2, c).transpose(2, 0, 1, 3).reshape(2, ffn, orders * c)
    full = lambda a: pl.BlockSpec(a.shape, lambda i: (0,) * a.ndim)
    args = (w1[0:1], w1[1:1 + bands], w1[1 + bands:], b1[None], f1[None], w2, b2[None], f2[None], w3_dir, bias)
    in_specs = [full(a) for a in args]
    in_specs[8] = pl.BlockSpec((None, ffn, orders * c), lambda i: (i // tiles, 0, 0))
    return pl.pallas_call(
        functools.partial(_filter_kernel, seq=seq, tl=tl, c=c, bands=bands),
        grid=(2 * tiles,),
        in_specs=in_specs,
        out_specs=pl.BlockSpec((orders, tl, c), lambda i: (0, i, 0)),
        out_shape=jax.ShapeDtypeStruct((orders, 2 * seq, c), F32),
        compiler_params=_cparams("parallel"),
        name="hyena_filters",
    )(*args)


def _stack_complex(m):
    return np.block([[m.real, -m.imag], [m.imag, m.real]])


@functools.lru_cache(maxsize=None)
def _dft_constants(seq):
    n1, n2 = DFT_N1, DFT_N2
    n = n1 * n2
    assert n == 2 * seq
    nh = seq // n2
    k1 = np.arange(n1)[:, None].astype(np.float64)
    q = np.arange(nh)[None, :].astype(np.float64)
    qf = np.arange(n1)[None, :].astype(np.float64)
    ma = np.empty((n2, 2 * n1, 2 * nh), np.float64)
    mai = np.empty((n2, 2 * nh, 2 * n1), np.float64)
    maf = np.empty((n2, 2 * n1, n1), np.float64)
    for r in range(n2):
        e = np.exp(-2j * np.pi * (q * k1 / n1 + r * k1 / n))
        ma[r] = _stack_complex(e)
        mai[r] = _stack_complex(np.conj(e).T / n)
        ef = np.exp(-2j * np.pi * (qf * k1 / n1 + r * k1 / n))
        maf[r] = np.concatenate([ef.real, ef.imag], axis=0)
    kk = np.arange(n2)[:, None].astype(np.float64)
    rr = np.arange(n2)[None, :].astype(np.float64)
    f = np.exp(-2j * np.pi * kk * rr / n2)
    mb = _stack_complex(f)
    mbi = _stack_complex(np.conj(f).T)
    return tuple(np.asarray(a, np.float32) for a in (ma, mb, mbi, mai, maf))


def _stage_a(load_x, ma_ref, a_sc):
    rows = 2 * DFT_N1

    def body(r, carry):
        a_sc[pl.ds(pl.multiple_of(r * rows, rows), rows), :] = jnp.dot(
            ma_ref[r], load_x(r), preferred_element_type=F32)
        return carry

    lax.fori_loop(0, DFT_N2, body, 0, unroll=UNROLL_STAGE_A)


def _load_a_columns(a_sc, k1):
    cols = []
    for j in range(STAGE_B_COLS):
        ar = a_sc[pl.ds(k1 + j, DFT_N2, stride=2 * DFT_N1), :]
        ai = a_sc[pl.ds(DFT_N1 + k1 + j, DFT_N2, stride=2 * DFT_N1), :]
        cols.append(jnp.concatenate([ar, ai], axis=0))
    return jnp.concatenate(cols, axis=1).astype(BF16)


def _spectrum_kernel(h_ref, maf_ref, mb_ref, o_ref, a_sc):
    rows = 2 * DFT_N2

    def load_x(r):
        return h_ref[pl.ds(r, DFT_N1, stride=DFT_N2), :].astype(BF16)

    _stage_a(load_x, maf_ref, a_sc)

    cb = o_ref.shape[-1]

    def body(kp, carry):
        k1 = kp * STAGE_B_COLS
        z = jnp.dot(mb_ref[...], _load_a_columns(a_sc, k1), preferred_element_type=F32)
        for j in range(STAGE_B_COLS):
            o_ref[pl.ds(pl.multiple_of((k1 + j) * rows, rows), rows), :] = z[:, j * cb:(j + 1) * cb]
        return carry

    lax.fori_loop(0, DFT_N1 // STAGE_B_COLS, body, 0, unroll=UNROLL_STAGE_B)


def _spectrum(hfull, maf, mb):
    orders, n, c = hfull.shape
    rows = 2 * DFT_N1 * DFT_N2
    return pl.pallas_call(
        _spectrum_kernel,
        grid=(orders, c // LANES),
        in_specs=[
            pl.BlockSpec((None, n, LANES), lambda o, j: (o, 0, j)),
            _single(maf.shape, lambda o, j: (0, 0, 0)),
            _single(mb.shape, lambda o, j: (0, 0)),
        ],
        out_specs=pl.BlockSpec((None, rows, LANES), lambda o, j: (o, 0, j)),
        out_shape=jax.ShapeDtypeStruct((orders, rows, c), F32),
        scratch_shapes=[pltpu.VMEM((rows, LANES), F32)],
        compiler_params=_cparams("parallel", "parallel"),
        name="hyena_spectrum",
    )(hfull, maf, mb)


def _longconv_kernel(z_ref, gate_ref, h_ref, ma_ref, mb_ref, mbi_ref, mai_ref, o_ref, a_sc, v_sc,
                     *, nh, natural_out):
    def load_x(r):
        src = pl.ds(pl.multiple_of(r * nh, nh), nh)
        return jnp.concatenate([z_ref[0, src, :], z_ref[1, src, :]], axis=0).astype(BF16)

    _stage_a(load_x, ma_ref, a_sc)

    rows = 2 * DFT_N2

    cb = o_ref.shape[-1]

    def freq(kp, carry):
        k1 = kp * STAGE_B_COLS
        zf = jnp.dot(mb_ref[...], _load_a_columns(a_sc, k1), preferred_element_type=F32)
        ys = []
        for j in range(STAGE_B_COLS):
            base = pl.multiple_of((k1 + j) * rows, rows)
            hr = h_ref[pl.ds(base, DFT_N2), :]
            hi = h_ref[pl.ds(base + DFT_N2, DFT_N2), :]
            zr = zf[:DFT_N2, j * cb:(j + 1) * cb]
            zi = zf[DFT_N2:, j * cb:(j + 1) * cb]
            ys.append(jnp.concatenate([zr * hr - zi * hi, zr * hi + zi * hr], axis=0))
        v = jnp.dot(mbi_ref[...], jnp.concatenate(ys, axis=1).astype(BF16), preferred_element_type=F32)
        for j in range(STAGE_B_COLS):
            v_sc[pl.ds(pl.multiple_of((k1 + j) * rows, rows), rows), :] = v[:, j * cb:(j + 1) * cb]
        return carry

    lax.fori_loop(0, DFT_N1 // STAGE_B_COLS, freq, 0, unroll=UNROLL_STAGE_B)

    def back(r, carry):
        vr = v_sc[pl.ds(r, DFT_N1, stride=rows), :]
        vi = v_sc[pl.ds(DFT_N2 + r, DFT_N1, stride=rows), :]
        y = jnp.dot(mai_ref[r], jnp.concatenate([vr, vi], axis=0).astype(BF16), preferred_element_type=F32)
        src = pl.ds(pl.multiple_of(r * nh, nh), nh)
        dst = pl.ds(r, nh, stride=DFT_N2) if natural_out else src
        o_ref[0, dst, :] = y[:nh] * gate_ref[0, src, :]
        o_ref[1, dst, :] = y[nh:] * gate_ref[1, src, :]
        return carry

    lax.fori_loop(0, DFT_N2, back, 0, unroll=UNROLL_STAGE_A)


def _longconv(z4, z_part, gate4, gate_part, h3, order, consts, natural_out):
    _, bsz, seq, c = z4.shape
    ma, mb, mbi, mai = consts[:4]
    nh = seq // DFT_N2
    rows = 2 * DFT_N1 * DFT_N2
    pair = lambda part: (lambda j, p: (part, p, 0, j))
    return pl.pallas_call(
        functools.partial(_longconv_kernel, nh=nh, natural_out=natural_out),
        grid=(c // LANES, bsz // 2),
        in_specs=[
            _single((None, 2, seq, LANES), pair(z_part)),
            _single((None, 2, seq, LANES), pair(gate_part)),
            _single((None, rows, LANES), lambda j, p: (order, 0, j)),
            _single(ma.shape, lambda j, p: (0, 0, 0)),
            _single(mb.shape, lambda j, p: (0, 0)),
            _single(mbi.shape, lambda j, p: (0, 0)),
            _single(mai.shape, lambda j, p: (0, 0, 0)),
        ],
        out_specs=pl.BlockSpec((None, 2, seq, LANES), pair(0)),
        out_shape=jax.ShapeDtypeStruct((1, bsz, seq, c), F32),
        scratch_shapes=[pltpu.VMEM((rows, LANES), F32), pltpu.VMEM((rows, LANES), F32)],
        compiler_params=_cparams("parallel", "parallel"),
        name="hyena_longconv",
    )(z4, gate4, h3, ma, mb, mbi, mai)


def _rope_kernel(q_ref, k_ref, v_ref, cos_ref, sin_ref, qo_ref, kt_ref, vo_ref, *, half, scale):
    cos = cos_ref[...]
    sin = sin_ref[...]
    lane = lax.broadcasted_iota(jnp.int32, (1, LANES), 1)
    first_half = (lane % (2 * half)) < half

    def rot(x):
        outs = []
        for j in range(x.shape[1] // LANES):
            xb = x[:, j * LANES:(j + 1) * LANES]
            partner = jnp.where(first_half, pltpu.roll(xb, LANES - half, axis=1), pltpu.roll(xb, half, axis=1))
            outs.append(xb * cos + partner * sin)
        return jnp.concatenate(outs, axis=1)

    qo_ref[...] = (rot(q_ref[...]) * scale).astype(BF16)
    kt_ref[...] = rot(k_ref[...]).T.astype(BF16)
    vo_ref[...] = v_ref[...].astype(BF16)


def _rope(p2, bsz, seq, qk_width, v_width, head_dim, col_q, tm=512):
    t = p2.shape[0]
    assert qk_width == v_width and col_q % qk_width == 0
    jq = col_q // qk_width
    half = head_dim // 2
    inv = ROPE_THETA ** (-jnp.arange(half, dtype=F32) * 2.0 / head_dim)
    ang = jnp.arange(seq, dtype=F32)[:, None] * inv[None, :]
    cos, sin = jnp.cos(ang), jnp.sin(ang)
    reps = LANES // head_dim
    cos_t = jnp.tile(jnp.concatenate([cos, cos], axis=1), (1, reps))
    sin_t = jnp.tile(jnp.concatenate([-sin, sin], axis=1), (1, reps))
    ns = seq // tm
    return pl.pallas_call(
        functools.partial(_rope_kernel, half=half, scale=head_dim ** -0.5 * math.log2(math.e)),
        grid=(t // tm,),
        in_specs=[
            pl.BlockSpec((tm, qk_width), lambda i: (i, jq)),
            pl.BlockSpec((tm, qk_width), lambda i: (i, jq + 1)),
            pl.BlockSpec((tm, v_width), lambda i: (i, jq + 2)),
            pl.BlockSpec((tm, LANES), lambda i: (i % ns, 0)),
            pl.BlockSpec((tm, LANES), lambda i: (i % ns, 0)),
        ],
        out_specs=[
            pl.BlockSpec((tm, qk_width), lambda i: (i, 0)),
            pl.BlockSpec((None, qk_width, tm), lambda i: (i // ns, 0, i % ns)),
            pl.BlockSpec((tm, v_width), lambda i: (i, 0)),
        ],
        out_shape=[
            jax.ShapeDtypeStruct((t, qk_width), BF16),
            jax.ShapeDtypeStruct((bsz, qk_width, seq), BF16),
            jax.ShapeDtypeStruct((t, v_width), BF16),
        ],
        compiler_params=_cparams("parallel"),
        name="rope",
    )(p2, p2, p2, cos_t, sin_t)


def _attn_kernel(q_ref, kt_ref, v_ref, lq1_ref, lk1_ref, lq2_ref, lk2_ref, g_ref, o_ref, *, head_dim, lam_init):
    lam = (jnp.exp(jnp.sum(lq1_ref[...] * lk1_ref[...], axis=-1, keepdims=True))
           - jnp.exp(jnp.sum(lq2_ref[...] * lk2_ref[...], axis=-1, keepdims=True)) + lam_init)
    lane = lax.broadcasted_iota(jnp.int32, (1, q_ref.shape[1]), 1)
    sub = ATTN_SUB_ROWS
    nsub = q_ref.shape[0] // sub

    def scores(j):
        q = q_ref[pl.ds(j * sub, sub), :]
        zero = jnp.zeros_like(q)
        qq = jnp.concatenate([jnp.where(lane < head_dim, q, zero), jnp.where(lane >= head_dim, q, zero)], axis=0)
        return jnp.dot(qq, kt_ref[...], preferred_element_type=F32)

    def weights(s):
        e = jnp.exp2(s - jnp.max(s, axis=-1, keepdims=True))
        l = jnp.sum(e, axis=-1, keepdims=True)
        l1, l2 = l[:sub], l[sub:]
        return (e[:sub] - (lam * l1 / l2) * e[sub:]).astype(BF16), l1

    def emit(j, a, l1):
        o = jnp.dot(a, v_ref[...], preferred_element_type=F32) / l1
        o = o * lax.rsqrt(jnp.mean(o * o, axis=-1, keepdims=True) + SUBLN_EPS) * g_ref[...]
        o_ref[pl.ds(j * sub, sub), :] = o * (1.0 - lam_init)

    s_of, a_of = {}, {}
    for t in range(nsub + 2):
        if t < nsub:
            s_of[t] = scores(t)
        if 0 <= t - 1 < nsub:
            a_of[t - 1] = weights(s_of.pop(t - 1))
        if 0 <= t - 2 < nsub:
            emit(t - 2, *a_of.pop(t - 2))


def _attention(q, kt, v, lq1, lk1, lq2, lk2, subln_g, head_dim, lam_init, tq=512):
    bsz, seq, width = q.shape
    v_dim = subln_g.shape[-1]
    assert v_dim == 2 * head_dim == LANES
    heads = width // v_dim
    vec = lambda a: pl.BlockSpec((1, a.shape[-1]), lambda b, h, i: (0, 0))
    lams = [a[None] for a in (lq1, lk1, lq2, lk2)]
    return pl.pallas_call(
        functools.partial(_attn_kernel, head_dim=head_dim, lam_init=lam_init),
        grid=(bsz, heads, seq // tq),
        in_specs=[
            pl.BlockSpec((None, tq, v_dim), lambda b, h, i: (b, i, h)),
            pl.BlockSpec((None, v_dim, seq), lambda b, h, i: (b, h, 0)),
            pl.BlockSpec((None, seq, v_dim), lambda b, h, i: (b, 0, h)),
            *[vec(a) for a in lams],
            vec(subln_g[None]),
        ],
        out_specs=pl.BlockSpec((None, tq, v_dim), lambda b, h, i: (b, i, h)),
        out_shape=jax.ShapeDtypeStruct((bsz, seq, width), F32),
        compiler_params=_cparams("parallel", "parallel", "parallel"),
        name="diff_attention",
    )(q, kt, v, *lams, subln_g[None])


def _merge_kernel(x_ref, yh_ref, ya_ref, gh_ref, ga_ref, wuh_ref, wua_ref, wo_ref, g_ref, wr_ref, br_ref,
                  xo_ref, n_ref, aff_ref):
    mh = jnp.dot(yh_ref[...].astype(BF16), wuh_ref[...], preferred_element_type=F32)
    ma = jnp.dot(ya_ref[...].astype(BF16), wua_ref[...], preferred_element_type=F32)
    merged = jax.nn.sigmoid(gh_ref[...]) * mh + jax.nn.sigmoid(ga_ref[...]) * ma
    x = x_ref[...] + jnp.dot(merged.astype(BF16), wo_ref[...], preferred_element_type=F32)
    xo_ref[...] = x
    n = x * lax.rsqrt(jnp.mean(x * x, axis=-1, keepdims=True) + NORM_EPS) * g_ref[...]
    n_ref[...] = n.astype(BF16)
    logits = lax.dot_general(wr_ref[...], n, (((1,), (1,)), ((), ())), preferred_element_type=F32,
                             precision=lax.Precision.HIGHEST) + br_ref[...]
    e = jnp.exp(logits - jnp.max(logits, axis=0, keepdims=True))
    aff_ref[...] = e / jnp.sum(e, axis=0, keepdims=True)


def _merge(x2, yh2, ya2, p2, col_gate, wuh, wua, wo, g, wr_t, br, bsz, seq, tm=512):
    t, d = x2.shape
    c = yh2.shape[1]
    e = wr_t.shape[0]
    jg = col_gate // d
    ns = seq // tm
    const = lambda a: pl.BlockSpec(a.shape, lambda i: (0,) * a.ndim)
    return pl.pallas_call(
        _merge_kernel,
        grid=(t // tm,),
        in_specs=[
            pl.BlockSpec((tm, d), lambda i: (i, 0)),
            pl.BlockSpec((tm, c), lambda i: (i, 0)),
            pl.BlockSpec((tm, ya2.shape[1]), lambda i: (i, 0)),
            pl.BlockSpec((tm, d), lambda i: (i, jg)),
            pl.BlockSpec((tm, d), lambda i: (i, jg + 1)),
            const(wuh), const(wua), const(wo), const(g), const(wr_t), const(br),
        ],
        out_specs=[
            pl.BlockSpec((tm, d), lambda i: (i, 0)),
            pl.BlockSpec((tm, d), lambda i: (i, 0)),
            pl.BlockSpec((None, e, tm), lambda i: (i // ns, 0, i % ns)),
        ],
        out_shape=[
            jax.ShapeDtypeStruct((t, d), F32),
            jax.ShapeDtypeStruct((t, d), BF16),
            jax.ShapeDtypeStruct((bsz, e, seq), F32),
        ],
        compiler_params=_cparams("parallel"),
        name="merge_router",
    )(x2, yh2, ya2, p2, p2, wuh, wua, wo, g, wr_t, br)


def _select_kernel(aff_ref, pos_ref, *, cap):
    a = aff_ref[...]
    rows, seq = a.shape
    as_f32 = lambda b: lax.bitcast_convert_type(b, F32)
    count = lambda m: jnp.sum(jnp.where(m, 1.0, 0.0), axis=-1, keepdims=True)
    thr = jnp.zeros((rows, 1), jnp.int32)
    for bit in range(30, -1, -1):
        cand = thr | (1 << bit)
        thr = jnp.where(count(a >= as_f32(cand)) >= cap, cand, thr)
    gt = a >= as_f32(thr + 1)
    eq = (a >= as_f32(thr)) & jnp.logical_not(gt)
    need = cap - count(gt)
    tri = jnp.where(lax.broadcasted_iota(jnp.int32, (LANES, LANES), 0)
                    <= lax.broadcasted_iota(jnp.int32, (LANES, LANES), 1), 1.0, 0.0).astype(BF16)

    def exclusive_cumsum(mask):
        ones = jnp.where(mask, 1.0, 0.0)
        carry = jnp.zeros((rows, 1), F32)
        chunks = []
        for j in range(seq // LANES):
            blk = ones[:, j * LANES:(j + 1) * LANES]
            incl = jnp.dot(blk.astype(BF16), tri, preferred_element_type=F32)
            chunks.append(incl - blk + carry)
            carry = carry + jnp.sum(blk, axis=-1, keepdims=True)
        return jnp.concatenate(chunks, axis=1)

    sel = gt | (eq & (exclusive_cumsum(eq) < need))
    pos_ref[...] = jnp.where(sel, exclusive_cumsum(sel), -1.0).astype(jnp.int32)


def _select(aff_rows, cap):
    return pl.pallas_call(
        functools.partial(_select_kernel, cap=cap),
        out_shape=jax.ShapeDtypeStruct(aff_rows.shape, jnp.int32),
        compiler_params=pltpu.CompilerParams(vmem_limit_bytes=VMEM_LIMIT_V7X),
        name="expert_select",
    )(aff_rows)


def _gather_kernel(pos_ref, aff_ref, n_ref, o_ref, gate_ref, *, ts):
    cap = o_ref.shape[0]
    seq = n_ref.shape[0]
    slot = lax.broadcasted_iota(jnp.int32, (cap, ts), 0)
    acc = jnp.zeros(o_ref.shape, F32)
    gate = jnp.zeros((cap, 1), F32)
    for j in range(seq // ts):
        match = slot == pos_ref[:, j * ts:(j + 1) * ts]
        acc = acc + jnp.dot(match.astype(BF16), n_ref[pl.ds(j * ts, ts), :], preferred_element_type=F32)
        gate = gate + jnp.sum(jnp.where(match, aff_ref[:, j * ts:(j + 1) * ts], 0.0), axis=-1, keepdims=True)
    o_ref[...] = acc.astype(BF16)
    gate_ref[...] = gate


def _gather(pos4, aff4, n3, cap, ts=512):
    bsz, e, _, seq = pos4.shape
    d = n3.shape[-1]
    row = pl.BlockSpec((None, None, 1, seq), lambda b, x: (b, x, 0, 0))
    return pl.pallas_call(
        functools.partial(_gather_kernel, ts=ts),
        grid=(bsz, e),
        in_specs=[row, row, pl.BlockSpec((None, seq, d), lambda b, x: (b, 0, 0))],
        out_specs=[
            pl.BlockSpec((None, None, cap, d), lambda b, x: (b, x, 0, 0)),
            pl.BlockSpec((None, None, cap, 1), lambda b, x: (b, x, 0, 0)),
        ],
        out_shape=[
            jax.ShapeDtypeStruct((bsz, e, cap, d), BF16),
            jax.ShapeDtypeStruct((bsz, e, cap, 1), F32),
        ],
        compiler_params=_cparams("parallel", "parallel"),
        name="expert_gather",
    )(pos4, aff4, n3)


def _expert_kernel(x_ref, gate_ref, wg_ref, wu_ref, wd_ref, o_ref, acc_sc, *, last):
    s = pl.program_id(1)
    b = pl.program_id(2)
    x = x_ref[...]
    g = jnp.dot(x, wg_ref[...].astype(BF16), preferred_element_type=F32)
    u = jnp.dot(x, wu_ref[...].astype(BF16), preferred_element_type=F32)
    h = (g * jax.nn.sigmoid(g) * u).astype(BF16)
    y = jnp.dot(h, wd_ref[...].astype(BF16), preferred_element_type=F32)

    @pl.when(s == 0)
    def _():
        acc_sc[b] = y
        o_ref[...] = y.astype(BF16)

    @pl.when(jnp.logical_and(s > 0, s < last))
    def _():
        total = acc_sc[b] + y
        acc_sc[b] = total
        o_ref[...] = total.astype(BF16)

    @pl.when(s == last)
    def _():
        o_ref[...] = ((acc_sc[b] + y) * gate_ref[...]).astype(BF16)


def _experts(xg, gate, wg4, wu4, wd4, layer, f_slices=2):
    bsz, e, cap, d = xg.shape
    f = wg4.shape[-1]
    assert f_slices >= 2
    fs = f // f_slices
    last = f_slices - 1
    out_idx = lambda x, s, b: (jnp.where(s == last, b, 0), x, 0, 0)
    return pl.pallas_call(
        functools.partial(_expert_kernel, last=last),
        grid=(e, f_slices, bsz),
        in_specs=[
            pl.BlockSpec((None, None, cap, d), lambda x, s, b: (b, x, 0, 0)),
            pl.BlockSpec((None, None, cap, 1), lambda x, s, b: (b, x, 0, 0)),
            pl.BlockSpec((None, None, d, fs), lambda x, s, b: (layer, x, 0, s)),
            pl.BlockSpec((None, None, d, fs), lambda x, s, b: (layer, x, 0, s)),
            pl.BlockSpec((None, None, fs, d), lambda x, s, b: (layer, x, s, 0)),
        ],
        out_specs=pl.BlockSpec((None, None, cap, d), out_idx),
        out_shape=jax.ShapeDtypeStruct((bsz, e, cap, d), BF16),
        scratch_shapes=[pltpu.VMEM((bsz, cap, d), F32)],
        compiler_params=_cparams("arbitrary", "arbitrary", "arbitrary"),
        name="expert_ffn",
    )(xg, gate, wg4, wu4, wd4)


def _combine_kernel(starts_ref, x_ref, pos_ref, ye_ref, g_ref, o_ref, stage_sc, *, final, win):
    b = pl.program_id(0)
    i = pl.program_id(1)
    ts = x_ref.shape[0]
    e, cap, _ = ye_ref.shape
    lane = lax.broadcasted_iota(jnp.int32, (ts, win), 1)
    pos = pos_ref[...]

    def first_row(x):
        lo = starts_ref[b, x, i]
        return pl.multiple_of(jnp.minimum((lo // BF16_ROWS) * BF16_ROWS, cap - win), BF16_ROWS)

    onehots = []
    for x in range(e):
        base = first_row(x)
        stage_sc[pl.ds(x * win, win), :] = ye_ref[x, pl.ds(base, win), :]
        onehots.append((pos[:, x:x + 1] - base == lane).astype(BF16))
    o_ref[...] = x_ref[...] + jnp.dot(jnp.concatenate(onehots, axis=1), stage_sc[...],
                                      preferred_element_type=F32)

    for x in range(e):
        base = first_row(x)
        hi = starts_ref[b, x, i + 1]
        col = pos[:, x:x + 1]

        def extra(k, carry, base=base, col=col, x=x):
            want = base + k * win
            row = pl.multiple_of(jnp.minimum(want, cap - win), BF16_ROWS)
            onehot = jnp.logical_and(col - row == lane, col >= want).astype(BF16)
            o_ref[...] += jnp.dot(onehot, ye_ref[x, pl.ds(row, win), :], preferred_element_type=F32)
            return carry

        windows = (jnp.maximum(hi - base, 1) + win - 1) // win
        lax.fori_loop(1, windows, extra, 0)

    if final:
        acc = o_ref[...]
        o_ref[...] = acc * lax.rsqrt(jnp.mean(acc * acc, axis=-1, keepdims=True) + NORM_EPS) * g_ref[...]


def _combine(x3, pos_t, starts, ye, g, final, ts=512, win=128):
    bsz, seq, d = x3.shape
    e, cap = ye.shape[1], ye.shape[2]
    assert cap % BF16_ROWS == 0 and win % BF16_ROWS == 0 and win <= cap
    grid_spec = pltpu.PrefetchScalarGridSpec(
        num_scalar_prefetch=1,
        grid=(bsz, seq // ts),
        in_specs=[
            pl.BlockSpec((None, ts, d), lambda b, i, st: (b, i, 0)),
            pl.BlockSpec((None, ts, e), lambda b, i, st: (b, i, 0)),
            _single((None, e, cap, d), lambda b, i, st: (b, 0, 0, 0)),
            pl.BlockSpec((1, d), lambda b, i, st: (0, 0)),
        ],
        out_specs=pl.BlockSpec((None, ts, d), lambda b, i, st: (b, i, 0)),
        scratch_shapes=[pltpu.VMEM((e * win, d), BF16)],
    )
    return pl.pallas_call(
        functools.partial(_combine_kernel, final=final, win=win),
        grid_spec=grid_spec,
        out_shape=jax.ShapeDtypeStruct((bsz, seq, d), F32),
        compiler_params=_cparams("parallel", "parallel"),
        name="expert_combine",
    )(starts, x3, pos_t, ye, g)


def kernel(x, norm_mix, w_in, b_in, hy_conv_w, hy_conv_b, hy_ffn_w1, hy_ffn_b1, hy_ffn_f1, hy_ffn_w2, hy_ffn_b2, hy_ffn_f2, hy_ffn_w3, hy_bias, lambda_q1, lambda_k1, lambda_q2, lambda_k2, subln_g, w_up_hyena, w_up_attn, w_out, norm_ffn, w_router, b_router, w_e_gate, w_e_up, w_e_down, norm_final):
    bsz, seq, d = x.shape
    depth = w_in.shape[0]
    orders, c = hy_bias.shape[1], hy_bias.shape[2]
    head_dim = lambda_q1.shape[1]
    v_width = w_up_attn.shape[1]
    qk_width = v_width
    e = w_router.shape[2]
    cap = EC_FACTOR * seq // e
    col_q = (orders + 1) * c
    col_gate = col_q + 2 * qk_width + v_width
    assert orders == 2 and bsz % 2 == 0 and col_gate % d == 0

    consts = tuple(jnp.asarray(a, F32).astype(BF16) for a in _dft_constants(seq))

    xs = x.reshape(bsz * seq, d)
    out = None
    for l in range(depth):
        p2 = _inproj(xs, norm_mix[l][None], w_in[l].astype(BF16), b_in[l][None])
        p3 = p2.reshape(bsz, seq, -1)

        uc = _shortconv(p3, hy_conv_w[l], hy_conv_b[l][None], c)
        hfull = _filters(hy_ffn_w1[l], hy_ffn_b1[l], hy_ffn_f1[l], hy_ffn_w2[l], hy_ffn_b2[l], hy_ffn_f2[l],
                        hy_ffn_w3[l], hy_bias[l], seq)
        hspec = _spectrum(hfull, consts[4], consts[1])
        z = _longconv(uc, 0, uc, 1, hspec, 0, consts, natural_out=False)
        y_hy = _longconv(z, 0, uc, 2, hspec, 1, consts, natural_out=True)

        q_r, k_t, v_b = _rope(p2, bsz, seq, qk_width, v_width, head_dim, col_q)
        lam_init = 0.8 - 0.6 * math.exp(-0.3 * l)
        y_da = _attention(q_r.reshape(bsz, seq, qk_width), k_t, v_b.reshape(bsz, seq, v_width),
                          lambda_q1[l], lambda_k1[l], lambda_q2[l], lambda_k2[l], subln_g[l], head_dim, lam_init)

        xs, n2, aff = _merge(xs, y_hy.reshape(bsz * seq, c), y_da.reshape(bsz * seq, v_width), p2, col_gate,
                             w_up_hyena[l].astype(BF16), w_up_attn[l].astype(BF16), w_out[l].astype(BF16),
                             norm_ffn[l][None], w_router[l].T, b_router[l][:, None], bsz, seq)

        pos = _select(aff.reshape(bsz * e, seq), cap).reshape(bsz, e, seq)
        xg, gate = _gather(pos.reshape(bsz, e, 1, seq), aff.reshape(bsz, e, 1, seq), n2.reshape(bsz, seq, d), cap)
        ye = _experts(xg, gate, w_e_gate, w_e_up, w_e_down, l)
        final = l == depth - 1
        tiles = seq // COMBINE_TILE
        counts = jnp.sum((pos >= 0).reshape(bsz, e, tiles, COMBINE_TILE), axis=-1, dtype=jnp.int32)
        starts = jnp.concatenate([jnp.zeros((bsz, e, 1), jnp.int32), jnp.cumsum(counts, axis=-1)], axis=-1)
        out = _combine(xs.reshape(bsz, seq, d), pos.transpose(0, 2, 1), starts, ye, norm_final[None], final,
                       ts=COMBINE_TILE)
        xs = out.reshape(bsz * seq, d)
    return out
```

```python
import functools
import math

import numpy as np
import jax
import jax.numpy as jnp
from jax import lax
from jax.experimental import pallas as pl
from jax.experimental.pallas import tpu as pltpu

F32 = jnp.float32
BF16 = jnp.bfloat16

NORM_EPS = 1e-6
SUBLN_EPS = 1e-5
ROPE_THETA = 10000.0
HY_FAST_DECAY = 0.3
HY_SLOW_DECAY = 1.5
HY_TARGET = 1e-2
EC_FACTOR = 2

VMEM_LIMIT_V7X = 56 * 1024 * 1024
LANES = 128
BF16_ROWS = 16
COMBINE_TILE = 512
ATTN_SUB_ROWS = 128

DFT_N1 = 64
DFT_N2 = 128
UNROLL_STAGE_A = 4
UNROLL_STAGE_B = 2
STAGE_B_COLS = 2
UNROLL_SHORTCONV = 4


def _cparams(*sem):
    return pltpu.CompilerParams(dimension_semantics=sem, vmem_limit_bytes=VMEM_LIMIT_V7X)


def _single(block_shape, index_map):
    return pl.BlockSpec(block_shape, index_map, pipeline_mode=pl.Buffered(1))


def _inproj_kernel(x_ref, g_ref, w_ref, b_ref, o_ref, n_sc):
    @pl.when(pl.program_id(1) == 0)
    def _():
        x = x_ref[...]
        n = x * lax.rsqrt(jnp.mean(x * x, axis=-1, keepdims=True) + NORM_EPS) * g_ref[...]
        n_sc[...] = n.astype(BF16)

    o_ref[...] = jnp.dot(n_sc[...], w_ref[...], preferred_element_type=F32) + b_ref[...]


def _inproj(x2, g, w_bf, b, tm=2048, tn=1024):
    t, d = x2.shape
    width = w_bf.shape[1]
    return pl.pallas_call(
        _inproj_kernel,
        grid=(t // tm, width // tn),
        in_specs=[
            pl.BlockSpec((tm, d), lambda i, j: (i, 0)),
            pl.BlockSpec((1, d), lambda i, j: (0, 0)),
            pl.BlockSpec((d, tn), lambda i, j: (0, j)),
            pl.BlockSpec((1, tn), lambda i, j: (0, j)),
        ],
        out_specs=pl.BlockSpec((tm, tn), lambda i, j: (i, j)),
        out_shape=jax.ShapeDtypeStruct((t, width), F32),
        scratch_shapes=[pltpu.VMEM((tm, d), BF16)],
        compiler_params=_cparams("parallel", "arbitrary"),
        name="inproj",
    )(x2, g, w_bf, b)


def _shortconv_kernel(u_ref, w_ref, b_ref, o_ref, pad_sc, *, seq, n2, nh):
    zeros = jnp.zeros((8, LANES), F32)
    pad_sc[pl.ds(0, 8), :] = zeros
    pad_sc[pl.ds(8 + seq, 8), :] = zeros
    pad_sc[pl.ds(8, seq), :] = u_ref[...]
    w = w_ref[...]
    bias = b_ref[...]

    column = lambda t: pad_sc[pl.ds(7 + t, nh, stride=n2), :]

    def body(r, taps):
        prev, cur = taps
        nxt = column(r + 2)
        o_ref[pl.ds(pl.multiple_of(r * nh, nh), nh), :] = prev * w[0:1] + cur * w[1:2] + nxt * w[2:3] + bias
        return cur, nxt

    lax.fori_loop(0, n2, body, (column(0), column(1)), unroll=UNROLL_SHORTCONV)


def _shortconv(p3, conv_w, conv_b, c):
    bsz, seq, _ = p3.shape
    parts = conv_w.shape[1] // c
    cb_per_part = c // LANES
    nh = seq // DFT_N2
    return pl.pallas_call(
        functools.partial(_shortconv_kernel, seq=seq, n2=DFT_N2, nh=nh),
        grid=(bsz, parts * cb_per_part),
        in_specs=[
            pl.BlockSpec((None, seq, LANES), lambda b, j: (b, 0, j)),
            pl.BlockSpec((3, LANES), lambda b, j: (0, j)),
            pl.BlockSpec((1, LANES), lambda b, j: (0, j)),
        ],
        out_specs=pl.BlockSpec((None, None, seq, LANES), lambda b, j: (j // cb_per_part, b, 0, j % cb_per_part)),
        out_shape=jax.ShapeDtypeStruct((parts, bsz, seq, c), F32),
        scratch_shapes=[pltpu.VMEM((seq + 16, LANES), F32)],
        compiler_params=_cparams("parallel", "parallel"),
        name="shortconv",
    )(p3, conv_w, conv_b)


def _filter_kernel(w1t_ref, w1c_ref, w1s_ref, b1_ref, f1_ref, w2_ref, b2_ref, f2_ref, w3_ref, bias_ref, o_ref,
                   *, seq, tl, c, bands):
    hi = lax.Precision.HIGHEST
    lag = lambda idx: jnp.where(idx < seq, idx, 2 * seq - idx).astype(F32)
    row = lax.broadcasted_iota(jnp.int32, (tl, 1), 0) + pl.program_id(0) * tl
    t = lag(row) / (seq - 1.0)
    pos = lag(lax.broadcasted_iota(jnp.int32, (1, tl), 1) + pl.program_id(0) * tl)
    t_l = pos / (seq - 1.0)
    w_l = (2.0 * math.pi) * pos / float(seq)
    band = lax.broadcasted_iota(jnp.int32, (bands, 1), 0).astype(F32)
    fr = 1e-4 + band * ((bands - 1 - 1e-4) / (bands - 1))
    ang = fr * w_l
    pre = (w1t_ref[...] * t_l
           + jnp.dot(w1c_ref[...], jnp.cos(ang), preferred_element_type=F32, precision=hi)
           - jnp.dot(w1s_ref[...], jnp.sin(ang), preferred_element_type=F32, precision=hi)
           + b1_ref[...])
    h = jnp.sin(f1_ref[...] * pre)
    h = jnp.sin(f2_ref[...] * (jnp.dot(w2_ref[...], h, preferred_element_type=F32, precision=hi) + b2_ref[...]))
    h = jnp.dot(h.T, w3_ref[...], preferred_element_type=F32, precision=hi)
    min_decay = math.log(HY_TARGET) / HY_FAST_DECAY
    max_decay = math.log(HY_TARGET) / HY_SLOW_DECAY
    ch = lax.broadcasted_iota(jnp.int32, (1, c), 1).astype(F32)
    deltas = jnp.abs(min_decay + ch * ((max_decay - min_decay) / (c - 1)))
    decay = jnp.exp(-t * deltas)
    orders = o_ref.shape[0]
    for o in range(orders):
        taps = h[:, o * c:(o + 1) * c] * decay
        taps = jnp.where(row == 0, taps + bias_ref[o:o + 1, :], taps)
        o_ref[o] = jnp.where(row == seq, 0.0, taps)


def _filters(w1, b1, f1, w2, b2, f2, w3, bias, seq, tl=512):
    emb, ffn = w1.shape
    bands = (emb - 1) // 2
    orders, c = bias.shape
    tiles = seq // tl
    w3_dir = w3.reshape(ffn, orders, 2, c).transpose(2, 0, 1, 3).reshape(2, ffn, orders * c)
    full = lambda a: pl.BlockSpec(a.shape, lambda i: (0,) * a.ndim)
    col = lambda a: a[:, None]
    args = (w1[0:1].T, w1[1:1 + bands].T, w1[1 + bands:].T, col(b1), col(f1), w2.T, col(b2), col(f2), w3_dir, bias)
    in_specs = [full(a) for a in args]
    in_specs[8] = pl.BlockSpec((None, ffn, orders * c), lambda i: (i // tiles, 0, 0))
    return pl.pallas_call(
        functools.partial(_filter_kernel, seq=seq, tl=tl, c=c, bands=bands),
        grid=(2 * tiles,),
        in_specs=in_specs,
        out_specs=pl.BlockSpec((orders, tl, c), lambda i: (0, i, 0)),
        out_shape=jax.ShapeDtypeStruct((orders, 2 * seq, c), F32),
        compiler_params=_cparams("parallel"),
        name="hyena_filters",
    )(*args)


def _stack_complex(m):
    return np.block([[m.real, -m.imag], [m.imag, m.real]])


@functools.lru_cache(maxsize=None)
def _dft_constants(seq):
    n1, n2 = DFT_N1, DFT_N2
    n = n1 * n2
    assert n == 2 * seq
    nh = seq // n2
    k1 = np.arange(n1)[:, None].astype(np.float64)
    q = np.arange(nh)[None, :].astype(np.float64)
    qf = np.arange(n1)[None, :].astype(np.float64)
    ma = np.empty((n2, 2 * n1, 2 * nh), np.float64)
    mai = np.empty((n2, 2 * nh, 2 * n1), np.float64)
    maf = np.empty((n2, 2 * n1, n1), np.float64)
    for r in range(n2):
        e = np.exp(-2j * np.pi * (q * k1 / n1 + r * k1 / n))
        ma[r] = _stack_complex(e)
        mai[r] = _stack_complex(np.conj(e).T / n)
        ef = np.exp(-2j * np.pi * (qf * k1 / n1 + r * k1 / n))
        maf[r] = np.concatenate([ef.real, ef.imag], axis=0)
    kk = np.arange(n2)[:, None].astype(np.float64)
    rr = np.arange(n2)[None, :].astype(np.float64)
    f = np.exp(-2j * np.pi * kk * rr / n2)
    mb = _stack_complex(f)
    mbi = _stack_complex(np.conj(f).T)
    return tuple(np.asarray(a, np.float32) for a in (ma, mb, mbi, mai, maf))


def _stage_a(load_x, ma_ref, a_sc):
    rows = 2 * DFT_N1

    def body(r, carry):
        a_sc[pl.ds(pl.multiple_of(r * rows, rows), rows), :] = jnp.dot(
            ma_ref[r], load_x(r), preferred_element_type=F32)
        return carry

    lax.fori_loop(0, DFT_N2, body, 0, unroll=UNROLL_STAGE_A)


def _load_a_columns(a_sc, k1):
    cols = []
    for j in range(STAGE_B_COLS):
        ar = a_sc[pl.ds(k1 + j, DFT_N2, stride=2 * DFT_N1), :]
        ai = a_sc[pl.ds(DFT_N1 + k1 + j, DFT_N2, stride=2 * DFT_N1), :]
        cols.append(jnp.concatenate([ar, ai], axis=0))
    return jnp.concatenate(cols, axis=1).astype(BF16)


def _spectrum_kernel(h_ref, maf_ref, mb_ref, o_ref, a_sc):
    rows = 2 * DFT_N2

    def load_x(r):
        return h_ref[pl.ds(r, DFT_N1, stride=DFT_N2), :].astype(BF16)

    _stage_a(load_x, maf_ref, a_sc)

    cb = o_ref.shape[-1]

    def body(kp, carry):
        k1 = kp * STAGE_B_COLS
        z = jnp.dot(mb_ref[...], _load_a_columns(a_sc, k1), preferred_element_type=F32)
        for j in range(STAGE_B_COLS):
            o_ref[pl.ds(pl.multiple_of((k1 + j) * rows, rows), rows), :] = z[:, j * cb:(j + 1) * cb]
        return carry

    lax.fori_loop(0, DFT_N1 // STAGE_B_COLS, body, 0, unroll=UNROLL_STAGE_B)


def _spectrum(hfull, maf, mb):
    orders, n, c = hfull.shape
    rows = 2 * DFT_N1 * DFT_N2
    return pl.pallas_call(
        _spectrum_kernel,
        grid=(orders, c // LANES),
        in_specs=[
            pl.BlockSpec((None, n, LANES), lambda o, j: (o, 0, j)),
            _single(maf.shape, lambda o, j: (0, 0, 0)),
            _single(mb.shape, lambda o, j: (0, 0)),
        ],
        out_specs=pl.BlockSpec((None, rows, LANES), lambda o, j: (o, 0, j)),
        out_shape=jax.ShapeDtypeStruct((orders, rows, c), F32),
        scratch_shapes=[pltpu.VMEM((rows, LANES), F32)],
        compiler_params=_cparams("parallel", "parallel"),
        name="hyena_spectrum",
    )(hfull, maf, mb)


def _longconv_kernel(z_ref, gate_ref, h_ref, ma_ref, mb_ref, mbi_ref, mai_ref, o_ref, a_sc, v_sc,
                     *, nh, natural_out):
    def load_x(r):
        src = pl.ds(pl.multiple_of(r * nh, nh), nh)
        return jnp.concatenate([z_ref[0, src, :], z_ref[1, src, :]], axis=0).astype(BF16)

    _stage_a(load_x, ma_ref, a_sc)

    rows = 2 * DFT_N2

    cb = o_ref.shape[-1]

    def freq(kp, carry):
        k1 = kp * STAGE_B_COLS
        zf = jnp.dot(mb_ref[...], _load_a_columns(a_sc, k1), preferred_element_type=F32)
        ys = []
        for j in range(STAGE_B_COLS):
            base = pl.multiple_of((k1 + j) * rows, rows)
            hr = h_ref[pl.ds(base, DFT_N2), :]
            hi = h_ref[pl.ds(base + DFT_N2, DFT_N2), :]
            zr = zf[:DFT_N2, j * cb:(j + 1) * cb]
            zi = zf[DFT_N2:, j * cb:(j + 1) * cb]
            ys.append(jnp.concatenate([zr * hr - zi * hi, zr * hi + zi * hr], axis=0))
        v = jnp.dot(mbi_ref[...], jnp.concatenate(ys, axis=1).astype(BF16), preferred_element_type=F32)
        for j in range(STAGE_B_COLS):
            v_sc[pl.ds(pl.multiple_of((k1 + j) * rows, rows), rows), :] = v[:, j * cb:(j + 1) * cb]
        return carry

    lax.fori_loop(0, DFT_N1 // STAGE_B_COLS, freq, 0, unroll=UNROLL_STAGE_B)

    def back(r, carry):
        vr = v_sc[pl.ds(r, DFT_N1, stride=rows), :]
        vi = v_sc[pl.ds(DFT_N2 + r, DFT_N1, stride=rows), :]
        y = jnp.dot(mai_ref[r], jnp.concatenate([vr, vi], axis=0).astype(BF16), preferred_element_type=F32)
        src = pl.ds(pl.multiple_of(r * nh, nh), nh)
        dst = pl.ds(r, nh, stride=DFT_N2) if natural_out else src
        o_ref[0, dst, :] = y[:nh] * gate_ref[0, src, :]
        o_ref[1, dst, :] = y[nh:] * gate_ref[1, src, :]
        return carry

    lax.fori_loop(0, DFT_N2, back, 0, unroll=UNROLL_STAGE_A)


def _longconv(z4, z_part, gate4, gate_part, h3, order, consts, natural_out):
    _, bsz, seq, c = z4.shape
    ma, mb, mbi, mai = consts[:4]
    nh = seq // DFT_N2
    rows = 2 * DFT_N1 * DFT_N2
    pair = lambda part: (lambda j, p: (part, p, 0, j))
    return pl.pallas_call(
        functools.partial(_longconv_kernel, nh=nh, natural_out=natural_out),
        grid=(c // LANES, bsz // 2),
        in_specs=[
            _single((None, 2, seq, LANES), pair(z_part)),
            _single((None, 2, seq, LANES), pair(gate_part)),
            _single((None, rows, LANES), lambda j, p: (order, 0, j)),
            _single(ma.shape, lambda j, p: (0, 0, 0)),
            _single(mb.shape, lambda j, p: (0, 0)),
            _single(mbi.shape, lambda j, p: (0, 0)),
            _single(mai.shape, lambda j, p: (0, 0, 0)),
        ],
        out_specs=pl.BlockSpec((None, 2, seq, LANES), pair(0)),
        out_shape=jax.ShapeDtypeStruct((1, bsz, seq, c), F32),
        scratch_shapes=[pltpu.VMEM((rows, LANES), F32), pltpu.VMEM((rows, LANES), F32)],
        compiler_params=_cparams("parallel", "parallel"),
        name="hyena_longconv",
    )(z4, gate4, h3, ma, mb, mbi, mai)


def _rope_kernel(q_ref, k_ref, v_ref, cos_ref, sin_ref, qo_ref, kt_ref, vo_ref, *, half, scale):
    cos = cos_ref[...]
    sin = sin_ref[...]
    lane = lax.broadcasted_iota(jnp.int32, (1, LANES), 1)
    first_half = (lane % (2 * half)) < half

    def rot(x):
        outs = []
        for j in range(x.shape[1] // LANES):
            xb = x[:, j * LANES:(j + 1) * LANES]
            partner = jnp.where(first_half, pltpu.roll(xb, LANES - half, axis=1), pltpu.roll(xb, half, axis=1))
            outs.append(xb * cos + partner * sin)
        return jnp.concatenate(outs, axis=1)

    qo_ref[...] = (rot(q_ref[...]) * scale).astype(BF16)
    kt_ref[...] = rot(k_ref[...]).T.astype(BF16)
    vo_ref[...] = v_ref[...].astype(BF16)


def _rope(p2, bsz, seq, qk_width, v_width, head_dim, col_q, tm=512):
    t = p2.shape[0]
    assert qk_width == v_width and col_q % qk_width == 0
    jq = col_q // qk_width
    half = head_dim // 2
    inv = ROPE_THETA ** (-jnp.arange(half, dtype=F32) * 2.0 / head_dim)
    ang = jnp.arange(seq, dtype=F32)[:, None] * inv[None, :]
    cos, sin = jnp.cos(ang), jnp.sin(ang)
    reps = LANES // head_dim
    cos_t = jnp.tile(jnp.concatenate([cos, cos], axis=1), (1, reps))
    sin_t = jnp.tile(jnp.concatenate([-sin, sin], axis=1), (1, reps))
    ns = seq // tm
    return pl.pallas_call(
        functools.partial(_rope_kernel, half=half, scale=head_dim ** -0.5 * math.log2(math.e)),
        grid=(t // tm,),
        in_specs=[
            pl.BlockSpec((tm, qk_width), lambda i: (i, jq)),
            pl.BlockSpec((tm, qk_width), lambda i: (i, jq + 1)),
            pl.BlockSpec((tm, v_width), lambda i: (i, jq + 2)),
            pl.BlockSpec((tm, LANES), lambda i: (i % ns, 0)),
            pl.BlockSpec((tm, LANES), lambda i: (i % ns, 0)),
        ],
        out_specs=[
            pl.BlockSpec((tm, qk_width), lambda i: (i, 0)),
            pl.BlockSpec((None, qk_width, tm), lambda i: (i // ns, 0, i % ns)),
            pl.BlockSpec((tm, v_width), lambda i: (i, 0)),
        ],
        out_shape=[
            jax.ShapeDtypeStruct((t, qk_width), BF16),
            jax.ShapeDtypeStruct((bsz, qk_width, seq), BF16),
            jax.ShapeDtypeStruct((t, v_width), BF16),
        ],
        compiler_params=_cparams("parallel"),
        name="rope",
    )(p2, p2, p2, cos_t, sin_t)


def _attn_kernel(q_ref, kt_ref, v_ref, lq1_ref, lk1_ref, lq2_ref, lk2_ref, g_ref, o_ref, *, head_dim, lam_init):
    lam = (jnp.exp(jnp.sum(lq1_ref[...] * lk1_ref[...], axis=-1, keepdims=True))
           - jnp.exp(jnp.sum(lq2_ref[...] * lk2_ref[...], axis=-1, keepdims=True)) + lam_init)
    lane = lax.broadcasted_iota(jnp.int32, (1, q_ref.shape[1]), 1)
    sub = ATTN_SUB_ROWS
    nsub = q_ref.shape[0] // sub

    def scores(j):
        q = q_ref[pl.ds(j * sub, sub), :]
        zero = jnp.zeros_like(q)
        qq = jnp.concatenate([jnp.where(lane < head_dim, q, zero), jnp.where(lane >= head_dim, q, zero)], axis=0)
        return jnp.dot(qq, kt_ref[...], preferred_element_type=F32)

    def weights(s):
        e = jnp.exp2(s - jnp.max(s, axis=-1, keepdims=True))
        l = jnp.sum(e, axis=-1, keepdims=True)
        l1, l2 = l[:sub], l[sub:]
        return (e[:sub] - (lam * l1 / l2) * e[sub:]).astype(BF16), l1

    def emit(j, a, l1):
        o = jnp.dot(a, v_ref[...], preferred_element_type=F32) / l1
        o = o * lax.rsqrt(jnp.mean(o * o, axis=-1, keepdims=True) + SUBLN_EPS) * g_ref[...]
        o_ref[pl.ds(j * sub, sub), :] = o * (1.0 - lam_init)

    s_of, a_of = {}, {}
    for t in range(nsub + 2):
        if t < nsub:
            s_of[t] = scores(t)
        if 0 <= t - 1 < nsub:
            a_of[t - 1] = weights(s_of.pop(t - 1))
        if 0 <= t - 2 < nsub:
            emit(t - 2, *a_of.pop(t - 2))


def _attention(q, kt, v, lq1, lk1, lq2, lk2, subln_g, head_dim, lam_init, tq=512):
    bsz, seq, width = q.shape
    v_dim = subln_g.shape[-1]
    assert v_dim == 2 * head_dim == LANES
    heads = width // v_dim
    vec = lambda a: pl.BlockSpec((1, a.shape[-1]), lambda b, h, i: (0, 0))
    lams = [a[None] for a in (lq1, lk1, lq2, lk2)]
    return pl.pallas_call(
        functools.partial(_attn_kernel, head_dim=head_dim, lam_init=lam_init),
        grid=(bsz, heads, seq // tq),
        in_specs=[
            pl.BlockSpec((None, tq, v_dim), lambda b, h, i: (b, i, h)),
            pl.BlockSpec((None, v_dim, seq), lambda b, h, i: (b, h, 0)),
            pl.BlockSpec((None, seq, v_dim), lambda b, h, i: (b, 0, h)),
            *[vec(a) for a in lams],
            vec(subln_g[None]),
        ],
        out_specs=pl.BlockSpec((None, tq, v_dim), lambda b, h, i: (b, i, h)),
        out_shape=jax.ShapeDtypeStruct((bsz, seq, width), F32),
        compiler_params=_cparams("parallel", "parallel", "parallel"),
        name="diff_attention",
    )(q, kt, v, *lams, subln_g[None])


def _merge_kernel(x_ref, yh_ref, ya_ref, gh_ref, ga_ref, wuh_ref, wua_ref, wo_ref, g_ref, wr_ref, br_ref,
                  xo_ref, n_ref, aff_ref):
    mh = jnp.dot(yh_ref[...].astype(BF16), wuh_ref[...], preferred_element_type=F32)
    ma = jnp.dot(ya_ref[...].astype(BF16), wua_ref[...], preferred_element_type=F32)
    merged = jax.nn.sigmoid(gh_ref[...]) * mh + jax.nn.sigmoid(ga_ref[...]) * ma
    x = x_ref[...] + jnp.dot(merged.astype(BF16), wo_ref[...], preferred_element_type=F32)
    xo_ref[...] = x
    n = x * lax.rsqrt(jnp.mean(x * x, axis=-1, keepdims=True) + NORM_EPS) * g_ref[...]
    n_ref[...] = n.astype(BF16)
    logits = lax.dot_general(wr_ref[...], n, (((1,), (1,)), ((), ())), preferred_element_type=F32,
                             precision=lax.Precision.HIGHEST) + br_ref[...]
    e = jnp.exp(logits - jnp.max(logits, axis=0, keepdims=True))
    aff_ref[...] = e / jnp.sum(e, axis=0, keepdims=True)


def _merge(x2, yh2, ya2, p2, col_gate, wuh, wua, wo, g, wr_t, br, bsz, seq, tm=512):
    t, d = x2.shape
    c = yh2.shape[1]
    e = wr_t.shape[0]
    jg = col_gate // d
    ns = seq // tm
    const = lambda a: pl.BlockSpec(a.shape, lambda i: (0,) * a.ndim)
    return pl.pallas_call(
        _merge_kernel,
        grid=(t // tm,),
        in_specs=[
            pl.BlockSpec((tm, d), lambda i: (i, 0)),
            pl.BlockSpec((tm, c), lambda i: (i, 0)),
            pl.BlockSpec((tm, ya2.shape[1]), lambda i: (i, 0)),
            pl.BlockSpec((tm, d), lambda i: (i, jg)),
            pl.BlockSpec((tm, d), lambda i: (i, jg + 1)),
            const(wuh), const(wua), const(wo), const(g), const(wr_t), const(br),
        ],
        out_specs=[
            pl.BlockSpec((tm, d), lambda i: (i, 0)),
            pl.BlockSpec((tm, d), lambda i: (i, 0)),
            pl.BlockSpec((None, e, tm), lambda i: (i // ns, 0, i % ns)),
        ],
        out_shape=[
            jax.ShapeDtypeStruct((t, d), F32),
            jax.ShapeDtypeStruct((t, d), BF16),
            jax.ShapeDtypeStruct((bsz, e, seq), F32),
        ],
        compiler_params=_cparams("parallel"),
        name="merge_router",
    )(x2, yh2, ya2, p2, p2, wuh, wua, wo, g, wr_t, br)


def _select_kernel(aff_ref, pos_ref, *, cap):
    a = aff_ref[...]
    rows, seq = a.shape
    as_f32 = lambda b: lax.bitcast_convert_type(b, F32)
    count = lambda m: jnp.sum(jnp.where(m, 1.0, 0.0), axis=-1, keepdims=True)
    thr = jnp.zeros((rows, 1), jnp.int32)
    for bit in range(30, -1, -1):
        cand = thr | (1 << bit)
        thr = jnp.where(count(a >= as_f32(cand)) >= cap, cand, thr)
    gt = a >= as_f32(thr + 1)
    eq = (a >= as_f32(thr)) & jnp.logical_not(gt)
    need = cap - count(gt)
    tri = jnp.where(lax.broadcasted_iota(jnp.int32, (LANES, LANES), 0)
                    <= lax.broadcasted_iota(jnp.int32, (LANES, LANES), 1), 1.0, 0.0).astype(BF16)

    def exclusive_cumsum(mask):
        ones = jnp.where(mask, 1.0, 0.0)
        carry = jnp.zeros((rows, 1), F32)
        chunks = []
        for j in range(seq // LANES):
            blk = ones[:, j * LANES:(j + 1) * LANES]
            incl = jnp.dot(blk.astype(BF16), tri, preferred_element_type=F32)
            chunks.append(incl - blk + carry)
            carry = carry + jnp.sum(blk, axis=-1, keepdims=True)
        return jnp.concatenate(chunks, axis=1)

    sel = gt | (eq & (exclusive_cumsum(eq) < need))
    pos_ref[...] = jnp.where(sel, exclusive_cumsum(sel), -1.0).astype(jnp.int32)


def _select(aff_rows, cap):
    return pl.pallas_call(
        functools.partial(_select_kernel, cap=cap),
        out_shape=jax.ShapeDtypeStruct(aff_rows.shape, jnp.int32),
        compiler_params=pltpu.CompilerParams(vmem_limit_bytes=VMEM_LIMIT_V7X),
        name="expert_select",
    )(aff_rows)


def _gather_kernel(starts_ref, pos_ref, aff_ref, n_ref, o_ref, gate_ref, *, win):
    b = pl.program_id(0)
    i = pl.program_id(1)
    e, cap, _ = o_ref.shape
    ts = n_ref.shape[0]

    @pl.when(i == 0)
    def _():
        o_ref[...] = jnp.zeros_like(o_ref)
        gate_ref[...] = jnp.zeros_like(gate_ref)

    slot0 = lax.broadcasted_iota(jnp.int32, (win, ts), 0)
    pos = pos_ref[...]
    aff = aff_ref[...]

    def first_row(x):
        lo = starts_ref[b, x, i]
        return pl.multiple_of(jnp.minimum((lo // BF16_ROWS) * BF16_ROWS, cap - win), BF16_ROWS)

    def add_gates(x, dst, match):
        picked = jnp.sum(jnp.where(match, aff[x:x + 1, :], 0.0), axis=-1, keepdims=True)
        gate_ref[x, dst, :] = gate_ref[x, dst, :] + picked

    onehots = []
    for x in range(e):
        base = first_row(x)
        match = (slot0 + base) == pos[x:x + 1, :]
        add_gates(x, pl.ds(base, win), match)
        onehots.append(match.astype(BF16))
    picked = jnp.dot(jnp.concatenate(onehots, axis=0), n_ref[...], preferred_element_type=F32)
    for x in range(e):
        dst = pl.ds(first_row(x), win)
        o_ref[x, dst, :] = o_ref[x, dst, :] + picked[x * win:(x + 1) * win].astype(BF16)

    for x in range(e):
        base = first_row(x)
        hi = starts_ref[b, x, i + 1]

        def extra(k, carry, base=base, x=x):
            want = base + k * win
            row = pl.multiple_of(jnp.minimum(want, cap - win), BF16_ROWS)
            slots = slot0 + row
            match = jnp.logical_and(slots == pos[x:x + 1, :], slots >= want)
            dst = pl.ds(row, win)
            add_gates(x, dst, match)
            o_ref[x, dst, :] = o_ref[x, dst, :] + jnp.dot(match.astype(BF16), n_ref[...],
                                                          preferred_element_type=F32).astype(BF16)
            return carry

        windows = (jnp.maximum(hi - base, 1) + win - 1) // win
        lax.fori_loop(1, windows, extra, 0)


def _gather(pos, aff, starts, n3, cap, ts=COMBINE_TILE, win=128):
    bsz, e, seq = pos.shape
    d = n3.shape[-1]
    assert cap % BF16_ROWS == 0 and win % BF16_ROWS == 0 and win <= cap
    tile = pl.BlockSpec((None, e, ts), lambda b, i, st: (b, 0, i))
    grid_spec = pltpu.PrefetchScalarGridSpec(
        num_scalar_prefetch=1,
        grid=(bsz, seq // ts),
        in_specs=[tile, tile, pl.BlockSpec((None, ts, d), lambda b, i, st: (b, i, 0))],
        out_specs=[
            pl.BlockSpec((None, e, cap, d), lambda b, i, st: (b, 0, 0, 0)),
            pl.BlockSpec((None, e, cap, 1), lambda b, i, st: (b, 0, 0, 0)),
        ],
    )
    return pl.pallas_call(
        functools.partial(_gather_kernel, win=win),
        grid_spec=grid_spec,
        out_shape=[
            jax.ShapeDtypeStruct((bsz, e, cap, d), BF16),
            jax.ShapeDtypeStruct((bsz, e, cap, 1), F32),
        ],
        compiler_params=_cparams("parallel", "arbitrary"),
        name="expert_gather",
    )(starts, pos, aff, n3)


def _expert_kernel(x_ref, gate_ref, wg_ref, wu_ref, wd_ref, o_ref, acc_sc, *, last):
    s = pl.program_id(1)
    b = pl.program_id(2)
    x = x_ref[...]
    g = jnp.dot(x, wg_ref[...].astype(BF16), preferred_element_type=F32)
    u = jnp.dot(x, wu_ref[...].astype(BF16), preferred_element_type=F32)
    h = (g * jax.nn.sigmoid(g) * u).astype(BF16)
    y = jnp.dot(h, wd_ref[...].astype(BF16), preferred_element_type=F32)

    @pl.when(s == 0)
    def _():
        acc_sc[b] = y
        o_ref[...] = y.astype(BF16)

    @pl.when(jnp.logical_and(s > 0, s < last))
    def _():
        total = acc_sc[b] + y
        acc_sc[b] = total
        o_ref[...] = total.astype(BF16)

    @pl.when(s == last)
    def _():
        o_ref[...] = ((acc_sc[b] + y) * gate_ref[...]).astype(BF16)


def _experts(xg, gate, wg4, wu4, wd4, layer, f_slices=2):
    bsz, e, cap, d = xg.shape
    f = wg4.shape[-1]
    assert f_slices >= 2
    fs = f // f_slices
    last = f_slices - 1
    out_idx = lambda x, s, b: (jnp.where(s == last, b, 0), x, 0, 0)
    return pl.pallas_call(
        functools.partial(_expert_kernel, last=last),
        grid=(e, f_slices, bsz),
        in_specs=[
            pl.BlockSpec((None, None, cap, d), lambda x, s, b: (b, x, 0, 0)),
            pl.BlockSpec((None, None, cap, 1), lambda x, s, b: (b, x, 0, 0)),
            pl.BlockSpec((None, None, d, fs), lambda x, s, b: (layer, x, 0, s)),
            pl.BlockSpec((None, None, d, fs), lambda x, s, b: (layer, x, 0, s)),
            pl.BlockSpec((None, None, fs, d), lambda x, s, b: (layer, x, s, 0)),
        ],
        out_specs=pl.BlockSpec((None, None, cap, d), out_idx),
        out_shape=jax.ShapeDtypeStruct((bsz, e, cap, d), BF16),
        scratch_shapes=[pltpu.VMEM((bsz, cap, d), F32)],
        compiler_params=_cparams("arbitrary", "arbitrary", "arbitrary"),
        name="expert_ffn",
    )(xg, gate, wg4, wu4, wd4)


def _combine_kernel(starts_ref, x_ref, pos_ref, ye_ref, g_ref, o_ref, stage_sc, *, final, win):
    b = pl.program_id(0)
    i = pl.program_id(1)
    ts = x_ref.shape[0]
    e, cap, _ = ye_ref.shape
    lane = lax.broadcasted_iota(jnp.int32, (ts, win), 1)
    pos = pos_ref[...]

    def first_row(x):
        lo = starts_ref[b, x, i]
        return pl.multiple_of(jnp.minimum((lo // BF16_ROWS) * BF16_ROWS, cap - win), BF16_ROWS)

    onehots = []
    for x in range(e):
        base = first_row(x)
        stage_sc[pl.ds(x * win, win), :] = ye_ref[x, pl.ds(base, win), :]
        onehots.append((pos[:, x:x + 1] - base == lane).astype(BF16))
    o_ref[...] = x_ref[...] + jnp.dot(jnp.concatenate(onehots, axis=1), stage_sc[...],
                                      preferred_element_type=F32)

    for x in range(e):
        base = first_row(x)
        hi = starts_ref[b, x, i + 1]
        col = pos[:, x:x + 1]

        def extra(k, carry, base=base, col=col, x=x):
            want = base + k * win
            row = pl.multiple_of(jnp.minimum(want, cap - win), BF16_ROWS)
            onehot = jnp.logical_and(col - row == lane, col >= want).astype(BF16)
            o_ref[...] += jnp.dot(onehot, ye_ref[x, pl.ds(row, win), :], preferred_element_type=F32)
            return carry

        windows = (jnp.maximum(hi - base, 1) + win - 1) // win
        lax.fori_loop(1, windows, extra, 0)

    if final:
        acc = o_ref[...]
        o_ref[...] = acc * lax.rsqrt(jnp.mean(acc * acc, axis=-1, keepdims=True) + NORM_EPS) * g_ref[...]


def _combine(x3, pos_t, starts, ye, g, final, ts=512, win=128):
    bsz, seq, d = x3.shape
    e, cap = ye.shape[1], ye.shape[2]
    assert cap % BF16_ROWS == 0 and win % BF16_ROWS == 0 and win <= cap
    grid_spec = pltpu.PrefetchScalarGridSpec(
        num_scalar_prefetch=1,
        grid=(bsz, seq // ts),
        in_specs=[
            pl.BlockSpec((None, ts, d), lambda b, i, st: (b, i, 0)),
            pl.BlockSpec((None, ts, e), lambda b, i, st: (b, i, 0)),
            _single((None, e, cap, d), lambda b, i, st: (b, 0, 0, 0)),
            pl.BlockSpec((1, d), lambda b, i, st: (0, 0)),
        ],
        out_specs=pl.BlockSpec((None, ts, d), lambda b, i, st: (b, i, 0)),
        scratch_shapes=[pltpu.VMEM((e * win, d), BF16)],
    )
    return pl.pallas_call(
        functools.partial(_combine_kernel, final=final, win=win),
        grid_spec=grid_spec,
        out_shape=jax.ShapeDtypeStruct((bsz, seq, d), F32),
        compiler_params=_cparams("parallel", "parallel"),
        name="expert_combine",
    )(starts, x3, pos_t, ye, g)


def kernel(x, norm_mix, w_in, b_in, hy_conv_w, hy_conv_b, hy_ffn_w1, hy_ffn_b1, hy_ffn_f1, hy_ffn_w2, hy_ffn_b2, hy_ffn_f2, hy_ffn_w3, hy_bias, lambda_q1, lambda_k1, lambda_q2, lambda_k2, subln_g, w_up_hyena, w_up_attn, w_out, norm_ffn, w_router, b_router, w_e_gate, w_e_up, w_e_down, norm_final):
    bsz, seq, d = x.shape
    depth = w_in.shape[0]
    orders, c = hy_bias.shape[1], hy_bias.shape[2]
    head_dim = lambda_q1.shape[1]
    v_width = w_up_attn.shape[1]
    qk_width = v_width
    e = w_router.shape[2]
    cap = EC_FACTOR * seq // e
    col_q = (orders + 1) * c
    col_gate = col_q + 2 * qk_width + v_width
    assert orders == 2 and bsz % 2 == 0 and col_gate % d == 0

    consts = tuple(jnp.asarray(a, F32).astype(BF16) for a in _dft_constants(seq))

    xs = x.reshape(bsz * seq, d)
    out = None
    for l in range(depth):
        p2 = _inproj(xs, norm_mix[l][None], w_in[l].astype(BF16), b_in[l][None])
        p3 = p2.reshape(bsz, seq, -1)

        uc = _shortconv(p3, hy_conv_w[l], hy_conv_b[l][None], c)
        hfull = _filters(hy_ffn_w1[l], hy_ffn_b1[l], hy_ffn_f1[l], hy_ffn_w2[l], hy_ffn_b2[l], hy_ffn_f2[l],
                        hy_ffn_w3[l], hy_bias[l], seq)
        hspec = _spectrum(hfull, consts[4], consts[1])
        z = _longconv(uc, 0, uc, 1, hspec, 0, consts, natural_out=False)
        y_hy = _longconv(z, 0, uc, 2, hspec, 1, consts, natural_out=True)

        q_r, k_t, v_b = _rope(p2, bsz, seq, qk_width, v_width, head_dim, col_q)
        lam_init = 0.8 - 0.6 * math.exp(-0.3 * l)
        y_da = _attention(q_r.reshape(bsz, seq, qk_width), k_t, v_b.reshape(bsz, seq, v_width),
                          lambda_q1[l], lambda_k1[l], lambda_q2[l], lambda_k2[l], subln_g[l], head_dim, lam_init)

        xs, n2, aff = _merge(xs, y_hy.reshape(bsz * seq, c), y_da.reshape(bsz * seq, v_width), p2, col_gate,
                             w_up_hyena[l].astype(BF16), w_up_attn[l].astype(BF16), w_out[l].astype(BF16),
                             norm_ffn[l][None], w_router[l].T, b_router[l][:, None], bsz, seq)

        pos = _select(aff.reshape(bsz * e, seq), cap).reshape(bsz, e, seq)
        tiles = seq // COMBINE_TILE
        counts = jnp.sum((pos >= 0).reshape(bsz, e, tiles, COMBINE_TILE), axis=-1, dtype=jnp.int32)
        starts = jnp.concatenate([jnp.zeros((bsz, e, 1), jnp.int32), jnp.cumsum(counts, axis=-1)], axis=-1)
        xg, gate = _gather(pos, aff, starts, n2.reshape(bsz, seq, d), cap)
        ye = _experts(xg, gate, w_e_gate, w_e_up, w_e_down, l)
        final = l == depth - 1
        out = _combine(xs.reshape(bsz, seq, d), pos.transpose(0, 2, 1), starts, ye, norm_final[None], final,
                       ts=COMBINE_TILE)
        xs = out.reshape(bsz * seq, d)
    return out
```

```python
import functools
import math

import numpy as np
import jax
import jax.numpy as jnp
from jax import lax
from jax.experimental import pallas as pl
from jax.experimental.pallas import tpu as pltpu

F32 = jnp.float32
BF16 = jnp.bfloat16

NORM_EPS = 1e-6
SUBLN_EPS = 1e-5
ROPE_THETA = 10000.0
HY_FAST_DECAY = 0.3
HY_SLOW_DECAY = 1.5
HY_TARGET = 1e-2
EC_FACTOR = 2

VMEM_LIMIT_V7X = 56 * 1024 * 1024
LANES = 128
BF16_ROWS = 16
COMBINE_TILE = 512
ATTN_SUB_ROWS = 128

DFT_N1 = 64
DFT_N2 = 128
UNROLL_STAGE_A = 4
UNROLL_STAGE_B = 2
STAGE_B_COLS = 2
UNROLL_SHORTCONV = 4


def _cparams(*sem):
    return pltpu.CompilerParams(dimension_semantics=sem, vmem_limit_bytes=VMEM_LIMIT_V7X)


def _single(block_shape, index_map):
    return pl.BlockSpec(block_shape, index_map, pipeline_mode=pl.Buffered(1))


def _rope_tables(seq, head_dim):
    half = head_dim // 2
    inv = ROPE_THETA ** (-jnp.arange(half, dtype=F32) * 2.0 / head_dim)
    ang = jnp.arange(seq, dtype=F32)[:, None] * inv[None, :]
    cos, sin = jnp.cos(ang), jnp.sin(ang)
    reps = LANES // head_dim
    return (jnp.tile(jnp.concatenate([cos, cos], axis=1), (1, reps)),
            jnp.tile(jnp.concatenate([-sin, sin], axis=1), (1, reps)))


def _inproj_kernel(x_ref, g_ref, w_ref, b_ref, cos_ref, sin_ref, p_ref, q_ref, kt_ref, v_ref, n_sc,
                   *, nu, half, scale):
    j = pl.program_id(1)

    @pl.when(j == 0)
    def _():
        x = x_ref[...]
        n = x * lax.rsqrt(jnp.mean(x * x, axis=-1, keepdims=True) + NORM_EPS) * g_ref[...]
        n_sc[...] = n.astype(BF16)

    y = jnp.dot(n_sc[...], w_ref[...], preferred_element_type=F32) + b_ref[...]

    def rot(x):
        cos = cos_ref[...]
        sin = sin_ref[...]
        lane = lax.broadcasted_iota(jnp.int32, (1, LANES), 1)
        first_half = (lane % (2 * half)) < half
        outs = []
        for c in range(x.shape[1] // LANES):
            xb = x[:, c * LANES:(c + 1) * LANES]
            partner = jnp.where(first_half, pltpu.roll(xb, LANES - half, axis=1), pltpu.roll(xb, half, axis=1))
            outs.append(xb * cos + partner * sin)
        return jnp.concatenate(outs, axis=1)

    @pl.when(jnp.logical_or(j < nu, j >= nu + 3))
    def _():
        p_ref[...] = y

    @pl.when(j == nu)
    def _():
        q_ref[...] = (rot(y) * scale).astype(BF16)

    @pl.when(j == nu + 1)
    def _():
        kt_ref[...] = rot(y).T.astype(BF16)

    @pl.when(j == nu + 2)
    def _():
        v_ref[...] = y.astype(BF16)


def _inproj(x2, g, w_bf, b, bsz, seq, hy_cols, qk_width, head_dim, tm=1024):
    t, d = x2.shape
    tn = qk_width
    width = w_bf.shape[1]
    assert hy_cols % tn == 0 and width % tn == 0 and seq % tm == 0
    nu = hy_cols // tn
    ng = width // tn - nu - 3
    ns = seq // tm
    cos_t, sin_t = _rope_tables(seq, head_dim)
    p_col = lambda j: jnp.where(j < nu, ng + j, jnp.where(j < nu + 3, ng + nu - 1, j - (nu + 3)))
    return pl.pallas_call(
        functools.partial(_inproj_kernel, nu=nu, half=head_dim // 2, scale=head_dim ** -0.5 * math.log2(math.e)),
        grid=(t // tm, width // tn),
        in_specs=[
            pl.BlockSpec((tm, d), lambda i, j: (i, 0)),
            pl.BlockSpec((1, d), lambda i, j: (0, 0)),
            pl.BlockSpec((d, tn), lambda i, j: (0, j)),
            pl.BlockSpec((1, tn), lambda i, j: (0, j)),
            pl.BlockSpec((tm, LANES), lambda i, j: (i % ns, 0)),
            pl.BlockSpec((tm, LANES), lambda i, j: (i % ns, 0)),
        ],
        out_specs=[
            pl.BlockSpec((tm, tn), lambda i, j: (i, p_col(j))),
            pl.BlockSpec((tm, tn), lambda i, j: (i, 0)),
            pl.BlockSpec((None, tn, tm), lambda i, j: (i // ns, 0, i % ns)),
            pl.BlockSpec((tm, tn), lambda i, j: (i, 0)),
        ],
        out_shape=[
            jax.ShapeDtypeStruct((t, (ng + nu) * tn), F32),
            jax.ShapeDtypeStruct((t, tn), BF16),
            jax.ShapeDtypeStruct((bsz, tn, seq), BF16),
            jax.ShapeDtypeStruct((t, tn), BF16),
        ],
        scratch_shapes=[pltpu.VMEM((tm, d), BF16)],
        compiler_params=_cparams("parallel", "arbitrary"),
        name="inproj",
    )(x2, g, w_bf, b, cos_t, sin_t)


def _shortconv_kernel(u_ref, w_ref, b_ref, o_ref, pad_sc, *, seq, n2, nh):
    zeros = jnp.zeros((8, LANES), F32)
    pad_sc[pl.ds(0, 8), :] = zeros
    pad_sc[pl.ds(8 + seq, 8), :] = zeros
    pad_sc[pl.ds(8, seq), :] = u_ref[...]
    w = w_ref[...]
    bias = b_ref[...]

    column = lambda t: pad_sc[pl.ds(7 + t, nh, stride=n2), :]

    def body(r, taps):
        prev, cur = taps
        nxt = column(r + 2)
        o_ref[pl.ds(pl.multiple_of(r * nh, nh), nh), :] = prev * w[0:1] + cur * w[1:2] + nxt * w[2:3] + bias
        return cur, nxt

    lax.fori_loop(0, n2, body, (column(0), column(1)), unroll=UNROLL_SHORTCONV)


def _shortconv(p3, u_col, conv_w, conv_b, c):
    bsz, seq, _ = p3.shape
    parts = conv_w.shape[1] // c
    cb_per_part = c // LANES
    nh = seq // DFT_N2
    return pl.pallas_call(
        functools.partial(_shortconv_kernel, seq=seq, n2=DFT_N2, nh=nh),
        grid=(bsz, parts * cb_per_part),
        in_specs=[
            pl.BlockSpec((None, seq, LANES), lambda b, j: (b, 0, u_col // LANES + j)),
            pl.BlockSpec((3, LANES), lambda b, j: (0, j)),
            pl.BlockSpec((1, LANES), lambda b, j: (0, j)),
        ],
        out_specs=pl.BlockSpec((None, None, seq, LANES), lambda b, j: (j // cb_per_part, b, 0, j % cb_per_part)),
        out_shape=jax.ShapeDtypeStruct((parts, bsz, seq, c), F32),
        scratch_shapes=[pltpu.VMEM((seq + 16, LANES), F32)],
        compiler_params=_cparams("parallel", "parallel"),
        name="shortconv",
    )(p3, conv_w, conv_b)


def _filter_kernel(w1t_ref, w1c_ref, w1s_ref, b1_ref, f1_ref, w2_ref, b2_ref, f2_ref, w3_ref, bias_ref, o_ref,
                   *, seq, tl, c, bands):
    hi = lax.Precision.HIGHEST
    lag = lambda idx: jnp.where(idx < seq, idx, 2 * seq - idx).astype(F32)
    row = lax.broadcasted_iota(jnp.int32, (tl, 1), 0) + pl.program_id(0) * tl
    t = lag(row) / (seq - 1.0)
    pos = lag(lax.broadcasted_iota(jnp.int32, (1, tl), 1) + pl.program_id(0) * tl)
    t_l = pos / (seq - 1.0)
    w_l = (2.0 * math.pi) * pos / float(seq)
    band = lax.broadcasted_iota(jnp.int32, (bands, 1), 0).astype(F32)
    fr = 1e-4 + band * ((bands - 1 - 1e-4) / (bands - 1))
    ang = fr * w_l
    pre = (w1t_ref[...] * t_l
           + jnp.dot(w1c_ref[...], jnp.cos(ang), preferred_element_type=F32, precision=hi)
           - jnp.dot(w1s_ref[...], jnp.sin(ang), preferred_element_type=F32, precision=hi)
           + b1_ref[...])
    h = jnp.sin(f1_ref[...] * pre)
    h = jnp.sin(f2_ref[...] * (jnp.dot(w2_ref[...], h, preferred_element_type=F32, precision=hi) + b2_ref[...]))
    h = jnp.dot(h.T, w3_ref[...], preferred_element_type=F32, precision=hi)
    min_decay = math.log(HY_TARGET) / HY_FAST_DECAY
    max_decay = math.log(HY_TARGET) / HY_SLOW_DECAY
    ch = lax.broadcasted_iota(jnp.int32, (1, c), 1).astype(F32)
    deltas = jnp.abs(min_decay + ch * ((max_decay - min_decay) / (c - 1)))
    decay = jnp.exp(-t * deltas)
    orders = o_ref.shape[0]
    for o in range(orders):
        taps = h[:, o * c:(o + 1) * c] * decay
        taps = jnp.where(row == 0, taps + bias_ref[o:o + 1, :], taps)
        o_ref[o] = jnp.where(row == seq, 0.0, taps)


def _filters(w1, b1, f1, w2, b2, f2, w3, bias, seq, tl=512):
    emb, ffn = w1.shape
    bands = (emb - 1) // 2
    orders, c = bias.shape
    tiles = seq // tl
    w3_dir = w3.reshape(ffn, orders, 2, c).transpose(2, 0, 1, 3).reshape(2, ffn, orders * c)
    full = lambda a: pl.BlockSpec(a.shape, lambda i: (0,) * a.ndim)
    col = lambda a: a[:, None]
    args = (w1[0:1].T, w1[1:1 + bands].T, w1[1 + bands:].T, col(b1), col(f1), w2.T, col(b2), col(f2), w3_dir, bias)
    in_specs = [full(a) for a in args]
    in_specs[8] = pl.BlockSpec((None, ffn, orders * c), lambda i: (i // tiles, 0, 0))
    return pl.pallas_call(
        functools.partial(_filter_kernel, seq=seq, tl=tl, c=c, bands=bands),
        grid=(2 * tiles,),
        in_specs=in_specs,
        out_specs=pl.BlockSpec((orders, tl, c), lambda i: (0, i, 0)),
        out_shape=jax.ShapeDtypeStruct((orders, 2 * seq, c), F32),
        compiler_params=_cparams("parallel"),
        name="hyena_filters",
    )(*args)


def _stack_complex(m):
    return np.block([[m.real, -m.imag], [m.imag, m.real]])


@functools.lru_cache(maxsize=None)
def _dft_constants(seq):
    n1, n2 = DFT_N1, DFT_N2
    n = n1 * n2
    assert n == 2 * seq
    nh = seq // n2
    k1 = np.arange(n1)[:, None].astype(np.float64)
    q = np.arange(nh)[None, :].astype(np.float64)
    qf = np.arange(n1)[None, :].astype(np.float64)
    ma = np.empty((n2, 2 * n1, 2 * nh), np.float64)
    mai = np.empty((n2, 2 * nh, 2 * n1), np.float64)
    maf = np.empty((n2, 2 * n1, n1), np.float64)
    for r in range(n2):
        e = np.exp(-2j * np.pi * (q * k1 / n1 + r * k1 / n))
        ma[r] = _stack_complex(e)
        mai[r] = _stack_complex(np.conj(e).T / n)
        ef = np.exp(-2j * np.pi * (qf * k1 / n1 + r * k1 / n))
        maf[r] = np.concatenate([ef.real, ef.imag], axis=0)
    kk = np.arange(n2)[:, None].astype(np.float64)
    rr = np.arange(n2)[None, :].astype(np.float64)
    f = np.exp(-2j * np.pi * kk * rr / n2)
    mb = _stack_complex(f)
    mbi = _stack_complex(np.conj(f).T)
    return tuple(np.asarray(a, np.float32) for a in (ma, mb, mbi, mai, maf))


def _stage_a(load_x, ma_ref, a_sc):
    rows = 2 * DFT_N1

    def body(r, carry):
        a_sc[pl.ds(pl.multiple_of(r * rows, rows), rows), :] = jnp.dot(
            ma_ref[r], load_x(r), preferred_element_type=F32)
        return carry

    lax.fori_loop(0, DFT_N2, body, 0, unroll=UNROLL_STAGE_A)


def _load_a_columns(a_sc, k1):
    cols = []
    for j in range(STAGE_B_COLS):
        ar = a_sc[pl.ds(k1 + j, DFT_N2, stride=2 * DFT_N1), :]
        ai = a_sc[pl.ds(DFT_N1 + k1 + j, DFT_N2, stride=2 * DFT_N1), :]
        cols.append(jnp.concatenate([ar, ai], axis=0))
    return jnp.concatenate(cols, axis=1).astype(BF16)


def _spectrum_kernel(h_ref, maf_ref, mb_ref, o_ref, a_sc):
    rows = 2 * DFT_N2

    def load_x(r):
        return h_ref[pl.ds(r, DFT_N1, stride=DFT_N2), :].astype(BF16)

    _stage_a(load_x, maf_ref, a_sc)

    cb = o_ref.shape[-1]

    def body(kp, carry):
        k1 = kp * STAGE_B_COLS
        z = jnp.dot(mb_ref[...], _load_a_columns(a_sc, k1), preferred_element_type=F32)
        for j in range(STAGE_B_COLS):
            o_ref[pl.ds(pl.multiple_of((k1 + j) * rows, rows), rows), :] = z[:, j * cb:(j + 1) * cb]
        return carry

    lax.fori_loop(0, DFT_N1 // STAGE_B_COLS, body, 0, unroll=UNROLL_STAGE_B)


def _spectrum(hfull, maf, mb):
    orders, n, c = hfull.shape
    rows = 2 * DFT_N1 * DFT_N2
    return pl.pallas_call(
        _spectrum_kernel,
        grid=(orders, c // LANES),
        in_specs=[
            pl.BlockSpec((None, n, LANES), lambda o, j: (o, 0, j)),
            _single(maf.shape, lambda o, j: (0, 0, 0)),
            _single(mb.shape, lambda o, j: (0, 0)),
        ],
        out_specs=pl.BlockSpec((None, rows, LANES), lambda o, j: (o, 0, j)),
        out_shape=jax.ShapeDtypeStruct((orders, rows, c), F32),
        scratch_shapes=[pltpu.VMEM((rows, LANES), F32)],
        compiler_params=_cparams("parallel", "parallel"),
        name="hyena_spectrum",
    )(hfull, maf, mb)


def _longconv_kernel(z_ref, gate_ref, h_ref, ma_ref, mb_ref, mbi_ref, mai_ref, o_ref, a_sc, v_sc,
                     *, nh, natural_out):
    def load_x(r):
        src = pl.ds(pl.multiple_of(r * nh, nh), nh)
        return jnp.concatenate([z_ref[0, src, :], z_ref[1, src, :]], axis=0).astype(BF16)

    _stage_a(load_x, ma_ref, a_sc)

    rows = 2 * DFT_N2

    cb = o_ref.shape[-1]

    def freq(kp, carry):
        k1 = kp * STAGE_B_COLS
        zf = jnp.dot(mb_ref[...], _load_a_columns(a_sc, k1), preferred_element_type=F32)
        ys = []
        for j in range(STAGE_B_COLS):
            base = pl.multiple_of((k1 + j) * rows, rows)
            hr = h_ref[pl.ds(base, DFT_N2), :]
            hi = h_ref[pl.ds(base + DFT_N2, DFT_N2), :]
            zr = zf[:DFT_N2, j * cb:(j + 1) * cb]
            zi = zf[DFT_N2:, j * cb:(j + 1) * cb]
            ys.append(jnp.concatenate([zr * hr - zi * hi, zr * hi + zi * hr], axis=0))
        v = jnp.dot(mbi_ref[...], jnp.concatenate(ys, axis=1).astype(BF16), preferred_element_type=F32)
        for j in range(STAGE_B_COLS):
            v_sc[pl.ds(pl.multiple_of((k1 + j) * rows, rows), rows), :] = v[:, j * cb:(j + 1) * cb]
        return carry

    lax.fori_loop(0, DFT_N1 // STAGE_B_COLS, freq, 0, unroll=UNROLL_STAGE_B)

    def back(r, carry):
        vr = v_sc[pl.ds(r, DFT_N1, stride=rows), :]
        vi = v_sc[pl.ds(DFT_N2 + r, DFT_N1, stride=rows), :]
        y = jnp.dot(mai_ref[r], jnp.concatenate([vr, vi], axis=0).astype(BF16), preferred_element_type=F32)
        src = pl.ds(pl.multiple_of(r * nh, nh), nh)
        dst = pl.ds(r, nh, stride=DFT_N2) if natural_out else src
        o_ref[0, dst, :] = y[:nh] * gate_ref[0, src, :]
        o_ref[1, dst, :] = y[nh:] * gate_ref[1, src, :]
        return carry

    lax.fori_loop(0, DFT_N2, back, 0, unroll=UNROLL_STAGE_A)


def _longconv(z4, z_part, gate4, gate_part, h3, order, consts, natural_out):
    _, bsz, seq, c = z4.shape
    ma, mb, mbi, mai = consts[:4]
    nh = seq // DFT_N2
    rows = 2 * DFT_N1 * DFT_N2
    pair = lambda part: (lambda j, p: (part, p, 0, j))
    return pl.pallas_call(
        functools.partial(_longconv_kernel, nh=nh, natural_out=natural_out),
        grid=(c // LANES, bsz // 2),
        in_specs=[
            _single((None, 2, seq, LANES), pair(z_part)),
            _single((None, 2, seq, LANES), pair(gate_part)),
            _single((None, rows, LANES), lambda j, p: (order, 0, j)),
            _single(ma.shape, lambda j, p: (0, 0, 0)),
            _single(mb.shape, lambda j, p: (0, 0)),
            _single(mbi.shape, lambda j, p: (0, 0)),
            _single(mai.shape, lambda j, p: (0, 0, 0)),
        ],
        out_specs=pl.BlockSpec((None, 2, seq, LANES), pair(0)),
        out_shape=jax.ShapeDtypeStruct((1, bsz, seq, c), F32),
        scratch_shapes=[pltpu.VMEM((rows, LANES), F32), pltpu.VMEM((rows, LANES), F32)],
        compiler_params=_cparams("parallel", "parallel"),
        name="hyena_longconv",
    )(z4, gate4, h3, ma, mb, mbi, mai)


def _attn_kernel(q_ref, kt_ref, v_ref, lq1_ref, lk1_ref, lq2_ref, lk2_ref, g_ref, o_ref, *, head_dim, lam_init):
    lam = (jnp.exp(jnp.sum(lq1_ref[...] * lk1_ref[...], axis=-1, keepdims=True))
           - jnp.exp(jnp.sum(lq2_ref[...] * lk2_ref[...], axis=-1, keepdims=True)) + lam_init)
    lane = lax.broadcasted_iota(jnp.int32, (1, q_ref.shape[1]), 1)
    sub = ATTN_SUB_ROWS
    nsub = q_ref.shape[0] // sub

    def scores(j):
        q = q_ref[pl.ds(j * sub, sub), :]
        zero = jnp.zeros_like(q)
        qq = jnp.concatenate([jnp.where(lane < head_dim, q, zero), jnp.where(lane >= head_dim, q, zero)], axis=0)
        return jnp.dot(qq, kt_ref[...], preferred_element_type=F32)

    v = v_ref[...]
    vd = v.shape[1]
    v_aug = jnp.concatenate([v, jnp.where(lane == 0, 1.0, 0.0).astype(BF16) + jnp.zeros_like(v)], axis=1)

    def weights(s):
        return (jnp.exp2(s - jnp.max(s, axis=-1, keepdims=True)).astype(BF16),)

    def emit(j, e):
        r = jnp.dot(e, v_aug, preferred_element_type=F32)
        o = (r[:sub, :vd] / r[:sub, vd:vd + 1]) - lam * (r[sub:, :vd] / r[sub:, vd:vd + 1])
        o = o * lax.rsqrt(jnp.mean(o * o, axis=-1, keepdims=True) + SUBLN_EPS) * g_ref[...]
        o_ref[pl.ds(j * sub, sub), :] = (o * (1.0 - lam_init)).astype(o_ref.dtype)

    s_of, a_of = {}, {}
    for t in range(nsub + 2):
        if t < nsub:
            s_of[t] = scores(t)
        if 0 <= t - 1 < nsub:
            a_of[t - 1] = weights(s_of.pop(t - 1))
        if 0 <= t - 2 < nsub:
            emit(t - 2, *a_of.pop(t - 2))


def _attention(q, kt, v, lq1, lk1, lq2, lk2, subln_g, head_dim, lam_init, tq=512):
    bsz, seq, width = q.shape
    v_dim = subln_g.shape[-1]
    assert v_dim == 2 * head_dim == LANES
    heads = width // v_dim
    vec = lambda a: pl.BlockSpec((1, a.shape[-1]), lambda b, h, i: (0, 0))
    lams = [a[None] for a in (lq1, lk1, lq2, lk2)]
    return pl.pallas_call(
        functools.partial(_attn_kernel, head_dim=head_dim, lam_init=lam_init),
        grid=(bsz, heads, seq // tq),
        in_specs=[
            pl.BlockSpec((None, tq, v_dim), lambda b, h, i: (b, i, h)),
            pl.BlockSpec((None, v_dim, seq), lambda b, h, i: (b, h, 0)),
            pl.BlockSpec((None, seq, v_dim), lambda b, h, i: (b, 0, h)),
            *[vec(a) for a in lams],
            vec(subln_g[None]),
        ],
        out_specs=pl.BlockSpec((None, tq, v_dim), lambda b, h, i: (b, i, h)),
        out_shape=jax.ShapeDtypeStruct((bsz, seq, width), BF16),
        compiler_params=_cparams("parallel", "parallel", "parallel"),
        name="diff_attention",
    )(q, kt, v, *lams, subln_g[None])


def _merge_kernel(x_ref, yh_ref, ya_ref, gh_ref, ga_ref, wuh_ref, wua_ref, wo_ref, g_ref, wr_ref, br_ref,
                  xo_ref, n_ref, aff_ref):
    mh = jnp.dot(yh_ref[...].astype(BF16), wuh_ref[...], preferred_element_type=F32)
    ma = jnp.dot(ya_ref[...].astype(BF16), wua_ref[...], preferred_element_type=F32)
    merged = jax.nn.sigmoid(gh_ref[...]) * mh + jax.nn.sigmoid(ga_ref[...]) * ma
    x = x_ref[...] + jnp.dot(merged.astype(BF16), wo_ref[...], preferred_element_type=F32)
    xo_ref[...] = x
    n = x * lax.rsqrt(jnp.mean(x * x, axis=-1, keepdims=True) + NORM_EPS) * g_ref[...]
    n_ref[...] = n.astype(BF16)
    logits = lax.dot_general(wr_ref[...], n, (((1,), (1,)), ((), ())), preferred_element_type=F32,
                             precision=lax.Precision.HIGHEST) + br_ref[...]
    e = jnp.exp(logits - jnp.max(logits, axis=0, keepdims=True))
    aff_ref[...] = e / jnp.sum(e, axis=0, keepdims=True)


def _merge(x2, yh2, ya2, p2, col_gate, wuh, wua, wo, g, wr_t, br, bsz, seq, tm=512):
    t, d = x2.shape
    c = yh2.shape[1]
    e = wr_t.shape[0]
    jg = col_gate // d
    ns = seq // tm
    const = lambda a: pl.BlockSpec(a.shape, lambda i: (0,) * a.ndim)
    return pl.pallas_call(
        _merge_kernel,
        grid=(t // tm,),
        in_specs=[
            pl.BlockSpec((tm, d), lambda i: (i, 0)),
            pl.BlockSpec((tm, c), lambda i: (i, 0)),
            pl.BlockSpec((tm, ya2.shape[1]), lambda i: (i, 0)),
            pl.BlockSpec((tm, d), lambda i: (i, jg)),
            pl.BlockSpec((tm, d), lambda i: (i, jg + 1)),
            const(wuh), const(wua), const(wo), const(g), const(wr_t), const(br),
        ],
        out_specs=[
            pl.BlockSpec((tm, d), lambda i: (i, 0)),
            pl.BlockSpec((tm, d), lambda i: (i, 0)),
            pl.BlockSpec((None, e, tm), lambda i: (i // ns, 0, i % ns)),
        ],
        out_shape=[
            jax.ShapeDtypeStruct((t, d), F32),
            jax.ShapeDtypeStruct((t, d), BF16),
            jax.ShapeDtypeStruct((bsz, e, seq), F32),
        ],
        compiler_params=_cparams("parallel"),
        name="merge_router",
    )(x2, yh2, ya2, p2, p2, wuh, wua, wo, g, wr_t, br)


def _select_kernel(aff_ref, pos_ref, *, cap):
    a = aff_ref[...]
    rows, seq = a.shape
    as_f32 = lambda b: lax.bitcast_convert_type(b, F32)
    count = lambda m: jnp.sum(jnp.where(m, 1.0, 0.0), axis=-1, keepdims=True)
    thr = jnp.zeros((rows, 1), jnp.int32)
    for bit in range(30, -1, -1):
        cand = thr | (1 << bit)
        thr = jnp.where(count(a >= as_f32(cand)) >= cap, cand, thr)
    gt = a >= as_f32(thr + 1)
    eq = (a >= as_f32(thr)) & jnp.logical_not(gt)
    need = cap - count(gt)
    tri = jnp.where(lax.broadcasted_iota(jnp.int32, (LANES, LANES), 0)
                    <= lax.broadcasted_iota(jnp.int32, (LANES, LANES), 1), 1.0, 0.0).astype(BF16)

    def exclusive_cumsum(mask):
        ones = jnp.where(mask, 1.0, 0.0)
        carry = jnp.zeros((rows, 1), F32)
        chunks = []
        for j in range(seq // LANES):
            blk = ones[:, j * LANES:(j + 1) * LANES]
            incl = jnp.dot(blk.astype(BF16), tri, preferred_element_type=F32)
            chunks.append(incl - blk + carry)
            carry = carry + jnp.sum(blk, axis=-1, keepdims=True)
        return jnp.concatenate(chunks, axis=1)

    sel = gt | (eq & (exclusive_cumsum(eq) < need))
    pos_ref[...] = jnp.where(sel, exclusive_cumsum(sel), -1.0).astype(jnp.int32)


def _select(aff_rows, cap):
    return pl.pallas_call(
        functools.partial(_select_kernel, cap=cap),
        out_shape=jax.ShapeDtypeStruct(aff_rows.shape, jnp.int32),
        compiler_params=pltpu.CompilerParams(vmem_limit_bytes=VMEM_LIMIT_V7X),
        name="expert_select",
    )(aff_rows)


def _gather_kernel(starts_ref, pos_ref, aff_ref, n_ref, o_ref, gate_ref, *, win):
    b = pl.program_id(0)
    i = pl.program_id(1)
    e, cap, _ = o_ref.shape
    ts = n_ref.shape[0]

    @pl.when(i == 0)
    def _():
        o_ref[...] = jnp.zeros_like(o_ref)
        gate_ref[...] = jnp.zeros_like(gate_ref)

    slot0 = lax.broadcasted_iota(jnp.int32, (win, ts), 0)
    pos = pos_ref[...]
    aff = aff_ref[...]

    def first_row(x):
        lo = starts_ref[b, x, i]
        return pl.multiple_of(jnp.minimum((lo // BF16_ROWS) * BF16_ROWS, cap - win), BF16_ROWS)

    def add_gates(x, dst, match):
        picked = jnp.sum(jnp.where(match, aff[x:x + 1, :], 0.0), axis=-1, keepdims=True)
        gate_ref[x, dst, :] = gate_ref[x, dst, :] + picked

    onehots = []
    for x in range(e):
        base = first_row(x)
        match = (slot0 + base) == pos[x:x + 1, :]
        add_gates(x, pl.ds(base, win), match)
        onehots.append(match.astype(BF16))
    picked = jnp.dot(jnp.concatenate(onehots, axis=0), n_ref[...], preferred_element_type=F32)
    for x in range(e):
        dst = pl.ds(first_row(x), win)
        o_ref[x, dst, :] = o_ref[x, dst, :] + picked[x * win:(x + 1) * win].astype(BF16)

    for x in range(e):
        base = first_row(x)
        hi = starts_ref[b, x, i + 1]

        def extra(k, carry, base=base, x=x):
            want = base + k * win
            row = pl.multiple_of(jnp.minimum(want, cap - win), BF16_ROWS)
            slots = slot0 + row
            match = jnp.logical_and(slots == pos[x:x + 1, :], slots >= want)
            dst = pl.ds(row, win)
            add_gates(x, dst, match)
            o_ref[x, dst, :] = o_ref[x, dst, :] + jnp.dot(match.astype(BF16), n_ref[...],
                                                          preferred_element_type=F32).astype(BF16)
            return carry

        windows = (jnp.maximum(hi - base, 1) + win - 1) // win
        lax.fori_loop(1, windows, extra, 0)


def _gather(pos, aff, starts, n3, cap, ts=COMBINE_TILE, win=128):
    bsz, e, seq = pos.shape
    d = n3.shape[-1]
    assert cap % BF16_ROWS == 0 and win % BF16_ROWS == 0 and win <= cap
    tile = pl.BlockSpec((None, e, ts), lambda b, i, st: (b, 0, i))
    grid_spec = pltpu.PrefetchScalarGridSpec(
        num_scalar_prefetch=1,
        grid=(bsz, seq // ts),
        in_specs=[tile, tile, pl.BlockSpec((None, ts, d), lambda b, i, st: (b, i, 0))],
        out_specs=[
            pl.BlockSpec((None, e, cap, d), lambda b, i, st: (b, 0, 0, 0)),
            pl.BlockSpec((None, e, cap, 1), lambda b, i, st: (b, 0, 0, 0)),
        ],
    )
    return pl.pallas_call(
        functools.partial(_gather_kernel, win=win),
        grid_spec=grid_spec,
        out_shape=[
            jax.ShapeDtypeStruct((bsz, e, cap, d), BF16),
            jax.ShapeDtypeStruct((bsz, e, cap, 1), F32),
        ],
        compiler_params=_cparams("parallel", "arbitrary"),
        name="expert_gather",
    )(starts, pos, aff, n3)


def _expert_kernel(x_ref, gate_ref, wg_ref, wu_ref, wd_ref, o_ref, acc_sc, *, last):
    s = pl.program_id(1)
    b = pl.program_id(2)
    x = x_ref[...]
    g = jnp.dot(x, wg_ref[...].astype(BF16), preferred_element_type=F32)
    u = jnp.dot(x, wu_ref[...].astype(BF16), preferred_element_type=F32)
    h = (g * jax.nn.sigmoid(g) * u).astype(BF16)
    y = jnp.dot(h, wd_ref[...].astype(BF16), preferred_element_type=F32)

    @pl.when(s == 0)
    def _():
        acc_sc[b] = y
        o_ref[...] = y.astype(BF16)

    @pl.when(jnp.logical_and(s > 0, s < last))
    def _():
        total = acc_sc[b] + y
        acc_sc[b] = total
        o_ref[...] = total.astype(BF16)

    @pl.when(s == last)
    def _():
        o_ref[...] = ((acc_sc[b] + y) * gate_ref[...]).astype(BF16)


def _experts(xg, gate, wg4, wu4, wd4, layer, f_slices=2):
    bsz, e, cap, d = xg.shape
    f = wg4.shape[-1]
    assert f_slices >= 2
    fs = f // f_slices
    last = f_slices - 1
    out_idx = lambda x, s, b: (jnp.where(s == last, b, 0), x, 0, 0)
    return pl.pallas_call(
        functools.partial(_expert_kernel, last=last),
        grid=(e, f_slices, bsz),
        in_specs=[
            pl.BlockSpec((None, None, cap, d), lambda x, s, b: (b, x, 0, 0)),
            pl.BlockSpec((None, None, cap, 1), lambda x, s, b: (b, x, 0, 0)),
            pl.BlockSpec((None, None, d, fs), lambda x, s, b: (layer, x, 0, s)),
            pl.BlockSpec((None, None, d, fs), lambda x, s, b: (layer, x, 0, s)),
            pl.BlockSpec((None, None, fs, d), lambda x, s, b: (layer, x, s, 0)),
        ],
        out_specs=pl.BlockSpec((None, None, cap, d), out_idx),
        out_shape=jax.ShapeDtypeStruct((bsz, e, cap, d), BF16),
        scratch_shapes=[pltpu.VMEM((bsz, cap, d), F32)],
        compiler_params=_cparams("arbitrary", "arbitrary", "arbitrary"),
        name="expert_ffn",
    )(xg, gate, wg4, wu4, wd4)


def _combine_kernel(starts_ref, x_ref, pos_ref, ye_ref, g_ref, o_ref, stage_sc, *, final, win):
    b = pl.program_id(0)
    i = pl.program_id(1)
    ts = x_ref.shape[0]
    e, cap, _ = ye_ref.shape
    lane = lax.broadcasted_iota(jnp.int32, (ts, win), 1)
    pos = pos_ref[...]

    def first_row(x):
        lo = starts_ref[b, x, i]
        return pl.multiple_of(jnp.minimum((lo // BF16_ROWS) * BF16_ROWS, cap - win), BF16_ROWS)

    onehots = []
    for x in range(e):
        base = first_row(x)
        stage_sc[pl.ds(x * win, win), :] = ye_ref[x, pl.ds(base, win), :]
        onehots.append((pos[:, x:x + 1] - base == lane).astype(BF16))
    o_ref[...] = x_ref[...] + jnp.dot(jnp.concatenate(onehots, axis=1), stage_sc[...],
                                      preferred_element_type=F32)

    for x in range(e):
        base = first_row(x)
        hi = starts_ref[b, x, i + 1]
        col = pos[:, x:x + 1]

        def extra(k, carry, base=base, col=col, x=x):
            want = base + k * win
            row = pl.multiple_of(jnp.minimum(want, cap - win), BF16_ROWS)
            onehot = jnp.logical_and(col - row == lane, col >= want).astype(BF16)
            o_ref[...] += jnp.dot(onehot, ye_ref[x, pl.ds(row, win), :], preferred_element_type=F32)
            return carry

        windows = (jnp.maximum(hi - base, 1) + win - 1) // win
        lax.fori_loop(1, windows, extra, 0)

    if final:
        acc = o_ref[...]
        o_ref[...] = acc * lax.rsqrt(jnp.mean(acc * acc, axis=-1, keepdims=True) + NORM_EPS) * g_ref[...]


def _combine(x3, pos_t, starts, ye, g, final, ts=512, win=128):
    bsz, seq, d = x3.shape
    e, cap = ye.shape[1], ye.shape[2]
    assert cap % BF16_ROWS == 0 and win % BF16_ROWS == 0 and win <= cap
    grid_spec = pltpu.PrefetchScalarGridSpec(
        num_scalar_prefetch=1,
        grid=(bsz, seq // ts),
        in_specs=[
            pl.BlockSpec((None, ts, d), lambda b, i, st: (b, i, 0)),
            pl.BlockSpec((None, ts, e), lambda b, i, st: (b, i, 0)),
            _single((None, e, cap, d), lambda b, i, st: (b, 0, 0, 0)),
            pl.BlockSpec((1, d), lambda b, i, st: (0, 0)),
        ],
        out_specs=pl.BlockSpec((None, ts, d), lambda b, i, st: (b, i, 0)),
        scratch_shapes=[pltpu.VMEM((e * win, d), BF16)],
    )
    return pl.pallas_call(
        functools.partial(_combine_kernel, final=final, win=win),
        grid_spec=grid_spec,
        out_shape=jax.ShapeDtypeStruct((bsz, seq, d), F32),
        compiler_params=_cparams("parallel", "parallel"),
        name="expert_combine",
    )(starts, x3, pos_t, ye, g)


def kernel(x, norm_mix, w_in, b_in, hy_conv_w, hy_conv_b, hy_ffn_w1, hy_ffn_b1, hy_ffn_f1, hy_ffn_w2, hy_ffn_b2, hy_ffn_f2, hy_ffn_w3, hy_bias, lambda_q1, lambda_k1, lambda_q2, lambda_k2, subln_g, w_up_hyena, w_up_attn, w_out, norm_ffn, w_router, b_router, w_e_gate, w_e_up, w_e_down, norm_final):
    bsz, seq, d = x.shape
    depth = w_in.shape[0]
    orders, c = hy_bias.shape[1], hy_bias.shape[2]
    head_dim = lambda_q1.shape[1]
    v_width = w_up_attn.shape[1]
    qk_width = v_width
    e = w_router.shape[2]
    cap = EC_FACTOR * seq // e
    col_q = (orders + 1) * c
    assert orders == 2 and bsz % 2 == 0 and w_in.shape[2] == col_q + 2 * qk_width + v_width + 2 * d

    consts = tuple(jnp.asarray(a, F32).astype(BF16) for a in _dft_constants(seq))

    xs = x.reshape(bsz * seq, d)
    out = None
    for l in range(depth):
        p2, q_r, k_t, v_b = _inproj(xs, norm_mix[l][None], w_in[l].astype(BF16), b_in[l][None], bsz, seq, col_q,
                                    qk_width, head_dim)
        p3 = p2.reshape(bsz, seq, -1)

        uc = _shortconv(p3, 2 * d, hy_conv_w[l], hy_conv_b[l][None], c)
        hfull = _filters(hy_ffn_w1[l], hy_ffn_b1[l], hy_ffn_f1[l], hy_ffn_w2[l], hy_ffn_b2[l], hy_ffn_f2[l],
                        hy_ffn_w3[l], hy_bias[l], seq)
        hspec = _spectrum(hfull, consts[4], consts[1])
        z = _longconv(uc, 0, uc, 1, hspec, 0, consts, natural_out=False)
        y_hy = _longconv(z, 0, uc, 2, hspec, 1, consts, natural_out=True)

        lam_init = 0.8 - 0.6 * math.exp(-0.3 * l)
        y_da = _attention(q_r.reshape(bsz, seq, qk_width), k_t, v_b.reshape(bsz, seq, v_width),
                          lambda_q1[l], lambda_k1[l], lambda_q2[l], lambda_k2[l], subln_g[l], head_dim, lam_init)

        xs, n2, aff = _merge(xs, y_hy.reshape(bsz * seq, c), y_da.reshape(bsz * seq, v_width), p2, 0,
                             w_up_hyena[l].astype(BF16), w_up_attn[l].astype(BF16), w_out[l].astype(BF16),
                             norm_ffn[l][None], w_router[l].T, b_router[l][:, None], bsz, seq)

        pos = _select(aff.reshape(bsz * e, seq), cap).reshape(bsz, e, seq)
        tiles = seq // COMBINE_TILE
        counts = jnp.sum((pos >= 0).reshape(bsz, e, tiles, COMBINE_TILE), axis=-1, dtype=jnp.int32)
        starts = jnp.concatenate([jnp.zeros((bsz, e, 1), jnp.int32), jnp.cumsum(counts, axis=-1)], axis=-1)
        xg, gate = _gather(pos, aff, starts, n2.reshape(bsz, seq, d), cap)
        ye = _experts(xg, gate, w_e_gate, w_e_up, w_e_down, l)
        final = l == depth - 1
        out = _combine(xs.reshape(bsz, seq, d), pos.transpose(0, 2, 1), starts, ye, norm_final[None], final,
                       ts=COMBINE_TILE)
        xs = out.reshape(bsz * seq, d)
    return out
```

```python
import functools
import math

import numpy as np
import jax
import jax.numpy as jnp
from jax import lax
from jax.experimental import pallas as pl
from jax.experimental.pallas import tpu as pltpu

F32 = jnp.float32
BF16 = jnp.bfloat16

NORM_EPS = 1e-6
SUBLN_EPS = 1e-5
ROPE_THETA = 10000.0
HY_FAST_DECAY = 0.3
HY_SLOW_DECAY = 1.5
HY_TARGET = 1e-2
EC_FACTOR = 2

VMEM_LIMIT_V7X = 56 * 1024 * 1024
LANES = 128
BF16_ROWS = 16
COMBINE_TILE = 512
ATTN_SUB_ROWS = 128

DFT_N1 = 64
DFT_N2 = 128
UNROLL_STAGE_A = 32
UNROLL_STAGE_B = 8
STAGE_B_COLS = 2
UNROLL_SHORTCONV = 8


def _cparams(*sem):
    return pltpu.CompilerParams(dimension_semantics=sem, vmem_limit_bytes=VMEM_LIMIT_V7X)


def _single(block_shape, index_map):
    return pl.BlockSpec(block_shape, index_map, pipeline_mode=pl.Buffered(1))


def _inproj_kernel(x_ref, g_ref, w_ref, b_ref, o_ref, n_sc):
    @pl.when(pl.program_id(1) == 0)
    def _():
        x = x_ref[...]
        n = x * lax.rsqrt(jnp.mean(x * x, axis=-1, keepdims=True) + NORM_EPS) * g_ref[...]
        n_sc[...] = n.astype(BF16)

    o_ref[...] = jnp.dot(n_sc[...], w_ref[...], preferred_element_type=F32) + b_ref[...]


def _inproj(x2, g, w_bf, b, tm=2048, tn=1024):
    t, d = x2.shape
    width = w_bf.shape[1]
    return pl.pallas_call(
        _inproj_kernel,
        grid=(t // tm, width // tn),
        in_specs=[
            pl.BlockSpec((tm, d), lambda i, j: (i, 0)),
            pl.BlockSpec((1, d), lambda i, j: (0, 0)),
            pl.BlockSpec((d, tn), lambda i, j: (0, j)),
            pl.BlockSpec((1, tn), lambda i, j: (0, j)),
        ],
        out_specs=pl.BlockSpec((tm, tn), lambda i, j: (i, j)),
        out_shape=jax.ShapeDtypeStruct((t, width), F32),
        scratch_shapes=[pltpu.VMEM((tm, d), BF16)],
        compiler_params=_cparams("parallel", "arbitrary"),
        name="inproj",
    )(x2, g, w_bf, b)


def _shortconv_kernel(u_ref, w_ref, b_ref, o_ref, pad_sc, *, seq, n2, nh):
    zeros = jnp.zeros((8, LANES), F32)
    pad_sc[pl.ds(0, 8), :] = zeros
    pad_sc[pl.ds(8 + seq, 8), :] = zeros
    pad_sc[pl.ds(8, seq), :] = u_ref[...]
    w = w_ref[...]
    bias = b_ref[...]

    column = lambda t: pad_sc[pl.ds(7 + t, nh, stride=n2), :]

    def body(r, taps):
        prev, cur = taps
        nxt = column(r + 2)
        o_ref[pl.ds(pl.multiple_of(r * nh, nh), nh), :] = prev * w[0:1] + cur * w[1:2] + nxt * w[2:3] + bias
        return cur, nxt

    lax.fori_loop(0, n2, body, (column(0), column(1)), unroll=UNROLL_SHORTCONV)


def _shortconv(p3, conv_w, conv_b, c):
    bsz, seq, _ = p3.shape
    parts = conv_w.shape[1] // c
    cb_per_part = c // LANES
    nh = seq // DFT_N2
    return pl.pallas_call(
        functools.partial(_shortconv_kernel, seq=seq, n2=DFT_N2, nh=nh),
        grid=(bsz, parts * cb_per_part),
        in_specs=[
            pl.BlockSpec((None, seq, LANES), lambda b, j: (b, 0, j)),
            pl.BlockSpec((3, LANES), lambda b, j: (0, j)),
            pl.BlockSpec((1, LANES), lambda b, j: (0, j)),
        ],
        out_specs=pl.BlockSpec((None, None, seq, LANES), lambda b, j: (j // cb_per_part, b, 0, j % cb_per_part)),
        out_shape=jax.ShapeDtypeStruct((parts, bsz, seq, c), F32),
        scratch_shapes=[pltpu.VMEM((seq + 16, LANES), F32)],
        compiler_params=_cparams("parallel", "parallel"),
        name="shortconv",
    )(p3, conv_w, conv_b)


def _filter_kernel(w1t_ref, w1c_ref, w1s_ref, b1_ref, f1_ref, w2_ref, b2_ref, f2_ref, w3_ref, bias_ref, o_ref,
                   *, seq, tl, c, bands):
    hi = lax.Precision.HIGHEST
    lag = lambda idx: jnp.where(idx < seq, idx, 2 * seq - idx).astype(F32)
    row = lax.broadcasted_iota(jnp.int32, (tl, 1), 0) + pl.program_id(0) * tl
    t = lag(row) / (seq - 1.0)
    pos = lag(lax.broadcasted_iota(jnp.int32, (1, tl), 1) + pl.program_id(0) * tl)
    t_l = pos / (seq - 1.0)
    w_l = (2.0 * math.pi) * pos / float(seq)
    band = lax.broadcasted_iota(jnp.int32, (bands, 1), 0).astype(F32)
    fr = 1e-4 + band * ((bands - 1 - 1e-4) / (bands - 1))
    ang = fr * w_l
    pre = (w1t_ref[...] * t_l
           + jnp.dot(w1c_ref[...], jnp.cos(ang), preferred_element_type=F32, precision=hi)
           - jnp.dot(w1s_ref[...], jnp.sin(ang), preferred_element_type=F32, precision=hi)
           + b1_ref[...])
    h = jnp.sin(f1_ref[...] * pre)
    h = jnp.sin(f2_ref[...] * (jnp.dot(w2_ref[...], h, preferred_element_type=F32, precision=hi) + b2_ref[...]))
    h = jnp.dot(h.T, w3_ref[...], preferred_element_type=F32, precision=hi)
    min_decay = math.log(HY_TARGET) / HY_FAST_DECAY
    max_decay = math.log(HY_TARGET) / HY_SLOW_DECAY
    ch = lax.broadcasted_iota(jnp.int32, (1, c), 1).astype(F32)
    deltas = jnp.abs(min_decay + ch * ((max_decay - min_decay) / (c - 1)))
    decay = jnp.exp(-t * deltas)
    orders = o_ref.shape[0]
    for o in range(orders):
        taps = h[:, o * c:(o + 1) * c] * decay
        taps = jnp.where(row == 0, taps + bias_ref[o:o + 1, :], taps)
        o_ref[o] = jnp.where(row == seq, 0.0, taps)


def _filters(w1, b1, f1, w2, b2, f2, w3, bias, seq, tl=512):
    emb, ffn = w1.shape
    bands = (emb - 1) // 2
    orders, c = bias.shape
    tiles = seq // tl
    w3_dir = w3.reshape(ffn, orders, 2, c).transpose(2, 0, 1, 3).reshape(2, ffn, orders * c)
    full = lambda a: pl.BlockSpec(a.shape, lambda i: (0,) * a.ndim)
    col = lambda a: a[:, None]
    args = (w1[0:1].T, w1[1:1 + bands].T, w1[1 + bands:].T, col(b1), col(f1), w2.T, col(b2), col(f2), w3_dir, bias)
    in_specs = [full(a) for a in args]
    in_specs[8] = pl.BlockSpec((None, ffn, orders * c), lambda i: (i // tiles, 0, 0))
    return pl.pallas_call(
        functools.partial(_filter_kernel, seq=seq, tl=tl, c=c, bands=bands),
        grid=(2 * tiles,),
        in_specs=in_specs,
        out_specs=pl.BlockSpec((orders, tl, c), lambda i: (0, i, 0)),
        out_shape=jax.ShapeDtypeStruct((orders, 2 * seq, c), F32),
        compiler_params=_cparams("parallel"),
        name="hyena_filters",
    )(*args)


def _stack_complex(m):
    return np.block([[m.real, -m.imag], [m.imag, m.real]])


@functools.lru_cache(maxsize=None)
def _dft_constants(seq):
    n1, n2 = DFT_N1, DFT_N2
    n = n1 * n2
    assert n == 2 * seq
    nh = seq // n2
    k1 = np.arange(n1)[:, None].astype(np.float64)
    q = np.arange(nh)[None, :].astype(np.float64)
    qf = np.arange(n1)[None, :].astype(np.float64)
    ma = np.empty((n2, 2 * n1, 2 * nh), np.float64)
    mai = np.empty((n2, 2 * nh, 2 * n1), np.float64)
    maf = np.empty((n2, 2 * n1, n1), np.float64)
    for r in range(n2):
        e = np.exp(-2j * np.pi * (q * k1 / n1 + r * k1 / n))
        ma[r] = _stack_complex(e)
        mai[r] = _stack_complex(np.conj(e).T / n)
        ef = np.exp(-2j * np.pi * (qf * k1 / n1 + r * k1 / n))
        maf[r] = np.concatenate([ef.real, ef.imag], axis=0)
    kk = np.arange(n2)[:, None].astype(np.float64)
    rr = np.arange(n2)[None, :].astype(np.float64)
    f = np.exp(-2j * np.pi * kk * rr / n2)
    mb = _stack_complex(f)
    mbi = _stack_complex(np.conj(f).T)
    return tuple(np.asarray(a, np.float32) for a in (ma, mb, mbi, mai, maf))


def _stage_a(load_x, ma_ref, a_sc):
    rows = 2 * DFT_N1

    def body(r, carry):
        a_sc[pl.ds(pl.multiple_of(r * rows, rows), rows), :] = jnp.dot(
            ma_ref[r], load_x(r), preferred_element_type=F32)
        return carry

    lax.fori_loop(0, DFT_N2, body, 0, unroll=UNROLL_STAGE_A)


def _load_a_columns(a_sc, k1):
    cols = []
    for j in range(STAGE_B_COLS):
        ar = a_sc[pl.ds(k1 + j, DFT_N2, stride=2 * DFT_N1), :]
        ai = a_sc[pl.ds(DFT_N1 + k1 + j, DFT_N2, stride=2 * DFT_N1), :]
        cols.append(jnp.concatenate([ar, ai], axis=0))
    return jnp.concatenate(cols, axis=1).astype(BF16)


def _spectrum_kernel(h_ref, maf_ref, mb_ref, o_ref, a_sc):
    rows = 2 * DFT_N2

    def load_x(r):
        return h_ref[pl.ds(r, DFT_N1, stride=DFT_N2), :].astype(BF16)

    _stage_a(load_x, maf_ref, a_sc)

    cb = o_ref.shape[-1]

    def body(kp, carry):
        k1 = kp * STAGE_B_COLS
        z = jnp.dot(mb_ref[...], _load_a_columns(a_sc, k1), preferred_element_type=F32)
        for j in range(STAGE_B_COLS):
            o_ref[pl.ds(pl.multiple_of((k1 + j) * rows, rows), rows), :] = z[:, j * cb:(j + 1) * cb]
        return carry

    lax.fori_loop(0, DFT_N1 // STAGE_B_COLS, body, 0, unroll=UNROLL_STAGE_B)


def _spectrum(hfull, maf, mb):
    orders, n, c = hfull.shape
    rows = 2 * DFT_N1 * DFT_N2
    return pl.pallas_call(
        _spectrum_kernel,
        grid=(orders, c // LANES),
        in_specs=[
            pl.BlockSpec((None, n, LANES), lambda o, j: (o, 0, j)),
            _single(maf.shape, lambda o, j: (0, 0, 0)),
            _single(mb.shape, lambda o, j: (0, 0)),
        ],
        out_specs=pl.BlockSpec((None, rows, LANES), lambda o, j: (o, 0, j)),
        out_shape=jax.ShapeDtypeStruct((orders, rows, c), F32),
        scratch_shapes=[pltpu.VMEM((rows, LANES), F32)],
        compiler_params=_cparams("parallel", "parallel"),
        name="hyena_spectrum",
    )(hfull, maf, mb)


def _longconv_kernel(z_ref, gate_ref, h_ref, ma_ref, mb_ref, mbi_ref, mai_ref, o_ref, a_sc, v_sc,
                     *, nh, natural_out):
    def load_x(r):
        src = pl.ds(pl.multiple_of(r * nh, nh), nh)
        return jnp.concatenate([z_ref[0, src, :], z_ref[1, src, :]], axis=0).astype(BF16)

    _stage_a(load_x, ma_ref, a_sc)

    rows = 2 * DFT_N2

    cb = o_ref.shape[-1]

    def freq(kp, carry):
        k1 = kp * STAGE_B_COLS
        zf = jnp.dot(mb_ref[...], _load_a_columns(a_sc, k1), preferred_element_type=F32)
        ys = []
        for j in range(STAGE_B_COLS):
            base = pl.multiple_of((k1 + j) * rows, rows)
            hr = h_ref[pl.ds(base, DFT_N2), :]
            hi = h_ref[pl.ds(base + DFT_N2, DFT_N2), :]
            zr = zf[:DFT_N2, j * cb:(j + 1) * cb]
            zi = zf[DFT_N2:, j * cb:(j + 1) * cb]
            ys.append(jnp.concatenate([zr * hr - zi * hi, zr * hi + zi * hr], axis=0))
        v = jnp.dot(mbi_ref[...], jnp.concatenate(ys, axis=1).astype(BF16), preferred_element_type=F32)
        for j in range(STAGE_B_COLS):
            v_sc[pl.ds(pl.multiple_of((k1 + j) * rows, rows), rows), :] = v[:, j * cb:(j + 1) * cb]
        return carry

    lax.fori_loop(0, DFT_N1 // STAGE_B_COLS, freq, 0, unroll=UNROLL_STAGE_B)

    def back(r, carry):
        vr = v_sc[pl.ds(r, DFT_N1, stride=rows), :]
        vi = v_sc[pl.ds(DFT_N2 + r, DFT_N1, stride=rows), :]
        y = jnp.dot(mai_ref[r], jnp.concatenate([vr, vi], axis=0).astype(BF16), preferred_element_type=F32)
        src = pl.ds(pl.multiple_of(r * nh, nh), nh)
        dst = pl.ds(r, nh, stride=DFT_N2) if natural_out else src
        o_ref[0, dst, :] = y[:nh] * gate_ref[0, src, :]
        o_ref[1, dst, :] = y[nh:] * gate_ref[1, src, :]
        return carry

    lax.fori_loop(0, DFT_N2, back, 0, unroll=UNROLL_STAGE_A)


def _longconv(z4, z_part, gate4, gate_part, h3, order, consts, natural_out):
    _, bsz, seq, c = z4.shape
    ma, mb, mbi, mai = consts[:4]
    nh = seq // DFT_N2
    rows = 2 * DFT_N1 * DFT_N2
    pair = lambda part: (lambda j, p: (part, p, 0, j))
    return pl.pallas_call(
        functools.partial(_longconv_kernel, nh=nh, natural_out=natural_out),
        grid=(c // LANES, bsz // 2),
        in_specs=[
            _single((None, 2, seq, LANES), pair(z_part)),
            _single((None, 2, seq, LANES), pair(gate_part)),
            _single((None, rows, LANES), lambda j, p: (order, 0, j)),
            _single(ma.shape, lambda j, p: (0, 0, 0)),
            _single(mb.shape, lambda j, p: (0, 0)),
            _single(mbi.shape, lambda j, p: (0, 0)),
            _single(mai.shape, lambda j, p: (0, 0, 0)),
        ],
        out_specs=pl.BlockSpec((None, 2, seq, LANES), pair(0)),
        out_shape=jax.ShapeDtypeStruct((1, bsz, seq, c), F32),
        scratch_shapes=[pltpu.VMEM((rows, LANES), F32), pltpu.VMEM((rows, LANES), F32)],
        compiler_params=_cparams("parallel", "parallel"),
        name="hyena_longconv",
    )(z4, gate4, h3, ma, mb, mbi, mai)


def _rope_kernel(q_ref, k_ref, v_ref, cos_ref, sin_ref, qo_ref, kt_ref, vo_ref, *, half, scale):
    cos = cos_ref[...]
    sin = sin_ref[...]
    lane = lax.broadcasted_iota(jnp.int32, (1, LANES), 1)
    first_half = (lane % (2 * half)) < half

    def rot(x):
        outs = []
        for j in range(x.shape[1] // LANES):
            xb = x[:, j * LANES:(j + 1) * LANES]
            partner = jnp.where(first_half, pltpu.roll(xb, LANES - half, axis=1), pltpu.roll(xb, half, axis=1))
            outs.append(xb * cos + partner * sin)
        return jnp.concatenate(outs, axis=1)

    qo_ref[...] = (rot(q_ref[...]) * scale).astype(BF16)
    kt_ref[...] = rot(k_ref[...]).T.astype(BF16)
    vo_ref[...] = v_ref[...].astype(BF16)


def _rope(p2, bsz, seq, qk_width, v_width, head_dim, col_q, tm=512):
    t = p2.shape[0]
    assert qk_width == v_width and col_q % qk_width == 0
    jq = col_q // qk_width
    half = head_dim // 2
    inv = ROPE_THETA ** (-jnp.arange(half, dtype=F32) * 2.0 / head_dim)
    ang = jnp.arange(seq, dtype=F32)[:, None] * inv[None, :]
    cos, sin = jnp.cos(ang), jnp.sin(ang)
    reps = LANES // head_dim
    cos_t = jnp.tile(jnp.concatenate([cos, cos], axis=1), (1, reps))
    sin_t = jnp.tile(jnp.concatenate([-sin, sin], axis=1), (1, reps))
    ns = seq // tm
    return pl.pallas_call(
        functools.partial(_rope_kernel, half=half, scale=head_dim ** -0.5 * math.log2(math.e)),
        grid=(t // tm,),
        in_specs=[
            pl.BlockSpec((tm, qk_width), lambda i: (i, jq)),
            pl.BlockSpec((tm, qk_width), lambda i: (i, jq + 1)),
            pl.BlockSpec((tm, v_width), lambda i: (i, jq + 2)),
            pl.BlockSpec((tm, LANES), lambda i: (i % ns, 0)),
            pl.BlockSpec((tm, LANES), lambda i: (i % ns, 0)),
        ],
        out_specs=[
            pl.BlockSpec((tm, qk_width), lambda i: (i, 0)),
            pl.BlockSpec((None, qk_width, tm), lambda i: (i // ns, 0, i % ns)),
            pl.BlockSpec((tm, v_width), lambda i: (i, 0)),
        ],
        out_shape=[
            jax.ShapeDtypeStruct((t, qk_width), BF16),
            jax.ShapeDtypeStruct((bsz, qk_width, seq), BF16),
            jax.ShapeDtypeStruct((t, v_width), BF16),
        ],
        compiler_params=_cparams("parallel"),
        name="rope",
    )(p2, p2, p2, cos_t, sin_t)


def _attn_kernel(q_ref, kt_ref, v_ref, lq1_ref, lk1_ref, lq2_ref, lk2_ref, g_ref, o_ref, *, head_dim, lam_init):
    lam = (jnp.exp(jnp.sum(lq1_ref[...] * lk1_ref[...], axis=-1, keepdims=True))
           - jnp.exp(jnp.sum(lq2_ref[...] * lk2_ref[...], axis=-1, keepdims=True)) + lam_init)
    lane = lax.broadcasted_iota(jnp.int32, (1, q_ref.shape[1]), 1)
    sub = ATTN_SUB_ROWS
    nsub = q_ref.shape[0] // sub

    def scores(j):
        q = q_ref[pl.ds(j * sub, sub), :]
        zero = jnp.zeros_like(q)
        qq = jnp.concatenate([jnp.where(lane < head_dim, q, zero), jnp.where(lane >= head_dim, q, zero)], axis=0)
        return jnp.dot(qq, kt_ref[...], preferred_element_type=F32)

    v = v_ref[...]
    vd = v.shape[1]
    v_aug = jnp.concatenate([v, jnp.where(lane == 0, 1.0, 0.0).astype(BF16) + jnp.zeros_like(v)], axis=1)

    def weights(s):
        return (jnp.exp2(s - jnp.max(s, axis=-1, keepdims=True)).astype(BF16),)

    def emit(j, e):
        r = jnp.dot(e, v_aug, preferred_element_type=F32)
        o = (r[:sub, :vd] / r[:sub, vd:vd + 1]) - lam * (r[sub:, :vd] / r[sub:, vd:vd + 1])
        o = o * lax.rsqrt(jnp.mean(o * o, axis=-1, keepdims=True) + SUBLN_EPS) * g_ref[...]
        o_ref[pl.ds(j * sub, sub), :] = (o * (1.0 - lam_init)).astype(o_ref.dtype)

    s_of, a_of = {}, {}
    for t in range(nsub + 2):
        if t < nsub:
            s_of[t] = scores(t)
        if 0 <= t - 1 < nsub:
            a_of[t - 1] = weights(s_of.pop(t - 1))
        if 0 <= t - 2 < nsub:
            emit(t - 2, *a_of.pop(t - 2))


def _attention(q, kt, v, lq1, lk1, lq2, lk2, subln_g, head_dim, lam_init, tq=512):
    bsz, seq, width = q.shape
    v_dim = subln_g.shape[-1]
    assert v_dim == 2 * head_dim == LANES
    heads = width // v_dim
    vec = lambda a: pl.BlockSpec((1, a.shape[-1]), lambda b, h, i: (0, 0))
    lams = [a[None] for a in (lq1, lk1, lq2, lk2)]
    return pl.pallas_call(
        functools.partial(_attn_kernel, head_dim=head_dim, lam_init=lam_init),
        grid=(bsz, heads, seq // tq),
        in_specs=[
            pl.BlockSpec((None, tq, v_dim), lambda b, h, i: (b, i, h)),
            pl.BlockSpec((None, v_dim, seq), lambda b, h, i: (b, h, 0)),
            pl.BlockSpec((None, seq, v_dim), lambda b, h, i: (b, 0, h)),
            *[vec(a) for a in lams],
            vec(subln_g[None]),
        ],
        out_specs=pl.BlockSpec((None, tq, v_dim), lambda b, h, i: (b, i, h)),
        out_shape=jax.ShapeDtypeStruct((bsz, seq, width), BF16),
        compiler_params=_cparams("parallel", "parallel", "parallel"),
        name="diff_attention",
    )(q, kt, v, *lams, subln_g[None])


def _merge_kernel(x_ref, yh_ref, ya_ref, gh_ref, ga_ref, wuh_ref, wua_ref, wo_ref, g_ref, wr_ref, br_ref,
                  xo_ref, n_ref, aff_ref):
    mh = jnp.dot(yh_ref[...].astype(BF16), wuh_ref[...], preferred_element_type=F32)
    ma = jnp.dot(ya_ref[...].astype(BF16), wua_ref[...], preferred_element_type=F32)
    merged = jax.nn.sigmoid(gh_ref[...]) * mh + jax.nn.sigmoid(ga_ref[...]) * ma
    x = x_ref[...] + jnp.dot(merged.astype(BF16), wo_ref[...], preferred_element_type=F32)
    xo_ref[...] = x
    n = x * lax.rsqrt(jnp.mean(x * x, axis=-1, keepdims=True) + NORM_EPS) * g_ref[...]
    n_ref[...] = n.astype(BF16)
    logits = lax.dot_general(wr_ref[...], n, (((1,), (1,)), ((), ())), preferred_element_type=F32,
                             precision=lax.Precision.HIGHEST) + br_ref[...]
    e = jnp.exp(logits - jnp.max(logits, axis=0, keepdims=True))
    aff_ref[...] = e / jnp.sum(e, axis=0, keepdims=True)


def _merge(x2, yh2, ya2, p2, col_gate, wuh, wua, wo, g, wr_t, br, bsz, seq, tm=512):
    t, d = x2.shape
    c = yh2.shape[1]
    e = wr_t.shape[0]
    jg = col_gate // d
    ns = seq // tm
    const = lambda a: pl.BlockSpec(a.shape, lambda i: (0,) * a.ndim)
    return pl.pallas_call(
        _merge_kernel,
        grid=(t // tm,),
        in_specs=[
            pl.BlockSpec((tm, d), lambda i: (i, 0)),
            pl.BlockSpec((tm, c), lambda i: (i, 0)),
            pl.BlockSpec((tm, ya2.shape[1]), lambda i: (i, 0)),
            pl.BlockSpec((tm, d), lambda i: (i, jg)),
            pl.BlockSpec((tm, d), lambda i: (i, jg + 1)),
            const(wuh), const(wua), const(wo), const(g), const(wr_t), const(br),
        ],
        out_specs=[
            pl.BlockSpec((tm, d), lambda i: (i, 0)),
            pl.BlockSpec((tm, d), lambda i: (i, 0)),
            pl.BlockSpec((None, e, tm), lambda i: (i // ns, 0, i % ns)),
        ],
        out_shape=[
            jax.ShapeDtypeStruct((t, d), F32),
            jax.ShapeDtypeStruct((t, d), BF16),
            jax.ShapeDtypeStruct((bsz, e, seq), F32),
        ],
        compiler_params=_cparams("parallel"),
        name="merge_router",
    )(x2, yh2, ya2, p2, p2, wuh, wua, wo, g, wr_t, br)


def _select_kernel(aff_ref, pos_ref, *, cap):
    a = aff_ref[...]
    rows, seq = a.shape
    as_f32 = lambda b: lax.bitcast_convert_type(b, F32)
    count = lambda m: jnp.sum(jnp.where(m, 1.0, 0.0), axis=-1, keepdims=True)
    thr = jnp.zeros((rows, 1), jnp.int32)
    for bit in range(30, -1, -1):
        cand = thr | (1 << bit)
        thr = jnp.where(count(a >= as_f32(cand)) >= cap, cand, thr)
    gt = a >= as_f32(thr + 1)
    eq = (a >= as_f32(thr)) & jnp.logical_not(gt)
    need = cap - count(gt)
    tri = jnp.where(lax.broadcasted_iota(jnp.int32, (LANES, LANES), 0)
                    <= lax.broadcasted_iota(jnp.int32, (LANES, LANES), 1), 1.0, 0.0).astype(BF16)

    def exclusive_cumsum(mask):
        ones = jnp.where(mask, 1.0, 0.0)
        carry = jnp.zeros((rows, 1), F32)
        chunks = []
        for j in range(seq // LANES):
            blk = ones[:, j * LANES:(j + 1) * LANES]
            incl = jnp.dot(blk.astype(BF16), tri, preferred_element_type=F32)
            chunks.append(incl - blk + carry)
            carry = carry + jnp.sum(blk, axis=-1, keepdims=True)
        return jnp.concatenate(chunks, axis=1)

    sel = gt | (eq & (exclusive_cumsum(eq) < need))
    pos_ref[...] = jnp.where(sel, exclusive_cumsum(sel), -1.0).astype(jnp.int32)


def _select(aff_rows, cap):
    return pl.pallas_call(
        functools.partial(_select_kernel, cap=cap),
        out_shape=jax.ShapeDtypeStruct(aff_rows.shape, jnp.int32),
        compiler_params=pltpu.CompilerParams(vmem_limit_bytes=VMEM_LIMIT_V7X),
        name="expert_select",
    )(aff_rows)


def _gather_kernel(starts_ref, pos_ref, aff_ref, n_ref, o_ref, gate_ref, *, win):
    b = pl.program_id(0)
    i = pl.program_id(1)
    e, cap, _ = o_ref.shape
    ts = n_ref.shape[0]

    @pl.when(i == 0)
    def _():
        o_ref[...] = jnp.zeros_like(o_ref)
        gate_ref[...] = jnp.zeros_like(gate_ref)

    slot0 = lax.broadcasted_iota(jnp.int32, (win, ts), 0)
    pos = pos_ref[...]
    aff = aff_ref[...]

    def first_row(x):
        lo = starts_ref[b, x, i]
        return pl.multiple_of(jnp.minimum((lo // BF16_ROWS) * BF16_ROWS, cap - win), BF16_ROWS)

    def add_gates(x, dst, match):
        picked = jnp.sum(jnp.where(match, aff[x:x + 1, :], 0.0), axis=-1, keepdims=True)
        gate_ref[x, dst, :] = gate_ref[x, dst, :] + picked

    onehots = []
    for x in range(e):
        base = first_row(x)
        match = (slot0 + base) == pos[x:x + 1, :]
        add_gates(x, pl.ds(base, win), match)
        onehots.append(match.astype(BF16))
    picked = jnp.dot(jnp.concatenate(onehots, axis=0), n_ref[...], preferred_element_type=F32)
    for x in range(e):
        dst = pl.ds(first_row(x), win)
        o_ref[x, dst, :] = o_ref[x, dst, :] + picked[x * win:(x + 1) * win].astype(BF16)

    for x in range(e):
        base = first_row(x)
        hi = starts_ref[b, x, i + 1]

        def extra(k, carry, base=base, x=x):
            want = base + k * win
            row = pl.multiple_of(jnp.minimum(want, cap - win), BF16_ROWS)
            slots = slot0 + row
            match = jnp.logical_and(slots == pos[x:x + 1, :], slots >= want)
            dst = pl.ds(row, win)
            add_gates(x, dst, match)
            o_ref[x, dst, :] = o_ref[x, dst, :] + jnp.dot(match.astype(BF16), n_ref[...],
                                                          preferred_element_type=F32).astype(BF16)
            return carry

        windows = (jnp.maximum(hi - base, 1) + win - 1) // win
        lax.fori_loop(1, windows, extra, 0)


def _gather(pos, aff, starts, n3, cap, ts=COMBINE_TILE, win=128):
    bsz, e, seq = pos.shape
    d = n3.shape[-1]
    assert cap % BF16_ROWS == 0 and win % BF16_ROWS == 0 and win <= cap
    tile = pl.BlockSpec((None, e, ts), lambda b, i, st: (b, 0, i))
    grid_spec = pltpu.PrefetchScalarGridSpec(
        num_scalar_prefetch=1,
        grid=(bsz, seq // ts),
        in_specs=[tile, tile, pl.BlockSpec((None, ts, d), lambda b, i, st: (b, i, 0))],
        out_specs=[
            pl.BlockSpec((None, e, cap, d), lambda b, i, st: (b, 0, 0, 0)),
            pl.BlockSpec((None, e, cap, 1), lambda b, i, st: (b, 0, 0, 0)),
        ],
    )
    return pl.pallas_call(
        functools.partial(_gather_kernel, win=win),
        grid_spec=grid_spec,
        out_shape=[
            jax.ShapeDtypeStruct((bsz, e, cap, d), BF16),
            jax.ShapeDtypeStruct((bsz, e, cap, 1), F32),
        ],
        compiler_params=_cparams("parallel", "arbitrary"),
        name="expert_gather",
    )(starts, pos, aff, n3)


def _expert_kernel(x_ref, gate_ref, wg_ref, wu_ref, wd_ref, o_ref, acc_sc, *, last):
    s = pl.program_id(1)
    bg = pl.program_id(2)
    rows, cap, d = x_ref.shape
    x = x_ref[...].reshape(rows * cap, d)
    g = jnp.dot(x, wg_ref[...].astype(BF16), preferred_element_type=F32)
    u = jnp.dot(x, wu_ref[...].astype(BF16), preferred_element_type=F32)
    h = (g * jax.nn.sigmoid(g) * u).astype(BF16)
    y = jnp.dot(h, wd_ref[...].astype(BF16), preferred_element_type=F32)

    @pl.when(s == 0)
    def _():
        acc_sc[bg] = y
        o_ref[...] = y.astype(BF16).reshape(rows, cap, d)

    @pl.when(jnp.logical_and(s > 0, s < last))
    def _():
        acc_sc[bg] = acc_sc[bg] + y

    @pl.when(s == last)
    def _():
        gate = gate_ref[...].reshape(rows * cap, 1)
        o_ref[...] = ((acc_sc[bg] + y) * gate).astype(BF16).reshape(rows, cap, d)


def _experts(xg, gate, wg4, wu4, wd4, layer, f_slices=2, rows=1):
    bsz, e, cap, d = xg.shape
    f = wg4.shape[-1]
    assert f_slices >= 2 and f % f_slices == 0 and bsz % rows == 0
    fs = f // f_slices
    last = f_slices - 1
    groups = bsz // rows
    out_idx = lambda x, s, bg: (jnp.where(s == last, bg, 0), x, 0, 0)
    return pl.pallas_call(
        functools.partial(_expert_kernel, last=last),
        grid=(e, f_slices, groups),
        in_specs=[
            pl.BlockSpec((rows, None, cap, d), lambda x, s, bg: (bg, x, 0, 0)),
            pl.BlockSpec((rows, None, cap, 1), lambda x, s, bg: (bg, x, 0, 0)),
            pl.BlockSpec((None, None, d, fs), lambda x, s, bg: (layer, x, 0, s)),
            pl.BlockSpec((None, None, d, fs), lambda x, s, bg: (layer, x, 0, s)),
            pl.BlockSpec((None, None, fs, d), lambda x, s, bg: (layer, x, s, 0)),
        ],
        out_specs=pl.BlockSpec((rows, None, cap, d), out_idx),
        out_shape=jax.ShapeDtypeStruct((bsz, e, cap, d), BF16),
        scratch_shapes=[pltpu.VMEM((groups, rows * cap, d), F32)],
        compiler_params=_cparams("arbitrary", "arbitrary", "arbitrary"),
        name="expert_ffn",
    )(xg, gate, wg4, wu4, wd4)


def _combine_kernel(starts_ref, x_ref, pos_ref, ye_ref, g_ref, o_ref, stage_sc, *, final, win):
    b = pl.program_id(0)
    i = pl.program_id(1)
    ts = x_ref.shape[0]
    e, cap, _ = ye_ref.shape
    lane = lax.broadcasted_iota(jnp.int32, (ts, win), 1)
    pos = pos_ref[...]

    def first_row(x):
        lo = starts_ref[b, x, i]
        return pl.multiple_of(jnp.minimum((lo // BF16_ROWS) * BF16_ROWS, cap - win), BF16_ROWS)

    onehots = []
    for x in range(e):
        base = first_row(x)
        stage_sc[pl.ds(x * win, win), :] = ye_ref[x, pl.ds(base, win), :]
        onehots.append((pos[:, x:x + 1] - base == lane).astype(BF16))
    o_ref[...] = x_ref[...] + jnp.dot(jnp.concatenate(onehots, axis=1), stage_sc[...],
                                      preferred_element_type=F32)

    for x in range(e):
        base = first_row(x)
        hi = starts_ref[b, x, i + 1]
        col = pos[:, x:x + 1]

        def extra(k, carry, base=base, col=col, x=x):
            want = base + k * win
            row = pl.multiple_of(jnp.minimum(want, cap - win), BF16_ROWS)
            onehot = jnp.logical_and(col - row == lane, col >= want).astype(BF16)
            o_ref[...] += jnp.dot(onehot, ye_ref[x, pl.ds(row, win), :], preferred_element_type=F32)
            return carry

        windows = (jnp.maximum(hi - base, 1) + win - 1) // win
        lax.fori_loop(1, windows, extra, 0)

    if final:
        acc = o_ref[...]
        o_ref[...] = acc * lax.rsqrt(jnp.mean(acc * acc, axis=-1, keepdims=True) + NORM_EPS) * g_ref[...]


def _combine(x3, pos_t, starts, ye, g, final, ts=512, win=128):
    bsz, seq, d = x3.shape
    e, cap = ye.shape[1], ye.shape[2]
    assert cap % BF16_ROWS == 0 and win % BF16_ROWS == 0 and win <= cap
    grid_spec = pltpu.PrefetchScalarGridSpec(
        num_scalar_prefetch=1,
        grid=(bsz, seq // ts),
        in_specs=[
            pl.BlockSpec((None, ts, d), lambda b, i, st: (b, i, 0)),
            pl.BlockSpec((None, ts, e), lambda b, i, st: (b, i, 0)),
            _single((None, e, cap, d), lambda b, i, st: (b, 0, 0, 0)),
            pl.BlockSpec((1, d), lambda b, i, st: (0, 0)),
        ],
        out_specs=pl.BlockSpec((None, ts, d), lambda b, i, st: (b, i, 0)),
        scratch_shapes=[pltpu.VMEM((e * win, d), BF16)],
    )
    return pl.pallas_call(
        functools.partial(_combine_kernel, final=final, win=win),
        grid_spec=grid_spec,
        out_shape=jax.ShapeDtypeStruct((bsz, seq, d), F32),
        compiler_params=_cparams("parallel", "parallel"),
        name="expert_combine",
    )(starts, x3, pos_t, ye, g)


def kernel(x, norm_mix, w_in, b_in, hy_conv_w, hy_conv_b, hy_ffn_w1, hy_ffn_b1, hy_ffn_f1, hy_ffn_w2, hy_ffn_b2, hy_ffn_f2, hy_ffn_w3, hy_bias, lambda_q1, lambda_k1, lambda_q2, lambda_k2, subln_g, w_up_hyena, w_up_attn, w_out, norm_ffn, w_router, b_router, w_e_gate, w_e_up, w_e_down, norm_final):
    bsz, seq, d = x.shape
    depth = w_in.shape[0]
    orders, c = hy_bias.shape[1], hy_bias.shape[2]
    head_dim = lambda_q1.shape[1]
    v_width = w_up_attn.shape[1]
    qk_width = v_width
    e = w_router.shape[2]
    cap = EC_FACTOR * seq // e
    col_q = (orders + 1) * c
    col_gate = col_q + 2 * qk_width + v_width
    assert orders == 2 and bsz % 2 == 0 and col_gate % d == 0

    consts = tuple(jnp.asarray(a, F32).astype(BF16) for a in _dft_constants(seq))

    xs = x.reshape(bsz * seq, d)
    out = None
    for l in range(depth):
        p2 = _inproj(xs, norm_mix[l][None], w_in[l].astype(BF16), b_in[l][None])
        p3 = p2.reshape(bsz, seq, -1)

        uc = _shortconv(p3, hy_conv_w[l], hy_conv_b[l][None], c)
        hfull = _filters(hy_ffn_w1[l], hy_ffn_b1[l], hy_ffn_f1[l], hy_ffn_w2[l], hy_ffn_b2[l], hy_ffn_f2[l],
                        hy_ffn_w3[l], hy_bias[l], seq)
        hspec = _spectrum(hfull, consts[4], consts[1])
        z = _longconv(uc, 0, uc, 1, hspec, 0, consts, natural_out=False)
        y_hy = _longconv(z, 0, uc, 2, hspec, 1, consts, natural_out=True)

        q_r, k_t, v_b = _rope(p2, bsz, seq, qk_width, v_width, head_dim, col_q)
        lam_init = 0.8 - 0.6 * math.exp(-0.3 * l)
        y_da = _attention(q_r.reshape(bsz, seq, qk_width), k_t, v_b.reshape(bsz, seq, v_width),
                          lambda_q1[l], lambda_k1[l], lambda_q2[l], lambda_k2[l], subln_g[l], head_dim, lam_init)

        xs, n2, aff = _merge(xs, y_hy.reshape(bsz * seq, c), y_da.reshape(bsz * seq, v_width), p2, col_gate,
                             w_up_hyena[l].astype(BF16), w_up_attn[l].astype(BF16), w_out[l].astype(BF16),
                             norm_ffn[l][None], w_router[l].T, b_router[l][:, None], bsz, seq)

        pos = _select(aff.reshape(bsz * e, seq), cap).reshape(bsz, e, seq)
        tiles = seq // COMBINE_TILE
        counts = jnp.sum((pos >= 0).reshape(bsz, e, tiles, COMBINE_TILE), axis=-1, dtype=jnp.int32)
        starts = jnp.concatenate([jnp.zeros((bsz, e, 1), jnp.int32), jnp.cumsum(counts, axis=-1)], axis=-1)
        xg, gate = _gather(pos, aff, starts, n2.reshape(bsz, seq, d), cap)
        ye = _experts(xg, gate, w_e_gate, w_e_up, w_e_down, l)
        final = l == depth - 1
        out = _combine(xs.reshape(bsz, seq, d), pos.transpose(0, 2, 1), starts, ye, norm_final[None], final,
                       ts=COMBINE_TILE)
        xs = out.reshape(bsz * seq, d)
    return out
```

```python
import functools
import math

import numpy as np
import jax
import jax.numpy as jnp
from jax import lax
from jax.experimental import pallas as pl
from jax.experimental.pallas import tpu as pltpu

F32 = jnp.float32
BF16 = jnp.bfloat16

NORM_EPS = 1e-6
SUBLN_EPS = 1e-5
ROPE_THETA = 10000.0
HY_FAST_DECAY = 0.3
HY_SLOW_DECAY = 1.5
HY_TARGET = 1e-2
EC_FACTOR = 2

VMEM_LIMIT_V7X = 56 * 1024 * 1024
LANES = 128
BF16_ROWS = 16
COMBINE_TILE = 512
GATHER_GROUP = 16
MERGE_ROW_SPLIT = 1
EXPERT_ROW_SPLIT = 1
ATTN_SUB_ROWS = 128

DFT_N1 = 64
DFT_N2 = 128
UNROLL_STAGE_A = 64
UNROLL_STAGE_B = 16
STAGE_B_COLS = 2
UNROLL_SHORTCONV = 8


def _cparams(*sem):
    return pltpu.CompilerParams(dimension_semantics=sem, vmem_limit_bytes=VMEM_LIMIT_V7X)


def _single(block_shape, index_map):
    return pl.BlockSpec(block_shape, index_map, pipeline_mode=pl.Buffered(1))


def _inproj_kernel(x_ref, g_ref, w_ref, b_ref, o_ref, n_sc):
    @pl.when(pl.program_id(1) == 0)
    def _():
        x = x_ref[...]
        n = x * lax.rsqrt(jnp.mean(x * x, axis=-1, keepdims=True) + NORM_EPS) * g_ref[...]
        n_sc[...] = n.astype(BF16)

    o_ref[...] = jnp.dot(n_sc[...], w_ref[...], preferred_element_type=F32) + b_ref[...]


def _inproj(x2, g, w_bf, b, tm=2048, tn=1024):
    t, d = x2.shape
    width = w_bf.shape[1]
    return pl.pallas_call(
        _inproj_kernel,
        grid=(t // tm, width // tn),
        in_specs=[
            pl.BlockSpec((tm, d), lambda i, j: (i, 0)),
            pl.BlockSpec((1, d), lambda i, j: (0, 0)),
            pl.BlockSpec((d, tn), lambda i, j: (0, j)),
            pl.BlockSpec((1, tn), lambda i, j: (0, j)),
        ],
        out_specs=pl.BlockSpec((tm, tn), lambda i, j: (i, j)),
        out_shape=jax.ShapeDtypeStruct((t, width), F32),
        scratch_shapes=[pltpu.VMEM((tm, d), BF16)],
        compiler_params=_cparams("parallel", "arbitrary"),
        name="inproj",
    )(x2, g, w_bf, b)


def _shortconv_kernel(u_ref, w_ref, b_ref, o_ref, pad_sc, *, seq, n2, nh):
    zeros = jnp.zeros((8, LANES), F32)
    pad_sc[pl.ds(0, 8), :] = zeros
    pad_sc[pl.ds(8 + seq, 8), :] = zeros
    pad_sc[pl.ds(8, seq), :] = u_ref[...]
    w = w_ref[...]
    bias = b_ref[...]

    column = lambda t: pad_sc[pl.ds(7 + t, nh, stride=n2), :]

    def body(r, taps):
        prev, cur = taps
        nxt = column(r + 2)
        o_ref[pl.ds(pl.multiple_of(r * nh, nh), nh), :] = prev * w[0:1] + cur * w[1:2] + nxt * w[2:3] + bias
        return cur, nxt

    lax.fori_loop(0, n2, body, (column(0), column(1)), unroll=UNROLL_SHORTCONV)


def _shortconv(p3, conv_w, conv_b, c):
    bsz, seq, _ = p3.shape
    parts = conv_w.shape[1] // c
    cb_per_part = c // LANES
    nh = seq // DFT_N2
    return pl.pallas_call(
        functools.partial(_shortconv_kernel, seq=seq, n2=DFT_N2, nh=nh),
        grid=(bsz, parts * cb_per_part),
        in_specs=[
            pl.BlockSpec((None, seq, LANES), lambda b, j: (b, 0, j)),
            pl.BlockSpec((3, LANES), lambda b, j: (0, j)),
            pl.BlockSpec((1, LANES), lambda b, j: (0, j)),
        ],
        out_specs=pl.BlockSpec((None, None, seq, LANES), lambda b, j: (j // cb_per_part, b, 0, j % cb_per_part)),
        out_shape=jax.ShapeDtypeStruct((parts, bsz, seq, c), F32),
        scratch_shapes=[pltpu.VMEM((seq + 16, LANES), F32)],
        compiler_params=_cparams("parallel", "parallel"),
        name="shortconv",
    )(p3, conv_w, conv_b)


def _filter_kernel(w1t_ref, w1c_ref, w1s_ref, b1_ref, f1_ref, w2_ref, b2_ref, f2_ref, w3_ref, bias_ref, o_ref,
                   *, seq, tl, c, bands):
    hi = lax.Precision.HIGHEST
    lag = lambda idx: jnp.where(idx < seq, idx, 2 * seq - idx).astype(F32)
    row = lax.broadcasted_iota(jnp.int32, (tl, 1), 0) + pl.program_id(0) * tl
    t = lag(row) / (seq - 1.0)
    pos = lag(lax.broadcasted_iota(jnp.int32, (1, tl), 1) + pl.program_id(0) * tl)
    t_l = pos / (seq - 1.0)
    w_l = (2.0 * math.pi) * pos / float(seq)
    band = lax.broadcasted_iota(jnp.int32, (bands, 1), 0).astype(F32)
    fr = 1e-4 + band * ((bands - 1 - 1e-4) / (bands - 1))
    ang = fr * w_l
    pre = (w1t_ref[...] * t_l
           + jnp.dot(w1c_ref[...], jnp.cos(ang), preferred_element_type=F32, precision=hi)
           - jnp.dot(w1s_ref[...], jnp.sin(ang), preferred_element_type=F32, precision=hi)
           + b1_ref[...])
    h = jnp.sin(f1_ref[...] * pre)
    h = jnp.sin(f2_ref[...] * (jnp.dot(w2_ref[...], h, preferred_element_type=F32, precision=hi) + b2_ref[...]))
    h = jnp.dot(h.T, w3_ref[...], preferred_element_type=F32, precision=hi)
    min_decay = math.log(HY_TARGET) / HY_FAST_DECAY
    max_decay = math.log(HY_TARGET) / HY_SLOW_DECAY
    ch = lax.broadcasted_iota(jnp.int32, (1, c), 1).astype(F32)
    deltas = jnp.abs(min_decay + ch * ((max_decay - min_decay) / (c - 1)))
    decay = jnp.exp(-t * deltas)
    orders = o_ref.shape[0]
    for o in range(orders):
        taps = h[:, o * c:(o + 1) * c] * decay
        taps = jnp.where(row == 0, taps + bias_ref[o:o + 1, :], taps)
        o_ref[o] = jnp.where(row == seq, 0.0, taps)


def _filters(w1, b1, f1, w2, b2, f2, w3, bias, seq, tl=512):
    emb, ffn = w1.shape
    bands = (emb - 1) // 2
    orders, c = bias.shape
    tiles = seq // tl
    w3_dir = w3.reshape(ffn, orders, 2, c).transpose(2, 0, 1, 3).reshape(2, ffn, orders * c)
    full = lambda a: pl.BlockSpec(a.shape, lambda i: (0,) * a.ndim)
    col = lambda a: a[:, None]
    args = (w1[0:1].T, w1[1:1 + bands].T, w1[1 + bands:].T, col(b1), col(f1), w2.T, col(b2), col(f2), w3_dir, bias)
    in_specs = [full(a) for a in args]
    in_specs[8] = pl.BlockSpec((None, ffn, orders * c), lambda i: (i // tiles, 0, 0))
    return pl.pallas_call(
        functools.partial(_filter_kernel, seq=seq, tl=tl, c=c, bands=bands),
        grid=(2 * tiles,),
        in_specs=in_specs,
        out_specs=pl.BlockSpec((orders, tl, c), lambda i: (0, i, 0)),
        out_shape=jax.ShapeDtypeStruct((orders, 2 * seq, c), F32),
        compiler_params=_cparams("parallel"),
        name="hyena_filters",
    )(*args)


def _stack_complex(m):
    return np.block([[m.real, -m.imag], [m.imag, m.real]])


@functools.lru_cache(maxsize=None)
def _dft_constants(seq):
    n1, n2 = DFT_N1, DFT_N2
    n = n1 * n2
    assert n == 2 * seq
    nh = seq // n2
    k1 = np.arange(n1)[:, None].astype(np.float64)
    q = np.arange(nh)[None, :].astype(np.float64)
    qf = np.arange(n1)[None, :].astype(np.float64)
    ma = np.empty((n2, 2 * n1, 2 * nh), np.float64)
    mai = np.empty((n2, 2 * nh, 2 * n1), np.float64)
    maf = np.empty((n2, 2 * n1, n1), np.float64)
    for r in range(n2):
        e = np.exp(-2j * np.pi * (q * k1 / n1 + r * k1 / n))
        ma[r] = _stack_complex(e)
        mai[r] = _stack_complex(np.conj(e).T / n)
        ef = np.exp(-2j * np.pi * (qf * k1 / n1 + r * k1 / n))
        maf[r] = np.concatenate([ef.real, ef.imag], axis=0)
    kk = np.arange(n2)[:, None].astype(np.float64)
    rr = np.arange(n2)[None, :].astype(np.float64)
    f = np.exp(-2j * np.pi * kk * rr / n2)
    mb = _stack_complex(f)
    mbi = _stack_complex(np.conj(f).T)
    return tuple(np.asarray(a, np.float32) for a in (ma, mb, mbi, mai, maf))


def _stage_a(load_x, ma_ref, a_sc):
    rows = 2 * DFT_N1

    def body(r, carry):
        a_sc[pl.ds(pl.multiple_of(r * rows, rows), rows), :] = jnp.dot(
            ma_ref[r], load_x(r), preferred_element_type=F32)
        return carry

    lax.fori_loop(0, DFT_N2, body, 0, unroll=UNROLL_STAGE_A)


def _load_a_columns(a_sc, k1):
    cols = []
    for j in range(STAGE_B_COLS):
        ar = a_sc[pl.ds(k1 + j, DFT_N2, stride=2 * DFT_N1), :]
        ai = a_sc[pl.ds(DFT_N1 + k1 + j, DFT_N2, stride=2 * DFT_N1), :]
        cols.append(jnp.concatenate([ar, ai], axis=0))
    return jnp.concatenate(cols, axis=1).astype(BF16)


def _spectrum_kernel(h_ref, maf_ref, mb_ref, o_ref, a_sc):
    rows = 2 * DFT_N2

    def load_x(r):
        return h_ref[pl.ds(r, DFT_N1, stride=DFT_N2), :].astype(BF16)

    _stage_a(load_x, maf_ref, a_sc)

    cb = o_ref.shape[-1]

    def body(kp, carry):
        k1 = kp * STAGE_B_COLS
        z = jnp.dot(mb_ref[...], _load_a_columns(a_sc, k1), preferred_element_type=F32)
        for j in range(STAGE_B_COLS):
            o_ref[pl.ds(pl.multiple_of((k1 + j) * rows, rows), rows), :] = z[:, j * cb:(j + 1) * cb]
        return carry

    lax.fori_loop(0, DFT_N1 // STAGE_B_COLS, body, 0, unroll=UNROLL_STAGE_B)


def _spectrum(hfull, maf, mb):
    orders, n, c = hfull.shape
    rows = 2 * DFT_N1 * DFT_N2
    return pl.pallas_call(
        _spectrum_kernel,
        grid=(orders, c // LANES),
        in_specs=[
            pl.BlockSpec((None, n, LANES), lambda o, j: (o, 0, j)),
            _single(maf.shape, lambda o, j: (0, 0, 0)),
            _single(mb.shape, lambda o, j: (0, 0)),
        ],
        out_specs=pl.BlockSpec((None, rows, LANES), lambda o, j: (o, 0, j)),
        out_shape=jax.ShapeDtypeStruct((orders, rows, c), F32),
        scratch_shapes=[pltpu.VMEM((rows, LANES), F32)],
        compiler_params=_cparams("parallel", "parallel"),
        name="hyena_spectrum",
    )(hfull, maf, mb)


def _longconv_kernel(z_ref, gate_ref, h_ref, ma_ref, mb_ref, mbi_ref, mai_ref, o_ref, a_sc, v_sc,
                     *, nh, natural_out):
    def load_x(r):
        src = pl.ds(pl.multiple_of(r * nh, nh), nh)
        return jnp.concatenate([z_ref[0, src, :], z_ref[1, src, :]], axis=0).astype(BF16)

    _stage_a(load_x, ma_ref, a_sc)

    rows = 2 * DFT_N2

    cb = o_ref.shape[-1]

    def freq(kp, carry):
        k1 = kp * STAGE_B_COLS
        zf = jnp.dot(mb_ref[...], _load_a_columns(a_sc, k1), preferred_element_type=F32)
        ys = []
        for j in range(STAGE_B_COLS):
            base = pl.multiple_of((k1 + j) * rows, rows)
            hr = h_ref[pl.ds(base, DFT_N2), :]
            hi = h_ref[pl.ds(base + DFT_N2, DFT_N2), :]
            zr = zf[:DFT_N2, j * cb:(j + 1) * cb]
            zi = zf[DFT_N2:, j * cb:(j + 1) * cb]
            ys.append(jnp.concatenate([zr * hr - zi * hi, zr * hi + zi * hr], axis=0))
        v = jnp.dot(mbi_ref[...], jnp.concatenate(ys, axis=1).astype(BF16), preferred_element_type=F32)
        for j in range(STAGE_B_COLS):
            v_sc[pl.ds(pl.multiple_of((k1 + j) * rows, rows), rows), :] = v[:, j * cb:(j + 1) * cb]
        return carry

    lax.fori_loop(0, DFT_N1 // STAGE_B_COLS, freq, 0, unroll=UNROLL_STAGE_B)

    def back(r, carry):
        vr = v_sc[pl.ds(r, DFT_N1, stride=rows), :]
        vi = v_sc[pl.ds(DFT_N2 + r, DFT_N1, stride=rows), :]
        y = jnp.dot(mai_ref[r], jnp.concatenate([vr, vi], axis=0).astype(BF16), preferred_element_type=F32)
        src = pl.ds(pl.multiple_of(r * nh, nh), nh)
        dst = pl.ds(r, nh, stride=DFT_N2) if natural_out else src
        o_ref[0, dst, :] = y[:nh] * gate_ref[0, src, :]
        o_ref[1, dst, :] = y[nh:] * gate_ref[1, src, :]
        return carry

    lax.fori_loop(0, DFT_N2, back, 0, unroll=UNROLL_STAGE_A)


def _longconv(z4, z_part, gate4, gate_part, h3, order, consts, natural_out):
    _, bsz, seq, c = z4.shape
    ma, mb, mbi, mai = consts[:4]
    nh = seq // DFT_N2
    rows = 2 * DFT_N1 * DFT_N2
    pair = lambda part: (lambda j, p: (part, p, 0, j))
    return pl.pallas_call(
        functools.partial(_longconv_kernel, nh=nh, natural_out=natural_out),
        grid=(c // LANES, bsz // 2),
        in_specs=[
            _single((None, 2, seq, LANES), pair(z_part)),
            _single((None, 2, seq, LANES), pair(gate_part)),
            _single((None, rows, LANES), lambda j, p: (order, 0, j)),
            _single(ma.shape, lambda j, p: (0, 0, 0)),
            _single(mb.shape, lambda j, p: (0, 0)),
            _single(mbi.shape, lambda j, p: (0, 0)),
            _single(mai.shape, lambda j, p: (0, 0, 0)),
        ],
        out_specs=pl.BlockSpec((None, 2, seq, LANES), pair(0)),
        out_shape=jax.ShapeDtypeStruct((1, bsz, seq, c), F32),
        scratch_shapes=[pltpu.VMEM((rows, LANES), F32), pltpu.VMEM((rows, LANES), F32)],
        compiler_params=_cparams("parallel", "parallel"),
        name="hyena_longconv",
    )(z4, gate4, h3, ma, mb, mbi, mai)


def _rope_kernel(q_ref, k_ref, v_ref, cos_ref, sin_ref, qo_ref, kt_ref, vo_ref, *, half, scale):
    cos = cos_ref[...]
    sin = sin_ref[...]
    lane = lax.broadcasted_iota(jnp.int32, (1, LANES), 1)
    first_half = (lane % (2 * half)) < half

    def rot(x):
        outs = []
        for j in range(x.shape[1] // LANES):
            xb = x[:, j * LANES:(j + 1) * LANES]
            partner = jnp.where(first_half, pltpu.roll(xb, LANES - half, axis=1), pltpu.roll(xb, half, axis=1))
            outs.append(xb * cos + partner * sin)
        return jnp.concatenate(outs, axis=1)

    qo_ref[...] = (rot(q_ref[...]) * scale).astype(BF16)
    kt_ref[...] = rot(k_ref[...]).T.astype(BF16)
    vo_ref[...] = v_ref[...].astype(BF16)


def _rope(p2, bsz, seq, qk_width, v_width, head_dim, col_q, tm=512):
    t = p2.shape[0]
    assert qk_width == v_width and col_q % qk_width == 0
    jq = col_q // qk_width
    half = head_dim // 2
    inv = ROPE_THETA ** (-jnp.arange(half, dtype=F32) * 2.0 / head_dim)
    ang = jnp.arange(seq, dtype=F32)[:, None] * inv[None, :]
    cos, sin = jnp.cos(ang), jnp.sin(ang)
    reps = LANES // head_dim
    cos_t = jnp.tile(jnp.concatenate([cos, cos], axis=1), (1, reps))
    sin_t = jnp.tile(jnp.concatenate([-sin, sin], axis=1), (1, reps))
    ns = seq // tm
    return pl.pallas_call(
        functools.partial(_rope_kernel, half=half, scale=head_dim ** -0.5 * math.log2(math.e)),
        grid=(t // tm,),
        in_specs=[
            pl.BlockSpec((tm, qk_width), lambda i: (i, jq)),
            pl.BlockSpec((tm, qk_width), lambda i: (i, jq + 1)),
            pl.BlockSpec((tm, v_width), lambda i: (i, jq + 2)),
            pl.BlockSpec((tm, LANES), lambda i: (i % ns, 0)),
            pl.BlockSpec((tm, LANES), lambda i: (i % ns, 0)),
        ],
        out_specs=[
            pl.BlockSpec((tm, qk_width), lambda i: (i, 0)),
            pl.BlockSpec((None, qk_width, tm), lambda i: (i // ns, 0, i % ns)),
            pl.BlockSpec((tm, v_width), lambda i: (i, 0)),
        ],
        out_shape=[
            jax.ShapeDtypeStruct((t, qk_width), BF16),
            jax.ShapeDtypeStruct((bsz, qk_width, seq), BF16),
            jax.ShapeDtypeStruct((t, v_width), BF16),
        ],
        compiler_params=_cparams("parallel"),
        name="rope",
    )(p2, p2, p2, cos_t, sin_t)


def _attn_kernel(q_ref, kt_ref, v_ref, lq1_ref, lk1_ref, lq2_ref, lk2_ref, g_ref, o_ref, *, head_dim, lam_init):
    lam = (jnp.exp(jnp.sum(lq1_ref[...] * lk1_ref[...], axis=-1, keepdims=True))
           - jnp.exp(jnp.sum(lq2_ref[...] * lk2_ref[...], axis=-1, keepdims=True)) + lam_init)
    lane = lax.broadcasted_iota(jnp.int32, (1, q_ref.shape[1]), 1)
    sub = ATTN_SUB_ROWS
    nsub = q_ref.shape[0] // sub

    def scores(j):
        q = q_ref[pl.ds(j * sub, sub), :]
        zero = jnp.zeros_like(q)
        qq = jnp.concatenate([jnp.where(lane < head_dim, q, zero), jnp.where(lane >= head_dim, q, zero)], axis=0)
        return jnp.dot(qq, kt_ref[...], preferred_element_type=F32)

    v = v_ref[...]
    vd = v.shape[1]
    v_aug = jnp.concatenate([v, jnp.where(lane == 0, 1.0, 0.0).astype(BF16) + jnp.zeros_like(v)], axis=1)

    def weights(s):
        return (jnp.exp2(s - jnp.max(s, axis=-1, keepdims=True)).astype(BF16),)

    def emit(j, e):
        r = jnp.dot(e, v_aug, preferred_element_type=F32)
        o = (r[:sub, :vd] / r[:sub, vd:vd + 1]) - lam * (r[sub:, :vd] / r[sub:, vd:vd + 1])
        o = o * lax.rsqrt(jnp.mean(o * o, axis=-1, keepdims=True) + SUBLN_EPS) * g_ref[...]
        o_ref[pl.ds(j * sub, sub), :] = (o * (1.0 - lam_init)).astype(o_ref.dtype)

    s_of, a_of = {}, {}
    for t in range(nsub + 2):
        if t < nsub:
            s_of[t] = scores(t)
        if 0 <= t - 1 < nsub:
            a_of[t - 1] = weights(s_of.pop(t - 1))
        if 0 <= t - 2 < nsub:
            emit(t - 2, *a_of.pop(t - 2))


def _attention(q, kt, v, lq1, lk1, lq2, lk2, subln_g, head_dim, lam_init, tq=512):
    bsz, seq, width = q.shape
    v_dim = subln_g.shape[-1]
    assert v_dim == 2 * head_dim == LANES
    heads = width // v_dim
    vec = lambda a: pl.BlockSpec((1, a.shape[-1]), lambda b, h, i: (0, 0))
    lams = [a[None] for a in (lq1, lk1, lq2, lk2)]
    return pl.pallas_call(
        functools.partial(_attn_kernel, head_dim=head_dim, lam_init=lam_init),
        grid=(bsz, heads, seq // tq),
        in_specs=[
            pl.BlockSpec((None, tq, v_dim), lambda b, h, i: (b, i, h)),
            pl.BlockSpec((None, v_dim, seq), lambda b, h, i: (b, h, 0)),
            pl.BlockSpec((None, seq, v_dim), lambda b, h, i: (b, 0, h)),
            *[vec(a) for a in lams],
            vec(subln_g[None]),
        ],
        out_specs=pl.BlockSpec((None, tq, v_dim), lambda b, h, i: (b, i, h)),
        out_shape=jax.ShapeDtypeStruct((bsz, seq, width), BF16),
        compiler_params=_cparams("parallel", "parallel", "parallel"),
        name="diff_attention",
    )(q, kt, v, *lams, subln_g[None])


def _merge_kernel(x_ref, yh_ref, ya_ref, gh_ref, ga_ref, wuh_ref, wua_ref, wo_ref, g_ref, wr_ref, br_ref,
                  xo_ref, n_ref, aff_ref):
    m = x_ref.shape[0] // MERGE_ROW_SPLIT
    for j in range(MERGE_ROW_SPLIT):
        r = pl.ds(j * m, m)
        mh = jnp.dot(yh_ref[r, :].astype(BF16), wuh_ref[...], preferred_element_type=F32)
        ma = jnp.dot(ya_ref[r, :].astype(BF16), wua_ref[...], preferred_element_type=F32)
        merged = jax.nn.sigmoid(gh_ref[r, :]) * mh + jax.nn.sigmoid(ga_ref[r, :]) * ma
        x = x_ref[r, :] + jnp.dot(merged.astype(BF16), wo_ref[...], preferred_element_type=F32)
        xo_ref[r, :] = x
        n = x * lax.rsqrt(jnp.mean(x * x, axis=-1, keepdims=True) + NORM_EPS) * g_ref[...]
        n_ref[r, :] = n.astype(BF16)
        logits = lax.dot_general(wr_ref[...], n, (((1,), (1,)), ((), ())), preferred_element_type=F32,
                                 precision=lax.Precision.HIGHEST) + br_ref[...]
        e = jnp.exp(logits - jnp.max(logits, axis=0, keepdims=True))
        aff_ref[:, pl.ds(j * m, m)] = e / jnp.sum(e, axis=0, keepdims=True)


def _merge(x2, yh2, ya2, p2, col_gate, wuh, wua, wo, g, wr_t, br, bsz, seq, tm=512):
    t, d = x2.shape
    c = yh2.shape[1]
    e = wr_t.shape[0]
    jg = col_gate // d
    ns = seq // tm
    const = lambda a: pl.BlockSpec(a.shape, lambda i: (0,) * a.ndim)
    return pl.pallas_call(
        _merge_kernel,
        grid=(t // tm,),
        in_specs=[
            pl.BlockSpec((tm, d), lambda i: (i, 0)),
            pl.BlockSpec((tm, c), lambda i: (i, 0)),
            pl.BlockSpec((tm, ya2.shape[1]), lambda i: (i, 0)),
            pl.BlockSpec((tm, d), lambda i: (i, jg)),
            pl.BlockSpec((tm, d), lambda i: (i, jg + 1)),
            const(wuh), const(wua), const(wo), const(g), const(wr_t), const(br),
        ],
        out_specs=[
            pl.BlockSpec((tm, d), lambda i: (i, 0)),
            pl.BlockSpec((tm, d), lambda i: (i, 0)),
            pl.BlockSpec((None, e, tm), lambda i: (i // ns, 0, i % ns)),
        ],
        out_shape=[
            jax.ShapeDtypeStruct((t, d), F32),
            jax.ShapeDtypeStruct((t, d), BF16),
            jax.ShapeDtypeStruct((bsz, e, seq), F32),
        ],
        compiler_params=_cparams("parallel"),
        name="merge_router",
    )(x2, yh2, ya2, p2, p2, wuh, wua, wo, g, wr_t, br)


def _select_kernel(aff_ref, pos_ref, *, cap):
    a = aff_ref[...]
    rows, seq = a.shape
    as_f32 = lambda b: lax.bitcast_convert_type(b, F32)
    count = lambda m: jnp.sum(jnp.where(m, 1.0, 0.0), axis=-1, keepdims=True)
    thr = jnp.zeros((rows, 1), jnp.int32)
    for bit in range(30, -1, -1):
        cand = thr | (1 << bit)
        thr = jnp.where(count(a >= as_f32(cand)) >= cap, cand, thr)
    gt = a >= as_f32(thr + 1)
    eq = (a >= as_f32(thr)) & jnp.logical_not(gt)
    need = cap - count(gt)
    tri = jnp.where(lax.broadcasted_iota(jnp.int32, (LANES, LANES), 0)
                    <= lax.broadcasted_iota(jnp.int32, (LANES, LANES), 1), 1.0, 0.0).astype(BF16)

    def exclusive_cumsum(mask):
        ones = jnp.where(mask, 1.0, 0.0)
        carry = jnp.zeros((rows, 1), F32)
        chunks = []
        for j in range(seq // LANES):
            blk = ones[:, j * LANES:(j + 1) * LANES]
            incl = jnp.dot(blk.astype(BF16), tri, preferred_element_type=F32)
            chunks.append(incl - blk + carry)
            carry = carry + jnp.sum(blk, axis=-1, keepdims=True)
        return jnp.concatenate(chunks, axis=1)

    sel = gt | (eq & (exclusive_cumsum(eq) < need))
    pos_ref[...] = jnp.where(sel, exclusive_cumsum(sel), -1.0).astype(jnp.int32)


def _select(aff_rows, cap):
    return pl.pallas_call(
        functools.partial(_select_kernel, cap=cap),
        out_shape=jax.ShapeDtypeStruct(aff_rows.shape, jnp.int32),
        compiler_params=pltpu.CompilerParams(vmem_limit_bytes=VMEM_LIMIT_V7X),
        name="expert_select",
    )(aff_rows)


def _gather_kernel(starts_ref, pos_ref, aff_ref, n_ref, o_ref, gate_ref, *, win):
    b = pl.program_id(0)
    i = pl.program_id(1)
    e, cap, _ = o_ref.shape
    ts = n_ref.shape[0]

    @pl.when(i == 0)
    def _():
        o_ref[...] = jnp.zeros_like(o_ref)
        gate_ref[...] = jnp.zeros_like(gate_ref)

    slot0 = lax.broadcasted_iota(jnp.int32, (win, ts), 0)
    pos = pos_ref[...]
    aff = aff_ref[...]

    def first_row(x):
        lo = starts_ref[b, x, i]
        return pl.multiple_of(jnp.minimum((lo // BF16_ROWS) * BF16_ROWS, cap - win), BF16_ROWS)

    def add_gates(x, dst, match):
        picked = jnp.sum(jnp.where(match, aff[x:x + 1, :], 0.0), axis=-1, keepdims=True)
        gate_ref[x, dst, :] = gate_ref[x, dst, :] + picked

    for x0 in range(0, e, GATHER_GROUP):
        group = range(x0, min(x0 + GATHER_GROUP, e))
        onehots = []
        for x in group:
            base = first_row(x)
            match = (slot0 + base) == pos[x:x + 1, :]
            add_gates(x, pl.ds(base, win), match)
            onehots.append(match.astype(BF16))
        picked = jnp.dot(jnp.concatenate(onehots, axis=0), n_ref[...], preferred_element_type=F32)
        for k, x in enumerate(group):
            dst = pl.ds(first_row(x), win)
            o_ref[x, dst, :] = o_ref[x, dst, :] + picked[k * win:(k + 1) * win].astype(BF16)

    for x in range(e):
        base = first_row(x)
        hi = starts_ref[b, x, i + 1]

        def extra(k, carry, base=base, x=x):
            want = base + k * win
            row = pl.multiple_of(jnp.minimum(want, cap - win), BF16_ROWS)
            slots = slot0 + row
            match = jnp.logical_and(slots == pos[x:x + 1, :], slots >= want)
            dst = pl.ds(row, win)
            add_gates(x, dst, match)
            o_ref[x, dst, :] = o_ref[x, dst, :] + jnp.dot(match.astype(BF16), n_ref[...],
                                                          preferred_element_type=F32).astype(BF16)
            return carry

        windows = (jnp.maximum(hi - base, 1) + win - 1) // win
        lax.fori_loop(1, windows, extra, 0)


def _gather(pos, aff, starts, n3, cap, ts=COMBINE_TILE, win=128):
    bsz, e, seq = pos.shape
    d = n3.shape[-1]
    assert cap % BF16_ROWS == 0 and win % BF16_ROWS == 0 and win <= cap
    tile = pl.BlockSpec((None, e, ts), lambda b, i, st: (b, 0, i))
    grid_spec = pltpu.PrefetchScalarGridSpec(
        num_scalar_prefetch=1,
        grid=(bsz, seq // ts),
        in_specs=[tile, tile, pl.BlockSpec((None, ts, d), lambda b, i, st: (b, i, 0))],
        out_specs=[
            pl.BlockSpec((None, e, cap, d), lambda b, i, st: (b, 0, 0, 0)),
            pl.BlockSpec((None, e, cap, 1), lambda b, i, st: (b, 0, 0, 0)),
        ],
    )
    return pl.pallas_call(
        functools.partial(_gather_kernel, win=win),
        grid_spec=grid_spec,
        out_shape=[
            jax.ShapeDtypeStruct((bsz, e, cap, d), BF16),
            jax.ShapeDtypeStruct((bsz, e, cap, 1), F32),
        ],
        compiler_params=_cparams("parallel", "arbitrary"),
        name="expert_gather",
    )(starts, pos, aff, n3)


def _expert_kernel(x_ref, gate_ref, wg_ref, wu_ref, wd_ref, o_ref, acc_sc):
    s = pl.program_id(1)
    bg = pl.program_id(2)
    rows, cap, d = x_ref.shape

    @pl.when(s == 0)
    def _():
        acc_sc[bg] = jnp.zeros((rows * cap, d), F32)

    x = x_ref[...].reshape(rows * cap, d)
    wg = wg_ref[...].astype(BF16)
    wu = wu_ref[...].astype(BF16)
    wd = wd_ref[...].astype(BF16)
    m = rows * cap // EXPERT_ROW_SPLIT
    parts = []
    for j in range(EXPERT_ROW_SPLIT):
        xj = x[j * m:(j + 1) * m]
        parts.append((jnp.dot(xj, wg, preferred_element_type=F32), jnp.dot(xj, wu, preferred_element_type=F32)))
    ys = []
    for g, u in parts:
        h = (g * jax.nn.sigmoid(g) * u).astype(BF16)
        ys.append(jnp.dot(h, wd, preferred_element_type=F32))
    total = acc_sc[bg] + jnp.concatenate(ys, axis=0)
    acc_sc[bg] = total
    o_ref[...] = (total * gate_ref[...].reshape(rows * cap, 1)).astype(BF16).reshape(rows, cap, d)


def _experts(xg, gate, wg4, wu4, wd4, layer, f_slices=2, rows=1):
    bsz, e, cap, d = xg.shape
    f = wg4.shape[-1]
    assert f_slices >= 2 and f % f_slices == 0 and bsz % rows == 0
    fs = f // f_slices
    last = f_slices - 1
    groups = bsz // rows
    out_idx = lambda x, s, bg: (jnp.where(s == last, bg, 0), x, 0, 0)
    return pl.pallas_call(
        _expert_kernel,
        grid=(e, f_slices, groups),
        in_specs=[
            pl.BlockSpec((rows, None, cap, d), lambda x, s, bg: (bg, x, 0, 0)),
            pl.BlockSpec((rows, None, cap, 1), lambda x, s, bg: (bg, x, 0, 0)),
            pl.BlockSpec((None, None, d, fs), lambda x, s, bg: (layer, x, 0, s)),
            pl.BlockSpec((None, None, d, fs), lambda x, s, bg: (layer, x, 0, s)),
            pl.BlockSpec((None, None, fs, d), lambda x, s, bg: (layer, x, s, 0)),
        ],
        out_specs=pl.BlockSpec((rows, None, cap, d), out_idx),
        out_shape=jax.ShapeDtypeStruct((bsz, e, cap, d), BF16),
        scratch_shapes=[pltpu.VMEM((groups, rows * cap, d), F32)],
        compiler_params=_cparams("arbitrary", "arbitrary", "arbitrary"),
        name="expert_ffn",
    )(xg, gate, wg4, wu4, wd4)


def _combine_kernel(starts_ref, x_ref, pos_ref, ye_ref, g_ref, o_ref, stage_sc, *, final, win):
    b = pl.program_id(0)
    i = pl.program_id(1)
    ts = x_ref.shape[0]
    e, cap, _ = ye_ref.shape
    lane = lax.broadcasted_iota(jnp.int32, (ts, win), 1)
    pos = pos_ref[...]

    def first_row(x):
        lo = starts_ref[b, x, i]
        return pl.multiple_of(jnp.minimum((lo // BF16_ROWS) * BF16_ROWS, cap - win), BF16_ROWS)

    onehots = []
    for x in range(e):
        base = first_row(x)
        stage_sc[pl.ds(x * win, win), :] = ye_ref[x, pl.ds(base, win), :]
        onehots.append((pos[:, x:x + 1] - base == lane).astype(BF16))
    o_ref[...] = x_ref[...] + jnp.dot(jnp.concatenate(onehots, axis=1), stage_sc[...],
                                      preferred_element_type=F32)

    for x in range(e):
        base = first_row(x)
        hi = starts_ref[b, x, i + 1]
        col = pos[:, x:x + 1]

        def extra(k, carry, base=base, col=col, x=x):
            want = base + k * win
            row = pl.multiple_of(jnp.minimum(want, cap - win), BF16_ROWS)
            onehot = jnp.logical_and(col - row == lane, col >= want).astype(BF16)
            o_ref[...] += jnp.dot(onehot, ye_ref[x, pl.ds(row, win), :], preferred_element_type=F32)
            return carry

        windows = (jnp.maximum(hi - base, 1) + win - 1) // win
        lax.fori_loop(1, windows, extra, 0)

    if final:
        acc = o_ref[...]
        o_ref[...] = acc * lax.rsqrt(jnp.mean(acc * acc, axis=-1, keepdims=True) + NORM_EPS) * g_ref[...]


def _combine(x3, pos_t, starts, ye, g, final, ts=512, win=128):
    bsz, seq, d = x3.shape
    e, cap = ye.shape[1], ye.shape[2]
    assert cap % BF16_ROWS == 0 and win % BF16_ROWS == 0 and win <= cap
    grid_spec = pltpu.PrefetchScalarGridSpec(
        num_scalar_prefetch=1,
        grid=(bsz, seq // ts),
        in_specs=[
            pl.BlockSpec((None, ts, d), lambda b, i, st: (b, i, 0)),
            pl.BlockSpec((None, ts, e), lambda b, i, st: (b, i, 0)),
            _single((None, e, cap, d), lambda b, i, st: (b, 0, 0, 0)),
            pl.BlockSpec((1, d), lambda b, i, st: (0, 0)),
        ],
        out_specs=pl.BlockSpec((None, ts, d), lambda b, i, st: (b, i, 0)),
        scratch_shapes=[pltpu.VMEM((e * win, d), BF16)],
    )
    return pl.pallas_call(
        functools.partial(_combine_kernel, final=final, win=win),
        grid_spec=grid_spec,
        out_shape=jax.ShapeDtypeStruct((bsz, seq, d), F32),
        compiler_params=_cparams("parallel", "parallel"),
        name="expert_combine",
    )(starts, x3, pos_t, ye, g)


def kernel(x, norm_mix, w_in, b_in, hy_conv_w, hy_conv_b, hy_ffn_w1, hy_ffn_b1, hy_ffn_f1, hy_ffn_w2, hy_ffn_b2, hy_ffn_f2, hy_ffn_w3, hy_bias, lambda_q1, lambda_k1, lambda_q2, lambda_k2, subln_g, w_up_hyena, w_up_attn, w_out, norm_ffn, w_router, b_router, w_e_gate, w_e_up, w_e_down, norm_final):
    bsz, seq, d = x.shape
    depth = w_in.shape[0]
    orders, c = hy_bias.shape[1], hy_bias.shape[2]
    head_dim = lambda_q1.shape[1]
    v_width = w_up_attn.shape[1]
    qk_width = v_width
    e = w_router.shape[2]
    cap = EC_FACTOR * seq // e
    col_q = (orders + 1) * c
    col_gate = col_q + 2 * qk_width + v_width
    assert orders == 2 and bsz % 2 == 0 and col_gate % d == 0

    consts = tuple(jnp.asarray(a, F32).astype(BF16) for a in _dft_constants(seq))

    xs = x.reshape(bsz * seq, d)
    out = None
    for l in range(depth):
        p2 = _inproj(xs, norm_mix[l][None], w_in[l].astype(BF16), b_in[l][None])
        p3 = p2.reshape(bsz, seq, -1)

        uc = _shortconv(p3, hy_conv_w[l], hy_conv_b[l][None], c)
        hfull = _filters(hy_ffn_w1[l], hy_ffn_b1[l], hy_ffn_f1[l], hy_ffn_w2[l], hy_ffn_b2[l], hy_ffn_f2[l],
                        hy_ffn_w3[l], hy_bias[l], seq)
        hspec = _spectrum(hfull, consts[4], consts[1])
        z = _longconv(uc, 0, uc, 1, hspec, 0, consts, natural_out=False)
        y_hy = _longconv(z, 0, uc, 2, hspec, 1, consts, natural_out=True)

        q_r, k_t, v_b = _rope(p2, bsz, seq, qk_width, v_width, head_dim, col_q)
        lam_init = 0.8 - 0.6 * math.exp(-0.3 * l)
        y_da = _attention(q_r.reshape(bsz, seq, qk_width), k_t, v_b.reshape(bsz, seq, v_width),
                          lambda_q1[l], lambda_k1[l], lambda_q2[l], lambda_k2[l], subln_g[l], head_dim, lam_init)

        xs, n2, aff = _merge(xs, y_hy.reshape(bsz * seq, c), y_da.reshape(bsz * seq, v_width), p2, col_gate,
                             w_up_hyena[l].astype(BF16), w_up_attn[l].astype(BF16), w_out[l].astype(BF16),
                             norm_ffn[l][None], w_router[l].T, b_router[l][:, None], bsz, seq)

        pos = _select(aff.reshape(bsz * e, seq), cap).reshape(bsz, e, seq)
        tiles = seq // COMBINE_TILE
        counts = jnp.sum((pos >= 0).reshape(bsz, e, tiles, COMBINE_TILE), axis=-1, dtype=jnp.int32)
        starts = jnp.concatenate([jnp.zeros((bsz, e, 1), jnp.int32), jnp.cumsum(counts, axis=-1)], axis=-1)
        xg, gate = _gather(pos, aff, starts, n2.reshape(bsz, seq, d), cap)
        ye = _experts(xg, gate, w_e_gate, w_e_up, w_e_down, l)
        final = l == depth - 1
        out = _combine(xs.reshape(bsz, seq, d), pos.transpose(0, 2, 1), starts, ye, norm_final[None], final,
                       ts=COMBINE_TILE)
        xs = out.reshape(bsz * seq, d)
    return out
```

```python
import functools
import math

import numpy as np
import jax
import jax.numpy as jnp
from jax import lax
from jax.experimental import pallas as pl
from jax.experimental.pallas import tpu as pltpu

F32 = jnp.float32
BF16 = jnp.bfloat16

NORM_EPS = 1e-6
SUBLN_EPS = 1e-5
ROPE_THETA = 10000.0
HY_FAST_DECAY = 0.3
HY_SLOW_DECAY = 1.5
HY_TARGET = 1e-2
EC_FACTOR = 2

VMEM_LIMIT_V7X = 56 * 1024 * 1024
LANES = 128
BF16_ROWS = 16
COMBINE_TILE = 512
GATHER_GROUP = 16
MERGE_ROW_SPLIT = 1
EXPERT_ROW_SPLIT = 1
ATTN_SUB_ROWS = 128

DFT_N1 = 64
DFT_N2 = 128
SUBLANES = 8
A_PITCH = 2 * DFT_N1 + SUBLANES
V_PITCH = 2 * DFT_N2 + SUBLANES
H_PITCH = DFT_N2 + SUBLANES
UNROLL_STAGE_A = 64
UNROLL_STAGE_B = 16
STAGE_B_COLS = 2
UNROLL_SHORTCONV = 8


def _cparams(*sem):
    return pltpu.CompilerParams(dimension_semantics=sem, vmem_limit_bytes=VMEM_LIMIT_V7X)


def _single(block_shape, index_map):
    return pl.BlockSpec(block_shape, index_map, pipeline_mode=pl.Buffered(1))


def _inproj_kernel(x_ref, g_ref, w_ref, b_ref, o_ref, n_sc):
    @pl.when(pl.program_id(1) == 0)
    def _():
        x = x_ref[...]
        n = x * lax.rsqrt(jnp.mean(x * x, axis=-1, keepdims=True) + NORM_EPS) * g_ref[...]
        n_sc[...] = n.astype(BF16)

    o_ref[...] = jnp.dot(n_sc[...], w_ref[...], preferred_element_type=F32) + b_ref[...]


def _inproj(x2, g, w_bf, b, tm=2048, tn=1024):
    t, d = x2.shape
    width = w_bf.shape[1]
    return pl.pallas_call(
        _inproj_kernel,
        grid=(t // tm, width // tn),
        in_specs=[
            pl.BlockSpec((tm, d), lambda i, j: (i, 0)),
            pl.BlockSpec((1, d), lambda i, j: (0, 0)),
            pl.BlockSpec((d, tn), lambda i, j: (0, j)),
            pl.BlockSpec((1, tn), lambda i, j: (0, j)),
        ],
        out_specs=pl.BlockSpec((tm, tn), lambda i, j: (i, j)),
        out_shape=jax.ShapeDtypeStruct((t, width), F32),
        scratch_shapes=[pltpu.VMEM((tm, d), BF16)],
        compiler_params=_cparams("parallel", "arbitrary"),
        name="inproj",
    )(x2, g, w_bf, b)


def _shortconv_kernel(u_ref, w_ref, b_ref, o_ref, pad_sc, *, seq, n2, nh):
    zero_row = jnp.zeros((1, LANES), F32)
    for q in range(nh):
        base = q * H_PITCH + SUBLANES
        pad_sc[pl.ds(base, n2), :] = u_ref[pl.ds(q * n2, n2), :]
        pad_sc[pl.ds(base - 1, 1), :] = u_ref[pl.ds(q * n2 - 1, 1), :] if q > 0 else zero_row
        pad_sc[pl.ds(base + n2, 1), :] = u_ref[pl.ds((q + 1) * n2, 1), :] if q < nh - 1 else zero_row
    w = w_ref[...]
    bias = b_ref[...]

    column = lambda t: pad_sc[pl.ds(SUBLANES - 1 + t, nh, stride=H_PITCH), :]

    def body(r, taps):
        prev, cur = taps
        nxt = column(r + 2)
        o_ref[pl.ds(pl.multiple_of(r * nh, nh), nh), :] = prev * w[0:1] + cur * w[1:2] + nxt * w[2:3] + bias
        return cur, nxt

    lax.fori_loop(0, n2, body, (column(0), column(1)), unroll=UNROLL_SHORTCONV)


def _shortconv(p3, conv_w, conv_b, c):
    bsz, seq, _ = p3.shape
    parts = conv_w.shape[1] // c
    cb_per_part = c // LANES
    nh = seq // DFT_N2
    return pl.pallas_call(
        functools.partial(_shortconv_kernel, seq=seq, n2=DFT_N2, nh=nh),
        grid=(bsz, parts * cb_per_part),
        in_specs=[
            pl.BlockSpec((None, seq, LANES), lambda b, j: (b, 0, j)),
            pl.BlockSpec((3, LANES), lambda b, j: (0, j)),
            pl.BlockSpec((1, LANES), lambda b, j: (0, j)),
        ],
        out_specs=pl.BlockSpec((None, None, seq, LANES), lambda b, j: (j // cb_per_part, b, 0, j % cb_per_part)),
        out_shape=jax.ShapeDtypeStruct((parts, bsz, seq, c), F32),
        scratch_shapes=[pltpu.VMEM((nh * H_PITCH + SUBLANES, LANES), F32)],
        compiler_params=_cparams("parallel", "parallel"),
        name="shortconv",
    )(p3, conv_w, conv_b)


def _filter_kernel(w1t_ref, w1c_ref, w1s_ref, b1_ref, f1_ref, w2_ref, b2_ref, f2_ref, w3_ref, bias_ref, o_ref,
                   *, seq, tl, c, bands):
    hi = lax.Precision.HIGHEST
    lag = lambda idx: jnp.where(idx < seq, idx, 2 * seq - idx).astype(F32)
    row = lax.broadcasted_iota(jnp.int32, (tl, 1), 0) + pl.program_id(0) * tl
    t = lag(row) / (seq - 1.0)
    pos = lag(lax.broadcasted_iota(jnp.int32, (1, tl), 1) + pl.program_id(0) * tl)
    t_l = pos / (seq - 1.0)
    w_l = (2.0 * math.pi) * pos / float(seq)
    band = lax.broadcasted_iota(jnp.int32, (bands, 1), 0).astype(F32)
    fr = 1e-4 + band * ((bands - 1 - 1e-4) / (bands - 1))
    ang = fr * w_l
    pre = (w1t_ref[...] * t_l
           + jnp.dot(w1c_ref[...], jnp.cos(ang), preferred_element_type=F32, precision=hi)
           - jnp.dot(w1s_ref[...], jnp.sin(ang), preferred_element_type=F32, precision=hi)
           + b1_ref[...])
    h = jnp.sin(f1_ref[...] * pre)
    h = jnp.sin(f2_ref[...] * (jnp.dot(w2_ref[...], h, preferred_element_type=F32, precision=hi) + b2_ref[...]))
    h = jnp.dot(h.T, w3_ref[...], preferred_element_type=F32, precision=hi)
    min_decay = math.log(HY_TARGET) / HY_FAST_DECAY
    max_decay = math.log(HY_TARGET) / HY_SLOW_DECAY
    ch = lax.broadcasted_iota(jnp.int32, (1, c), 1).astype(F32)
    deltas = jnp.abs(min_decay + ch * ((max_decay - min_decay) / (c - 1)))
    decay = jnp.exp(-t * deltas)
    orders = o_ref.shape[0]
    for o in range(orders):
        taps = h[:, o * c:(o + 1) * c] * decay
        taps = jnp.where(row == 0, taps + bias_ref[o:o + 1, :], taps)
        taps = jnp.where(row == seq, 0.0, taps)
        for blk in range(tl // DFT_N2):
            o_ref[o, pl.ds(blk * H_PITCH, DFT_N2), :] = taps[blk * DFT_N2:(blk + 1) * DFT_N2]
            o_ref[o, pl.ds(blk * H_PITCH + DFT_N2, H_PITCH - DFT_N2), :] = jnp.zeros((H_PITCH - DFT_N2, c), F32)


def _filters(w1, b1, f1, w2, b2, f2, w3, bias, seq, tl=512):
    emb, ffn = w1.shape
    bands = (emb - 1) // 2
    orders, c = bias.shape
    tiles = seq // tl
    w3_dir = w3.reshape(ffn, orders, 2, c).transpose(2, 0, 1, 3).reshape(2, ffn, orders * c)
    full = lambda a: pl.BlockSpec(a.shape, lambda i: (0,) * a.ndim)
    col = lambda a: a[:, None]
    args = (w1[0:1].T, w1[1:1 + bands].T, w1[1 + bands:].T, col(b1), col(f1), w2.T, col(b2), col(f2), w3_dir, bias)
    in_specs = [full(a) for a in args]
    in_specs[8] = pl.BlockSpec((None, ffn, orders * c), lambda i: (i // tiles, 0, 0))
    return pl.pallas_call(
        functools.partial(_filter_kernel, seq=seq, tl=tl, c=c, bands=bands),
        grid=(2 * tiles,),
        in_specs=in_specs,
        out_specs=pl.BlockSpec((orders, tl // DFT_N2 * H_PITCH, c), lambda i: (0, i, 0)),
        out_shape=jax.ShapeDtypeStruct((orders, 2 * seq // DFT_N2 * H_PITCH, c), F32),
        compiler_params=_cparams("parallel"),
        name="hyena_filters",
    )(*args)


def _stack_complex(m):
    return np.block([[m.real, -m.imag], [m.imag, m.real]])


@functools.lru_cache(maxsize=None)
def _dft_constants(seq):
    n1, n2 = DFT_N1, DFT_N2
    n = n1 * n2
    assert n == 2 * seq
    nh = seq // n2
    k1 = np.arange(n1)[:, None].astype(np.float64)
    q = np.arange(nh)[None, :].astype(np.float64)
    qf = np.arange(n1)[None, :].astype(np.float64)
    ma = np.empty((n2, 2 * n1, 2 * nh), np.float64)
    mai = np.empty((n2, 2 * nh, 2 * n1), np.float64)
    maf = np.empty((n2, 2 * n1, n1), np.float64)
    for r in range(n2):
        e = np.exp(-2j * np.pi * (q * k1 / n1 + r * k1 / n))
        ma[r] = _stack_complex(e)
        mai[r] = _stack_complex(np.conj(e).T / n)
        ef = np.exp(-2j * np.pi * (qf * k1 / n1 + r * k1 / n))
        maf[r] = np.concatenate([ef.real, ef.imag], axis=0)
    kk = np.arange(n2)[:, None].astype(np.float64)
    rr = np.arange(n2)[None, :].astype(np.float64)
    f = np.exp(-2j * np.pi * kk * rr / n2)
    mb = _stack_complex(f)
    mbi = _stack_complex(np.conj(f).T)
    return tuple(np.asarray(a, np.float32) for a in (ma, mb, mbi, mai, maf))


def _stage_a(load_x, ma_ref, a_sc):
    def body(r, carry):
        a_sc[pl.ds(pl.multiple_of(r * A_PITCH, SUBLANES), 2 * DFT_N1), :] = jnp.dot(
            ma_ref[r], load_x(r), preferred_element_type=F32)
        return carry

    lax.fori_loop(0, DFT_N2, body, 0, unroll=UNROLL_STAGE_A)


def _load_a_columns(a_sc, k1):
    cols = []
    for j in range(STAGE_B_COLS):
        ar = a_sc[pl.ds(k1 + j, DFT_N2, stride=A_PITCH), :]
        ai = a_sc[pl.ds(DFT_N1 + k1 + j, DFT_N2, stride=A_PITCH), :]
        cols.append(jnp.concatenate([ar, ai], axis=0))
    return jnp.concatenate(cols, axis=1).astype(BF16)


def _spectrum_kernel(h_ref, maf_ref, mb_ref, o_ref, a_sc):
    rows = 2 * DFT_N2

    def load_x(r):
        return h_ref[pl.ds(r, DFT_N1, stride=H_PITCH), :].astype(BF16)

    _stage_a(load_x, maf_ref, a_sc)

    cb = o_ref.shape[-1]

    def body(kp, carry):
        k1 = kp * STAGE_B_COLS
        z = jnp.dot(mb_ref[...], _load_a_columns(a_sc, k1), preferred_element_type=F32)
        for j in range(STAGE_B_COLS):
            o_ref[pl.ds(pl.multiple_of((k1 + j) * rows, rows), rows), :] = z[:, j * cb:(j + 1) * cb]
        return carry

    lax.fori_loop(0, DFT_N1 // STAGE_B_COLS, body, 0, unroll=UNROLL_STAGE_B)


def _spectrum(hfull, maf, mb):
    orders, n, c = hfull.shape
    rows = 2 * DFT_N1 * DFT_N2
    return pl.pallas_call(
        _spectrum_kernel,
        grid=(orders, c // LANES),
        in_specs=[
            pl.BlockSpec((None, n, LANES), lambda o, j: (o, 0, j)),
            _single(maf.shape, lambda o, j: (0, 0, 0)),
            _single(mb.shape, lambda o, j: (0, 0)),
        ],
        out_specs=pl.BlockSpec((None, rows, LANES), lambda o, j: (o, 0, j)),
        out_shape=jax.ShapeDtypeStruct((orders, rows, c), F32),
        scratch_shapes=[pltpu.VMEM((DFT_N2 * A_PITCH, LANES), F32)],
        compiler_params=_cparams("parallel", "parallel"),
        name="hyena_spectrum",
    )(hfull, maf, mb)


def _longconv_kernel(z_ref, gate_ref, h_ref, ma_ref, mb_ref, mbi_ref, mai_ref, o_ref, a_sc, v_sc,
                     *, nh, natural_out):
    def load_x(r):
        src = pl.ds(pl.multiple_of(r * nh, nh), nh)
        return jnp.concatenate([z_ref[0, src, :], z_ref[1, src, :]], axis=0).astype(BF16)

    _stage_a(load_x, ma_ref, a_sc)

    rows = 2 * DFT_N2

    cb = o_ref.shape[-1]

    def freq(kp, carry):
        k1 = kp * STAGE_B_COLS
        zf = jnp.dot(mb_ref[...], _load_a_columns(a_sc, k1), preferred_element_type=F32)
        ys = []
        for j in range(STAGE_B_COLS):
            base = pl.multiple_of((k1 + j) * rows, rows)
            hr = h_ref[pl.ds(base, DFT_N2), :]
            hi = h_ref[pl.ds(base + DFT_N2, DFT_N2), :]
            zr = zf[:DFT_N2, j * cb:(j + 1) * cb]
            zi = zf[DFT_N2:, j * cb:(j + 1) * cb]
            ys.append(jnp.concatenate([zr * hr - zi * hi, zr * hi + zi * hr], axis=0))
        v = jnp.dot(mbi_ref[...], jnp.concatenate(ys, axis=1).astype(BF16), preferred_element_type=F32)
        for j in range(STAGE_B_COLS):
            v_sc[pl.ds(pl.multiple_of((k1 + j) * V_PITCH, SUBLANES), rows), :] = v[:, j * cb:(j + 1) * cb]
        return carry

    lax.fori_loop(0, DFT_N1 // STAGE_B_COLS, freq, 0, unroll=UNROLL_STAGE_B)

    def back(r, carry):
        vr = v_sc[pl.ds(r, DFT_N1, stride=V_PITCH), :]
        vi = v_sc[pl.ds(DFT_N2 + r, DFT_N1, stride=V_PITCH), :]
        y = jnp.dot(mai_ref[r], jnp.concatenate([vr, vi], axis=0).astype(BF16), preferred_element_type=F32)
        src = pl.ds(pl.multiple_of(r * nh, nh), nh)
        dst = pl.ds(r, nh, stride=DFT_N2) if natural_out else src
        o_ref[0, dst, :] = y[:nh] * gate_ref[0, src, :]
        o_ref[1, dst, :] = y[nh:] * gate_ref[1, src, :]
        return carry

    lax.fori_loop(0, DFT_N2, back, 0, unroll=UNROLL_STAGE_A)


def _longconv(z4, z_part, gate4, gate_part, h3, order, consts, natural_out):
    _, bsz, seq, c = z4.shape
    ma, mb, mbi, mai = consts[:4]
    nh = seq // DFT_N2
    rows = 2 * DFT_N1 * DFT_N2
    pair = lambda part: (lambda j, p: (part, p, 0, j))
    return pl.pallas_call(
        functools.partial(_longconv_kernel, nh=nh, natural_out=natural_out),
        grid=(c // LANES, bsz // 2),
        in_specs=[
            _single((None, 2, seq, LANES), pair(z_part)),
            _single((None, 2, seq, LANES), pair(gate_part)),
            _single((None, rows, LANES), lambda j, p: (order, 0, j)),
            _single(ma.shape, lambda j, p: (0, 0, 0)),
            _single(mb.shape, lambda j, p: (0, 0)),
            _single(mbi.shape, lambda j, p: (0, 0)),
            _single(mai.shape, lambda j, p: (0, 0, 0)),
        ],
        out_specs=pl.BlockSpec((None, 2, seq, LANES), pair(0)),
        out_shape=jax.ShapeDtypeStruct((1, bsz, seq, c), F32),
        scratch_shapes=[pltpu.VMEM((DFT_N2 * A_PITCH, LANES), F32), pltpu.VMEM((DFT_N1 * V_PITCH, LANES), F32)],
        compiler_params=_cparams("parallel", "parallel"),
        name="hyena_longconv",
    )(z4, gate4, h3, ma, mb, mbi, mai)


def _rope_kernel(q_ref, k_ref, v_ref, cos_ref, sin_ref, qo_ref, kt_ref, vo_ref, *, half, scale):
    cos = cos_ref[...]
    sin = sin_ref[...]
    lane = lax.broadcasted_iota(jnp.int32, (1, LANES), 1)
    first_half = (lane % (2 * half)) < half

    def rot(x):
        outs = []
        for j in range(x.shape[1] // LANES):
            xb = x[:, j * LANES:(j + 1) * LANES]
            partner = jnp.where(first_half, pltpu.roll(xb, LANES - half, axis=1), pltpu.roll(xb, half, axis=1))
            outs.append(xb * cos + partner * sin)
        return jnp.concatenate(outs, axis=1)

    qo_ref[...] = (rot(q_ref[...]) * scale).astype(BF16)
    kt_ref[...] = rot(k_ref[...]).T.astype(BF16)
    vo_ref[...] = v_ref[...].astype(BF16)


def _rope(p2, bsz, seq, qk_width, v_width, head_dim, col_q, tm=512):
    t = p2.shape[0]
    assert qk_width == v_width and col_q % qk_width == 0
    jq = col_q // qk_width
    half = head_dim // 2
    inv = ROPE_THETA ** (-jnp.arange(half, dtype=F32) * 2.0 / head_dim)
    ang = jnp.arange(seq, dtype=F32)[:, None] * inv[None, :]
    cos, sin = jnp.cos(ang), jnp.sin(ang)
    reps = LANES // head_dim
    cos_t = jnp.tile(jnp.concatenate([cos, cos], axis=1), (1, reps))
    sin_t = jnp.tile(jnp.concatenate([-sin, sin], axis=1), (1, reps))
    ns = seq // tm
    return pl.pallas_call(
        functools.partial(_rope_kernel, half=half, scale=head_dim ** -0.5 * math.log2(math.e)),
        grid=(t // tm,),
        in_specs=[
            pl.BlockSpec((tm, qk_width), lambda i: (i, jq)),
            pl.BlockSpec((tm, qk_width), lambda i: (i, jq + 1)),
            pl.BlockSpec((tm, v_width), lambda i: (i, jq + 2)),
            pl.BlockSpec((tm, LANES), lambda i: (i % ns, 0)),
            pl.BlockSpec((tm, LANES), lambda i: (i % ns, 0)),
        ],
        out_specs=[
            pl.BlockSpec((tm, qk_width), lambda i: (i, 0)),
            pl.BlockSpec((None, qk_width, tm), lambda i: (i // ns, 0, i % ns)),
            pl.BlockSpec((tm, v_width), lambda i: (i, 0)),
        ],
        out_shape=[
            jax.ShapeDtypeStruct((t, qk_width), BF16),
            jax.ShapeDtypeStruct((bsz, qk_width, seq), BF16),
            jax.ShapeDtypeStruct((t, v_width), BF16),
        ],
        compiler_params=_cparams("parallel"),
        name="rope",
    )(p2, p2, p2, cos_t, sin_t)


def _attn_kernel(q_ref, kt_ref, v_ref, lq1_ref, lk1_ref, lq2_ref, lk2_ref, g_ref, o_ref, *, head_dim, lam_init):
    lam = (jnp.exp(jnp.sum(lq1_ref[...] * lk1_ref[...], axis=-1, keepdims=True))
           - jnp.exp(jnp.sum(lq2_ref[...] * lk2_ref[...], axis=-1, keepdims=True)) + lam_init)
    lane = lax.broadcasted_iota(jnp.int32, (1, q_ref.shape[1]), 1)
    sub = ATTN_SUB_ROWS
    nsub = q_ref.shape[0] // sub

    def scores(j):
        q = q_ref[pl.ds(j * sub, sub), :]
        zero = jnp.zeros_like(q)
        qq = jnp.concatenate([jnp.where(lane < head_dim, q, zero), jnp.where(lane >= head_dim, q, zero)], axis=0)
        return jnp.dot(qq, kt_ref[...], preferred_element_type=F32)

    v = v_ref[...]
    vd = v.shape[1]
    v_aug = jnp.concatenate([v, jnp.where(lane == 0, 1.0, 0.0).astype(BF16) + jnp.zeros_like(v)], axis=1)

    def weights(s):
        return (jnp.exp2(s - jnp.max(s, axis=-1, keepdims=True)).astype(BF16),)

    def emit(j, e):
        r = jnp.dot(e, v_aug, preferred_element_type=F32)
        o = (r[:sub, :vd] / r[:sub, vd:vd + 1]) - lam * (r[sub:, :vd] / r[sub:, vd:vd + 1])
        o = o * lax.rsqrt(jnp.mean(o * o, axis=-1, keepdims=True) + SUBLN_EPS) * g_ref[...]
        o_ref[pl.ds(j * sub, sub), :] = (o * (1.0 - lam_init)).astype(o_ref.dtype)

    s_of, a_of = {}, {}
    for t in range(nsub + 2):
        if t < nsub:
            s_of[t] = scores(t)
        if 0 <= t - 1 < nsub:
            a_of[t - 1] = weights(s_of.pop(t - 1))
        if 0 <= t - 2 < nsub:
            emit(t - 2, *a_of.pop(t - 2))


def _attention(q, kt, v, lq1, lk1, lq2, lk2, subln_g, head_dim, lam_init, tq=512):
    bsz, seq, width = q.shape
    v_dim = subln_g.shape[-1]
    assert v_dim == 2 * head_dim == LANES
    heads = width // v_dim
    vec = lambda a: pl.BlockSpec((1, a.shape[-1]), lambda b, h, i: (0, 0))
    lams = [a[None] for a in (lq1, lk1, lq2, lk2)]
    return pl.pallas_call(
        functools.partial(_attn_kernel, head_dim=head_dim, lam_init=lam_init),
        grid=(bsz, heads, seq // tq),
        in_specs=[
            pl.BlockSpec((None, tq, v_dim), lambda b, h, i: (b, i, h)),
            pl.BlockSpec((None, v_dim, seq), lambda b, h, i: (b, h, 0)),
            pl.BlockSpec((None, seq, v_dim), lambda b, h, i: (b, 0, h)),
            *[vec(a) for a in lams],
            vec(subln_g[None]),
        ],
        out_specs=pl.BlockSpec((None, tq, v_dim), lambda b, h, i: (b, i, h)),
        out_shape=jax.ShapeDtypeStruct((bsz, seq, width), BF16),
        compiler_params=_cparams("parallel", "parallel", "parallel"),
        name="diff_attention",
    )(q, kt, v, *lams, subln_g[None])


def _merge_kernel(x_ref, yh_ref, ya_ref, gh_ref, ga_ref, wuh_ref, wua_ref, wo_ref, g_ref, wr_ref, br_ref,
                  xo_ref, n_ref, aff_ref):
    m = x_ref.shape[0] // MERGE_ROW_SPLIT
    for j in range(MERGE_ROW_SPLIT):
        r = pl.ds(j * m, m)
        mh = jnp.dot(yh_ref[r, :].astype(BF16), wuh_ref[...], preferred_element_type=F32)
        ma = jnp.dot(ya_ref[r, :].astype(BF16), wua_ref[...], preferred_element_type=F32)
        merged = jax.nn.sigmoid(gh_ref[r, :]) * mh + jax.nn.sigmoid(ga_ref[r, :]) * ma
        x = x_ref[r, :] + jnp.dot(merged.astype(BF16), wo_ref[...], preferred_element_type=F32)
        xo_ref[r, :] = x
        n = x * lax.rsqrt(jnp.mean(x * x, axis=-1, keepdims=True) + NORM_EPS) * g_ref[...]
        n_ref[r, :] = n.astype(BF16)
        logits = lax.dot_general(wr_ref[...], n, (((1,), (1,)), ((), ())), preferred_element_type=F32,
                                 precision=lax.Precision.HIGHEST) + br_ref[...]
        e = jnp.exp(logits - jnp.max(logits, axis=0, keepdims=True))
        aff_ref[:, pl.ds(j * m, m)] = e / jnp.sum(e, axis=0, keepdims=True)


def _merge(x2, yh2, ya2, p2, col_gate, wuh, wua, wo, g, wr_t, br, bsz, seq, tm=512):
    t, d = x2.shape
    c = yh2.shape[1]
    e = wr_t.shape[0]
    jg = col_gate // d
    ns = seq // tm
    const = lambda a: pl.BlockSpec(a.shape, lambda i: (0,) * a.ndim)
    return pl.pallas_call(
        _merge_kernel,
        grid=(t // tm,),
        in_specs=[
            pl.BlockSpec((tm, d), lambda i: (i, 0)),
            pl.BlockSpec((tm, c), lambda i: (i, 0)),
            pl.BlockSpec((tm, ya2.shape[1]), lambda i: (i, 0)),
            pl.BlockSpec((tm, d), lambda i: (i, jg)),
            pl.BlockSpec((tm, d), lambda i: (i, jg + 1)),
            const(wuh), const(wua), const(wo), const(g), const(wr_t), const(br),
        ],
        out_specs=[
            pl.BlockSpec((tm, d), lambda i: (i, 0)),
            pl.BlockSpec((tm, d), lambda i: (i, 0)),
            pl.BlockSpec((None, e, tm), lambda i: (i // ns, 0, i % ns)),
        ],
        out_shape=[
            jax.ShapeDtypeStruct((t, d), F32),
            jax.ShapeDtypeStruct((t, d), BF16),
            jax.ShapeDtypeStruct((bsz, e, seq), F32),
        ],
        compiler_params=_cparams("parallel"),
        name="merge_router",
    )(x2, yh2, ya2, p2, p2, wuh, wua, wo, g, wr_t, br)


def _select_kernel(aff_ref, pos_ref, *, cap):
    a = aff_ref[...]
    rows, seq = a.shape
    as_f32 = lambda b: lax.bitcast_convert_type(b, F32)
    count = lambda m: jnp.sum(jnp.where(m, 1.0, 0.0), axis=-1, keepdims=True)
    thr = jnp.zeros((rows, 1), jnp.int32)
    for bit in range(30, -1, -1):
        cand = thr | (1 << bit)
        thr = jnp.where(count(a >= as_f32(cand)) >= cap, cand, thr)
    gt = a >= as_f32(thr + 1)
    eq = (a >= as_f32(thr)) & jnp.logical_not(gt)
    need = cap - count(gt)
    tri = jnp.where(lax.broadcasted_iota(jnp.int32, (LANES, LANES), 0)
                    <= lax.broadcasted_iota(jnp.int32, (LANES, LANES), 1), 1.0, 0.0).astype(BF16)

    def exclusive_cumsum(mask):
        ones = jnp.where(mask, 1.0, 0.0)
        carry = jnp.zeros((rows, 1), F32)
        chunks = []
        for j in range(seq // LANES):
            blk = ones[:, j * LANES:(j + 1) * LANES]
            incl = jnp.dot(blk.astype(BF16), tri, preferred_element_type=F32)
            chunks.append(incl - blk + carry)
            carry = carry + jnp.sum(blk, axis=-1, keepdims=True)
        return jnp.concatenate(chunks, axis=1)

    sel = gt | (eq & (exclusive_cumsum(eq) < need))
    pos_ref[...] = jnp.where(sel, exclusive_cumsum(sel), -1.0).astype(jnp.int32)


def _select(aff_rows, cap):
    return pl.pallas_call(
        functools.partial(_select_kernel, cap=cap),
        out_shape=jax.ShapeDtypeStruct(aff_rows.shape, jnp.int32),
        compiler_params=pltpu.CompilerParams(vmem_limit_bytes=VMEM_LIMIT_V7X),
        name="expert_select",
    )(aff_rows)


def _gather_kernel(starts_ref, pos_ref, aff_ref, n_ref, o_ref, gate_ref, *, win):
    b = pl.program_id(0)
    i = pl.program_id(1)
    e, cap, _ = o_ref.shape
    ts = n_ref.shape[0]

    @pl.when(i == 0)
    def _():
        o_ref[...] = jnp.zeros_like(o_ref)
        gate_ref[...] = jnp.zeros_like(gate_ref)

    slot0 = lax.broadcasted_iota(jnp.int32, (win, ts), 0)
    pos = pos_ref[...]
    aff = aff_ref[...]

    def first_row(x):
        lo = starts_ref[b, x, i]
        return pl.multiple_of(jnp.minimum((lo // BF16_ROWS) * BF16_ROWS, cap - win), BF16_ROWS)

    def add_gates(x, dst, match):
        picked = jnp.sum(jnp.where(match, aff[x:x + 1, :], 0.0), axis=-1, keepdims=True)
        gate_ref[x, dst, :] = gate_ref[x, dst, :] + picked

    for x0 in range(0, e, GATHER_GROUP):
        group = range(x0, min(x0 + GATHER_GROUP, e))
        onehots = []
        for x in group:
            base = first_row(x)
            match = (slot0 + base) == pos[x:x + 1, :]
            add_gates(x, pl.ds(base, win), match)
            onehots.append(match.astype(BF16))
        picked = jnp.dot(jnp.concatenate(onehots, axis=0), n_ref[...], preferred_element_type=F32)
        for k, x in enumerate(group):
            dst = pl.ds(first_row(x), win)
            o_ref[x, dst, :] = o_ref[x, dst, :] + picked[k * win:(k + 1) * win].astype(BF16)

    for x in range(e):
        base = first_row(x)
        hi = starts_ref[b, x, i + 1]

        def extra(k, carry, base=base, x=x):
            want = base + k * win
            row = pl.multiple_of(jnp.minimum(want, cap - win), BF16_ROWS)
            slots = slot0 + row
            match = jnp.logical_and(slots == pos[x:x + 1, :], slots >= want)
            dst = pl.ds(row, win)
            add_gates(x, dst, match)
            o_ref[x, dst, :] = o_ref[x, dst, :] + jnp.dot(match.astype(BF16), n_ref[...],
                                                          preferred_element_type=F32).astype(BF16)
            return carry

        windows = (jnp.maximum(hi - base, 1) + win - 1) // win
        lax.fori_loop(1, windows, extra, 0)


def _gather(pos, aff, starts, n3, cap, ts=COMBINE_TILE, win=128):
    bsz, e, seq = pos.shape
    d = n3.shape[-1]
    assert cap % BF16_ROWS == 0 and win % BF16_ROWS == 0 and win <= cap
    tile = pl.BlockSpec((None, e, ts), lambda b, i, st: (b, 0, i))
    grid_spec = pltpu.PrefetchScalarGridSpec(
        num_scalar_prefetch=1,
        grid=(bsz, seq // ts),
        in_specs=[tile, tile, pl.BlockSpec((None, ts, d), lambda b, i, st: (b, i, 0))],
        out_specs=[
            pl.BlockSpec((None, e, cap, d), lambda b, i, st: (b, 0, 0, 0)),
            pl.BlockSpec((None, e, cap, 1), lambda b, i, st: (b, 0, 0, 0)),
        ],
    )
    return pl.pallas_call(
        functools.partial(_gather_kernel, win=win),
        grid_spec=grid_spec,
        out_shape=[
            jax.ShapeDtypeStruct((bsz, e, cap, d), BF16),
            jax.ShapeDtypeStruct((bsz, e, cap, 1), F32),
        ],
        compiler_params=_cparams("parallel", "arbitrary"),
        name="expert_gather",
    )(starts, pos, aff, n3)


def _expert_kernel(x_ref, gate_ref, wg_ref, wu_ref, wd_ref, o_ref, acc_sc):
    s = pl.program_id(1)
    bg = pl.program_id(2)
    rows, cap, d = x_ref.shape

    @pl.when(s == 0)
    def _():
        acc_sc[bg] = jnp.zeros((rows * cap, d), F32)

    x = x_ref[...].reshape(rows * cap, d)
    wg = wg_ref[...].astype(BF16)
    wu = wu_ref[...].astype(BF16)
    wd = wd_ref[...].astype(BF16)
    m = rows * cap // EXPERT_ROW_SPLIT
    parts = []
    for j in range(EXPERT_ROW_SPLIT):
        xj = x[j * m:(j + 1) * m]
        parts.append((jnp.dot(xj, wg, preferred_element_type=F32), jnp.dot(xj, wu, preferred_element_type=F32)))
    ys = []
    for g, u in parts:
        h = (g * jax.nn.sigmoid(g) * u).astype(BF16)
        ys.append(jnp.dot(h, wd, preferred_element_type=F32))
    total = acc_sc[bg] + jnp.concatenate(ys, axis=0)
    acc_sc[bg] = total
    o_ref[...] = (total * gate_ref[...].reshape(rows * cap, 1)).astype(BF16).reshape(rows, cap, d)


def _experts(xg, gate, wg4, wu4, wd4, layer, f_slices=2, rows=1):
    bsz, e, cap, d = xg.shape
    f = wg4.shape[-1]
    assert f_slices >= 2 and f % f_slices == 0 and bsz % rows == 0
    fs = f // f_slices
    last = f_slices - 1
    groups = bsz // rows
    out_idx = lambda x, s, bg: (jnp.where(s == last, bg, 0), x, 0, 0)
    return pl.pallas_call(
        _expert_kernel,
        grid=(e, f_slices, groups),
        in_specs=[
            pl.BlockSpec((rows, None, cap, d), lambda x, s, bg: (bg, x, 0, 0)),
            pl.BlockSpec((rows, None, cap, 1), lambda x, s, bg: (bg, x, 0, 0)),
            pl.BlockSpec((None, None, d, fs), lambda x, s, bg: (layer, x, 0, s)),
            pl.BlockSpec((None, None, d, fs), lambda x, s, bg: (layer, x, 0, s)),
            pl.BlockSpec((None, None, fs, d), lambda x, s, bg: (layer, x, s, 0)),
        ],
        out_specs=pl.BlockSpec((rows, None, cap, d), out_idx),
        out_shape=jax.ShapeDtypeStruct((bsz, e, cap, d), BF16),
        scratch_shapes=[pltpu.VMEM((groups, rows * cap, d), F32)],
        compiler_params=_cparams("arbitrary", "arbitrary", "arbitrary"),
        name="expert_ffn",
    )(xg, gate, wg4, wu4, wd4)


def _combine_kernel(starts_ref, x_ref, pos_ref, ye_ref, g_ref, o_ref, stage_sc, *, final, win):
    b = pl.program_id(0)
    i = pl.program_id(1)
    ts = x_ref.shape[0]
    e, cap, _ = ye_ref.shape
    lane = lax.broadcasted_iota(jnp.int32, (ts, win), 1)
    pos = pos_ref[...]

    def first_row(x):
        lo = starts_ref[b, x, i]
        return pl.multiple_of(jnp.minimum((lo // BF16_ROWS) * BF16_ROWS, cap - win), BF16_ROWS)

    onehots = []
    for x in range(e):
        base = first_row(x)
        stage_sc[pl.ds(x * win, win), :] = ye_ref[x, pl.ds(base, win), :]
        onehots.append((pos[:, x:x + 1] - base == lane).astype(BF16))
    o_ref[...] = x_ref[...] + jnp.dot(jnp.concatenate(onehots, axis=1), stage_sc[...],
                                      preferred_element_type=F32)

    for x in range(e):
        base = first_row(x)
        hi = starts_ref[b, x, i + 1]
        col = pos[:, x:x + 1]

        def extra(k, carry, base=base, col=col, x=x):
            want = base + k * win
            row = pl.multiple_of(jnp.minimum(want, cap - win), BF16_ROWS)
            onehot = jnp.logical_and(col - row == lane, col >= want).astype(BF16)
            o_ref[...] += jnp.dot(onehot, ye_ref[x, pl.ds(row, win), :], preferred_element_type=F32)
            return carry

        windows = (jnp.maximum(hi - base, 1) + win - 1) // win
        lax.fori_loop(1, windows, extra, 0)

    if final:
        acc = o_ref[...]
        o_ref[...] = acc * lax.rsqrt(jnp.mean(acc * acc, axis=-1, keepdims=True) + NORM_EPS) * g_ref[...]


def _combine(x3, pos_t, starts, ye, g, final, ts=512, win=128):
    bsz, seq, d = x3.shape
    e, cap = ye.shape[1], ye.shape[2]
    assert cap % BF16_ROWS == 0 and win % BF16_ROWS == 0 and win <= cap
    grid_spec = pltpu.PrefetchScalarGridSpec(
        num_scalar_prefetch=1,
        grid=(bsz, seq // ts),
        in_specs=[
            pl.BlockSpec((None, ts, d), lambda b, i, st: (b, i, 0)),
            pl.BlockSpec((None, ts, e), lambda b, i, st: (b, i, 0)),
            _single((None, e, cap, d), lambda b, i, st: (b, 0, 0, 0)),
            pl.BlockSpec((1, d), lambda b, i, st: (0, 0)),
        ],
        out_specs=pl.BlockSpec((None, ts, d), lambda b, i, st: (b, i, 0)),
        scratch_shapes=[pltpu.VMEM((e * win, d), BF16)],
    )
    return pl.pallas_call(
        functools.partial(_combine_kernel, final=final, win=win),
        grid_spec=grid_spec,
        out_shape=jax.ShapeDtypeStruct((bsz, seq, d), F32),
        compiler_params=_cparams("parallel", "parallel"),
        name="expert_combine",
    )(starts, x3, pos_t, ye, g)


def kernel(x, norm_mix, w_in, b_in, hy_conv_w, hy_conv_b, hy_ffn_w1, hy_ffn_b1, hy_ffn_f1, hy_ffn_w2, hy_ffn_b2, hy_ffn_f2, hy_ffn_w3, hy_bias, lambda_q1, lambda_k1, lambda_q2, lambda_k2, subln_g, w_up_hyena, w_up_attn, w_out, norm_ffn, w_router, b_router, w_e_gate, w_e_up, w_e_down, norm_final):
    bsz, seq, d = x.shape
    depth = w_in.shape[0]
    orders, c = hy_bias.shape[1], hy_bias.shape[2]
    head_dim = lambda_q1.shape[1]
    v_width = w_up_attn.shape[1]
    qk_width = v_width
    e = w_router.shape[2]
    cap = EC_FACTOR * seq // e
    col_q = (orders + 1) * c
    col_gate = col_q + 2 * qk_width + v_width
    assert orders == 2 and bsz % 2 == 0 and col_gate % d == 0

    consts = tuple(jnp.asarray(a, F32).astype(BF16) for a in _dft_constants(seq))

    xs = x.reshape(bsz * seq, d)
    out = None
    for l in range(depth):
        p2 = _inproj(xs, norm_mix[l][None], w_in[l].astype(BF16), b_in[l][None])
        p3 = p2.reshape(bsz, seq, -1)

        uc = _shortconv(p3, hy_conv_w[l], hy_conv_b[l][None], c)
        hfull = _filters(hy_ffn_w1[l], hy_ffn_b1[l], hy_ffn_f1[l], hy_ffn_w2[l], hy_ffn_b2[l], hy_ffn_f2[l],
                        hy_ffn_w3[l], hy_bias[l], seq)
        hspec = _spectrum(hfull, consts[4], consts[1])
        z = _longconv(uc, 0, uc, 1, hspec, 0, consts, natural_out=False)
        y_hy = _longconv(z, 0, uc, 2, hspec, 1, consts, natural_out=True)

        q_r, k_t, v_b = _rope(p2, bsz, seq, qk_width, v_width, head_dim, col_q)
        lam_init = 0.8 - 0.6 * math.exp(-0.3 * l)
        y_da = _attention(q_r.reshape(bsz, seq, qk_width), k_t, v_b.reshape(bsz, seq, v_width),
                          lambda_q1[l], lambda_k1[l], lambda_q2[l], lambda_k2[l], subln_g[l], head_dim, lam_init)

        xs, n2, aff = _merge(xs, y_hy.reshape(bsz * seq, c), y_da.reshape(bsz * seq, v_width), p2, col_gate,
                             w_up_hyena[l].astype(BF16), w_up_attn[l].astype(BF16), w_out[l].astype(BF16),
                             norm_ffn[l][None], w_router[l].T, b_router[l][:, None], bsz, seq)

        pos = _select(aff.reshape(bsz * e, seq), cap).reshape(bsz, e, seq)
        tiles = seq // COMBINE_TILE
        counts = jnp.sum((pos >= 0).reshape(bsz, e, tiles, COMBINE_TILE), axis=-1, dtype=jnp.int32)
        starts = jnp.concatenate([jnp.zeros((bsz, e, 1), jnp.int32), jnp.cumsum(counts, axis=-1)], axis=-1)
        xg, gate = _gather(pos, aff, starts, n2.reshape(bsz, seq, d), cap)
        ye = _experts(xg, gate, w_e_gate, w_e_up, w_e_down, l)
        final = l == depth - 1
        out = _combine(xs.reshape(bsz, seq, d), pos.transpose(0, 2, 1), starts, ye, norm_final[None], final,
                       ts=COMBINE_TILE)
        xs = out.reshape(bsz * seq, d)
    return out
```

```python
import functools
import math

import numpy as np
import jax
import jax.numpy as jnp
from jax import lax
from jax.experimental import pallas as pl
from jax.experimental.pallas import tpu as pltpu

F32 = jnp.float32
BF16 = jnp.bfloat16

NORM_EPS = 1e-6
SUBLN_EPS = 1e-5
ROPE_THETA = 10000.0
HY_FAST_DECAY = 0.3
HY_SLOW_DECAY = 1.5
HY_TARGET = 1e-2
EC_FACTOR = 2

VMEM_LIMIT_V7X = 56 * 1024 * 1024
LANES = 128
BF16_ROWS = 16
COMBINE_TILE = 512
GATHER_GROUP = 16
MERGE_ROW_SPLIT = 1
EXPERT_ROW_SPLIT = 1
ATTN_SUB_ROWS = 128

DFT_N1 = 64
DFT_N2 = 128
SUBLANES = 8
A_PITCH = 2 * DFT_N1 + SUBLANES
V_PITCH = 2 * DFT_N2 + SUBLANES
H_PITCH = DFT_N2 + SUBLANES
UNROLL_STAGE_A = 64
UNROLL_STAGE_B = 16
STAGE_B_COLS = 2
UNROLL_SHORTCONV = 8


def _cparams(*sem):
    return pltpu.CompilerParams(dimension_semantics=sem, vmem_limit_bytes=VMEM_LIMIT_V7X)


def _single(block_shape, index_map):
    return pl.BlockSpec(block_shape, index_map, pipeline_mode=pl.Buffered(1))


def _inproj_kernel(x_ref, g_ref, w_ref, b_ref, o_ref, n_sc):
    @pl.when(pl.program_id(1) == 0)
    def _():
        x = x_ref[...]
        n = x * lax.rsqrt(jnp.mean(x * x, axis=-1, keepdims=True) + NORM_EPS) * g_ref[...]
        n_sc[...] = n.astype(BF16)

    o_ref[...] = jnp.dot(n_sc[...], w_ref[...], preferred_element_type=F32) + b_ref[...]


def _inproj(x2, g, w_bf, b, tm=2048, tn=1024):
    t, d = x2.shape
    width = w_bf.shape[1]
    return pl.pallas_call(
        _inproj_kernel,
        grid=(t // tm, width // tn),
        in_specs=[
            pl.BlockSpec((tm, d), lambda i, j: (i, 0)),
            pl.BlockSpec((1, d), lambda i, j: (0, 0)),
            pl.BlockSpec((d, tn), lambda i, j: (0, j)),
            pl.BlockSpec((1, tn), lambda i, j: (0, j)),
        ],
        out_specs=pl.BlockSpec((tm, tn), lambda i, j: (i, j)),
        out_shape=jax.ShapeDtypeStruct((t, width), F32),
        scratch_shapes=[pltpu.VMEM((tm, d), BF16)],
        compiler_params=_cparams("parallel", "arbitrary"),
        name="inproj",
    )(x2, g, w_bf, b)


def _shortconv_kernel(u_ref, w_ref, b_ref, o_ref, pad_sc, *, seq, n2, nh):
    zero_row = jnp.zeros((1, LANES), F32)
    for q in range(nh):
        base = q * H_PITCH + SUBLANES
        pad_sc[pl.ds(base, n2), :] = u_ref[pl.ds(q * n2, n2), :]
        pad_sc[pl.ds(base - 1, 1), :] = u_ref[pl.ds(q * n2 - 1, 1), :] if q > 0 else zero_row
        pad_sc[pl.ds(base + n2, 1), :] = u_ref[pl.ds((q + 1) * n2, 1), :] if q < nh - 1 else zero_row
    w = w_ref[...]
    bias = b_ref[...]

    column = lambda t: pad_sc[pl.ds(SUBLANES - 1 + t, nh, stride=H_PITCH), :]

    def body(r, taps):
        prev, cur = taps
        nxt = column(r + 2)
        o_ref[pl.ds(pl.multiple_of(r * nh, nh), nh), :] = prev * w[0:1] + cur * w[1:2] + nxt * w[2:3] + bias
        return cur, nxt

    lax.fori_loop(0, n2, body, (column(0), column(1)), unroll=UNROLL_SHORTCONV)


def _shortconv(p3, conv_w, conv_b, c):
    bsz, seq, _ = p3.shape
    parts = conv_w.shape[1] // c
    cb_per_part = c // LANES
    nh = seq // DFT_N2
    return pl.pallas_call(
        functools.partial(_shortconv_kernel, seq=seq, n2=DFT_N2, nh=nh),
        grid=(bsz, parts * cb_per_part),
        in_specs=[
            pl.BlockSpec((None, seq, LANES), lambda b, j: (b, 0, j)),
            pl.BlockSpec((3, LANES), lambda b, j: (0, j)),
            pl.BlockSpec((1, LANES), lambda b, j: (0, j)),
        ],
        out_specs=pl.BlockSpec((None, None, None, seq, LANES),
                               lambda b, j: (j // cb_per_part, b, j % cb_per_part, 0, 0)),
        out_shape=jax.ShapeDtypeStruct((parts, bsz, cb_per_part, seq, LANES), F32),
        scratch_shapes=[pltpu.VMEM((nh * H_PITCH + SUBLANES, LANES), F32)],
        compiler_params=_cparams("parallel", "parallel"),
        name="shortconv",
    )(p3, conv_w, conv_b)


def _filter_kernel(w1t_ref, w1c_ref, w1s_ref, b1_ref, f1_ref, w2_ref, b2_ref, f2_ref, w3_ref, bias_ref, o_ref,
                   *, seq, tl, c, bands):
    hi = lax.Precision.HIGHEST
    lag = lambda idx: jnp.where(idx < seq, idx, 2 * seq - idx).astype(F32)
    row = lax.broadcasted_iota(jnp.int32, (tl, 1), 0) + pl.program_id(0) * tl
    t = lag(row) / (seq - 1.0)
    pos = lag(lax.broadcasted_iota(jnp.int32, (1, tl), 1) + pl.program_id(0) * tl)
    t_l = pos / (seq - 1.0)
    w_l = (2.0 * math.pi) * pos / float(seq)
    band = lax.broadcasted_iota(jnp.int32, (bands, 1), 0).astype(F32)
    fr = 1e-4 + band * ((bands - 1 - 1e-4) / (bands - 1))
    ang = fr * w_l
    pre = (w1t_ref[...] * t_l
           + jnp.dot(w1c_ref[...], jnp.cos(ang), preferred_element_type=F32, precision=hi)
           - jnp.dot(w1s_ref[...], jnp.sin(ang), preferred_element_type=F32, precision=hi)
           + b1_ref[...])
    h = jnp.sin(f1_ref[...] * pre)
    h = jnp.sin(f2_ref[...] * (jnp.dot(w2_ref[...], h, preferred_element_type=F32, precision=hi) + b2_ref[...]))
    h = jnp.dot(h.T, w3_ref[...], preferred_element_type=F32, precision=hi)
    min_decay = math.log(HY_TARGET) / HY_FAST_DECAY
    max_decay = math.log(HY_TARGET) / HY_SLOW_DECAY
    ch = lax.broadcasted_iota(jnp.int32, (1, c), 1).astype(F32)
    deltas = jnp.abs(min_decay + ch * ((max_decay - min_decay) / (c - 1)))
    decay = jnp.exp(-t * deltas)
    orders = o_ref.shape[0]
    for o in range(orders):
        taps = h[:, o * c:(o + 1) * c] * decay
        taps = jnp.where(row == 0, taps + bias_ref[o:o + 1, :], taps)
        taps = jnp.where(row == seq, 0.0, taps)
        for blk in range(tl // DFT_N2):
            for cb in range(c // LANES):
                o_ref[o, cb, pl.ds(blk * H_PITCH, DFT_N2), :] = taps[blk * DFT_N2:(blk + 1) * DFT_N2,
                                                                   cb * LANES:(cb + 1) * LANES]
                o_ref[o, cb, pl.ds(blk * H_PITCH + DFT_N2, H_PITCH - DFT_N2), :] = jnp.zeros(
                    (H_PITCH - DFT_N2, LANES), F32)


def _filters(w1, b1, f1, w2, b2, f2, w3, bias, seq, tl=512):
    emb, ffn = w1.shape
    bands = (emb - 1) // 2
    orders, c = bias.shape
    tiles = seq // tl
    w3_dir = w3.reshape(ffn, orders, 2, c).transpose(2, 0, 1, 3).reshape(2, ffn, orders * c)
    full = lambda a: pl.BlockSpec(a.shape, lambda i: (0,) * a.ndim)
    col = lambda a: a[:, None]
    args = (w1[0:1].T, w1[1:1 + bands].T, w1[1 + bands:].T, col(b1), col(f1), w2.T, col(b2), col(f2), w3_dir, bias)
    in_specs = [full(a) for a in args]
    in_specs[8] = pl.BlockSpec((None, ffn, orders * c), lambda i: (i // tiles, 0, 0))
    return pl.pallas_call(
        functools.partial(_filter_kernel, seq=seq, tl=tl, c=c, bands=bands),
        grid=(2 * tiles,),
        in_specs=in_specs,
        out_specs=pl.BlockSpec((orders, c // LANES, tl // DFT_N2 * H_PITCH, LANES), lambda i: (0, 0, i, 0)),
        out_shape=jax.ShapeDtypeStruct((orders, c // LANES, 2 * seq // DFT_N2 * H_PITCH, LANES), F32),
        compiler_params=_cparams("parallel"),
        name="hyena_filters",
    )(*args)


def _stack_complex(m):
    return np.block([[m.real, -m.imag], [m.imag, m.real]])


@functools.lru_cache(maxsize=None)
def _dft_constants(seq):
    n1, n2 = DFT_N1, DFT_N2
    n = n1 * n2
    assert n == 2 * seq
    nh = seq // n2
    k1 = np.arange(n1)[:, None].astype(np.float64)
    q = np.arange(nh)[None, :].astype(np.float64)
    qf = np.arange(n1)[None, :].astype(np.float64)
    ma = np.empty((n2, 2 * n1, 2 * nh), np.float64)
    mai = np.empty((n2, 2 * nh, 2 * n1), np.float64)
    maf = np.empty((n2, 2 * n1, n1), np.float64)
    for r in range(n2):
        e = np.exp(-2j * np.pi * (q * k1 / n1 + r * k1 / n))
        ma[r] = _stack_complex(e)
        mai[r] = _stack_complex(np.conj(e).T / n)
        ef = np.exp(-2j * np.pi * (qf * k1 / n1 + r * k1 / n))
        maf[r] = np.concatenate([ef.real, ef.imag], axis=0)
    kk = np.arange(n2)[:, None].astype(np.float64)
    rr = np.arange(n2)[None, :].astype(np.float64)
    f = np.exp(-2j * np.pi * kk * rr / n2)
    mb = _stack_complex(f)
    mbi = _stack_complex(np.conj(f).T)
    return tuple(np.asarray(a, np.float32) for a in (ma, mb, mbi, mai, maf))


def _stage_a(load_x, ma_ref, a_sc):
    def body(r, carry):
        a_sc[pl.ds(pl.multiple_of(r * A_PITCH, SUBLANES), 2 * DFT_N1), :] = jnp.dot(
            ma_ref[r], load_x(r), preferred_element_type=F32)
        return carry

    lax.fori_loop(0, DFT_N2, body, 0, unroll=UNROLL_STAGE_A)


def _load_a_columns(a_sc, k1):
    cols = []
    for j in range(STAGE_B_COLS):
        ar = a_sc[pl.ds(k1 + j, DFT_N2, stride=A_PITCH), :]
        ai = a_sc[pl.ds(DFT_N1 + k1 + j, DFT_N2, stride=A_PITCH), :]
        cols.append(jnp.concatenate([ar, ai], axis=0))
    return jnp.concatenate(cols, axis=1).astype(BF16)


def _spectrum_kernel(h_ref, maf_ref, mb_ref, o_ref, a_sc):
    rows = 2 * DFT_N2

    def load_x(r):
        return h_ref[pl.ds(r, DFT_N1, stride=H_PITCH), :].astype(BF16)

    _stage_a(load_x, maf_ref, a_sc)

    cb = o_ref.shape[-1]

    def body(kp, carry):
        k1 = kp * STAGE_B_COLS
        z = jnp.dot(mb_ref[...], _load_a_columns(a_sc, k1), preferred_element_type=F32)
        for j in range(STAGE_B_COLS):
            o_ref[pl.ds(pl.multiple_of((k1 + j) * rows, rows), rows), :] = z[:, j * cb:(j + 1) * cb]
        return carry

    lax.fori_loop(0, DFT_N1 // STAGE_B_COLS, body, 0, unroll=UNROLL_STAGE_B)


def _spectrum(hfull, maf, mb):
    orders, cbs, n, _ = hfull.shape
    rows = 2 * DFT_N1 * DFT_N2
    return pl.pallas_call(
        _spectrum_kernel,
        grid=(orders, cbs),
        in_specs=[
            pl.BlockSpec((None, None, n, LANES), lambda o, j: (o, j, 0, 0)),
            _single(maf.shape, lambda o, j: (0, 0, 0)),
            _single(mb.shape, lambda o, j: (0, 0)),
        ],
        out_specs=pl.BlockSpec((None, None, rows, LANES), lambda o, j: (o, j, 0, 0)),
        out_shape=jax.ShapeDtypeStruct((orders, cbs, rows, LANES), F32),
        scratch_shapes=[pltpu.VMEM((DFT_N2 * A_PITCH, LANES), F32)],
        compiler_params=_cparams("parallel", "parallel"),
        name="hyena_spectrum",
    )(hfull, maf, mb)


def _longconv_kernel(z_ref, gate_ref, h_ref, ma_ref, mb_ref, mbi_ref, mai_ref, o_ref, a_sc, v_sc,
                     *, nh, natural_out):
    def load_x(r):
        src = pl.ds(pl.multiple_of(r * nh, nh), nh)
        return jnp.concatenate([z_ref[0, src, :], z_ref[1, src, :]], axis=0).astype(BF16)

    _stage_a(load_x, ma_ref, a_sc)

    rows = 2 * DFT_N2

    cb = o_ref.shape[-1]

    def freq(kp, carry):
        k1 = kp * STAGE_B_COLS
        zf = jnp.dot(mb_ref[...], _load_a_columns(a_sc, k1), preferred_element_type=F32)
        ys = []
        for j in range(STAGE_B_COLS):
            base = pl.multiple_of((k1 + j) * rows, rows)
            hr = h_ref[pl.ds(base, DFT_N2), :]
            hi = h_ref[pl.ds(base + DFT_N2, DFT_N2), :]
            zr = zf[:DFT_N2, j * cb:(j + 1) * cb]
            zi = zf[DFT_N2:, j * cb:(j + 1) * cb]
            ys.append(jnp.concatenate([zr * hr - zi * hi, zr * hi + zi * hr], axis=0))
        v = jnp.dot(mbi_ref[...], jnp.concatenate(ys, axis=1).astype(BF16), preferred_element_type=F32)
        for j in range(STAGE_B_COLS):
            v_sc[pl.ds(pl.multiple_of((k1 + j) * V_PITCH, SUBLANES), rows), :] = v[:, j * cb:(j + 1) * cb]
        return carry

    lax.fori_loop(0, DFT_N1 // STAGE_B_COLS, freq, 0, unroll=UNROLL_STAGE_B)

    def back(r, carry):
        vr = v_sc[pl.ds(r, DFT_N1, stride=V_PITCH), :]
        vi = v_sc[pl.ds(DFT_N2 + r, DFT_N1, stride=V_PITCH), :]
        y = jnp.dot(mai_ref[r], jnp.concatenate([vr, vi], axis=0).astype(BF16), preferred_element_type=F32)
        src = pl.ds(pl.multiple_of(r * nh, nh), nh)
        dst = pl.ds(r, nh, stride=DFT_N2) if natural_out else src
        o_ref[0, dst, :] = y[:nh] * gate_ref[0, src, :]
        o_ref[1, dst, :] = y[nh:] * gate_ref[1, src, :]
        return carry

    lax.fori_loop(0, DFT_N2, back, 0, unroll=UNROLL_STAGE_A)


def _longconv(z4, z_part, gate4, gate_part, h3, order, consts, natural_out):
    _, bsz, cbs, seq, _ = z4.shape
    ma, mb, mbi, mai = consts[:4]
    nh = seq // DFT_N2
    rows = 2 * DFT_N1 * DFT_N2
    pair = lambda part: (lambda j, p: (part, p, j, 0, 0))
    if natural_out:
        out_spec = pl.BlockSpec((None, 2, seq, LANES), lambda j, p: (0, p, 0, j))
        out_shape = jax.ShapeDtypeStruct((1, bsz, seq, cbs * LANES), F32)
    else:
        out_spec = pl.BlockSpec((None, 2, None, seq, LANES), pair(0))
        out_shape = jax.ShapeDtypeStruct((1, bsz, cbs, seq, LANES), F32)
    return pl.pallas_call(
        functools.partial(_longconv_kernel, nh=nh, natural_out=natural_out),
        grid=(cbs, bsz // 2),
        in_specs=[
            _single((None, 2, None, seq, LANES), pair(z_part)),
            _single((None, 2, None, seq, LANES), pair(gate_part)),
            _single((None, None, rows, LANES), lambda j, p: (order, j, 0, 0)),
            _single(ma.shape, lambda j, p: (0, 0, 0)),
            _single(mb.shape, lambda j, p: (0, 0)),
            _single(mbi.shape, lambda j, p: (0, 0)),
            _single(mai.shape, lambda j, p: (0, 0, 0)),
        ],
        out_specs=out_spec,
        out_shape=out_shape,
        scratch_shapes=[pltpu.VMEM((DFT_N2 * A_PITCH, LANES), F32), pltpu.VMEM((DFT_N1 * V_PITCH, LANES), F32)],
        compiler_params=_cparams("parallel", "parallel"),
        name="hyena_longconv",
    )(z4, gate4, h3, ma, mb, mbi, mai)


def _rope_kernel(q_ref, k_ref, v_ref, cos_ref, sin_ref, qo_ref, kt_ref, vo_ref, *, half, scale):
    cos = cos_ref[...]
    sin = sin_ref[...]
    lane = lax.broadcasted_iota(jnp.int32, (1, LANES), 1)
    first_half = (lane % (2 * half)) < half

    def rot(x):
        outs = []
        for j in range(x.shape[1] // LANES):
            xb = x[:, j * LANES:(j + 1) * LANES]
            partner = jnp.where(first_half, pltpu.roll(xb, LANES - half, axis=1), pltpu.roll(xb, half, axis=1))
            outs.append(xb * cos + partner * sin)
        return jnp.concatenate(outs, axis=1)

    qo_ref[...] = (rot(q_ref[...]) * scale).astype(BF16)
    kt_ref[...] = rot(k_ref[...]).T.astype(BF16)
    vo_ref[...] = v_ref[...].astype(BF16)


def _rope(p2, bsz, seq, qk_width, v_width, head_dim, col_q, tm=512):
    t = p2.shape[0]
    assert qk_width == v_width and col_q % qk_width == 0
    jq = col_q // qk_width
    half = head_dim // 2
    inv = ROPE_THETA ** (-jnp.arange(half, dtype=F32) * 2.0 / head_dim)
    ang = jnp.arange(seq, dtype=F32)[:, None] * inv[None, :]
    cos, sin = jnp.cos(ang), jnp.sin(ang)
    reps = LANES // head_dim
    cos_t = jnp.tile(jnp.concatenate([cos, cos], axis=1), (1, reps))
    sin_t = jnp.tile(jnp.concatenate([-sin, sin], axis=1), (1, reps))
    ns = seq // tm
    return pl.pallas_call(
        functools.partial(_rope_kernel, half=half, scale=head_dim ** -0.5 * math.log2(math.e)),
        grid=(t // tm,),
        in_specs=[
            pl.BlockSpec((tm, qk_width), lambda i: (i, jq)),
            pl.BlockSpec((tm, qk_width), lambda i: (i, jq + 1)),
            pl.BlockSpec((tm, v_width), lambda i: (i, jq + 2)),
            pl.BlockSpec((tm, LANES), lambda i: (i % ns, 0)),
            pl.BlockSpec((tm, LANES), lambda i: (i % ns, 0)),
        ],
        out_specs=[
            pl.BlockSpec((tm, qk_width), lambda i: (i, 0)),
            pl.BlockSpec((None, qk_width, tm), lambda i: (i // ns, 0, i % ns)),
            pl.BlockSpec((tm, v_width), lambda i: (i, 0)),
        ],
        out_shape=[
            jax.ShapeDtypeStruct((t, qk_width), BF16),
            jax.ShapeDtypeStruct((bsz, qk_width, seq), BF16),
            jax.ShapeDtypeStruct((t, v_width), BF16),
        ],
        compiler_params=_cparams("parallel"),
        name="rope",
    )(p2, p2, p2, cos_t, sin_t)


def _attn_kernel(q_ref, kt_ref, v_ref, lq1_ref, lk1_ref, lq2_ref, lk2_ref, g_ref, o_ref, *, head_dim, lam_init):
    lam = (jnp.exp(jnp.sum(lq1_ref[...] * lk1_ref[...], axis=-1, keepdims=True))
           - jnp.exp(jnp.sum(lq2_ref[...] * lk2_ref[...], axis=-1, keepdims=True)) + lam_init)
    lane = lax.broadcasted_iota(jnp.int32, (1, q_ref.shape[1]), 1)
    sub = ATTN_SUB_ROWS
    nsub = q_ref.shape[0] // sub

    def scores(j):
        q = q_ref[pl.ds(j * sub, sub), :]
        zero = jnp.zeros_like(q)
        qq = jnp.concatenate([jnp.where(lane < head_dim, q, zero), jnp.where(lane >= head_dim, q, zero)], axis=0)
        return jnp.dot(qq, kt_ref[...], preferred_element_type=F32)

    v = v_ref[...]
    vd = v.shape[1]
    v_aug = jnp.concatenate([v, jnp.where(lane == 0, 1.0, 0.0).astype(BF16) + jnp.zeros_like(v)], axis=1)

    def weights(s):
        return (jnp.exp2(s - jnp.max(s, axis=-1, keepdims=True)).astype(BF16),)

    def emit(j, e):
        r = jnp.dot(e, v_aug, preferred_element_type=F32)
        o = (r[:sub, :vd] / r[:sub, vd:vd + 1]) - lam * (r[sub:, :vd] / r[sub:, vd:vd + 1])
        o = o * lax.rsqrt(jnp.mean(o * o, axis=-1, keepdims=True) + SUBLN_EPS) * g_ref[...]
        o_ref[pl.ds(j * sub, sub), :] = (o * (1.0 - lam_init)).astype(o_ref.dtype)

    s_of, a_of = {}, {}
    for t in range(nsub + 2):
        if t < nsub:
            s_of[t] = scores(t)
        if 0 <= t - 1 < nsub:
            a_of[t - 1] = weights(s_of.pop(t - 1))
        if 0 <= t - 2 < nsub:
            emit(t - 2, *a_of.pop(t - 2))


def _attention(q, kt, v, lq1, lk1, lq2, lk2, subln_g, head_dim, lam_init, tq=512):
    bsz, seq, width = q.shape
    v_dim = subln_g.shape[-1]
    assert v_dim == 2 * head_dim == LANES
    heads = width // v_dim
    vec = lambda a: pl.BlockSpec((1, a.shape[-1]), lambda b, h, i: (0, 0))
    lams = [a[None] for a in (lq1, lk1, lq2, lk2)]
    return pl.pallas_call(
        functools.partial(_attn_kernel, head_dim=head_dim, lam_init=lam_init),
        grid=(bsz, heads, seq // tq),
        in_specs=[
            pl.BlockSpec((None, tq, v_dim), lambda b, h, i: (b, i, h)),
            pl.BlockSpec((None, v_dim, seq), lambda b, h, i: (b, h, 0)),
            pl.BlockSpec((None, seq, v_dim), lambda b, h, i: (b, 0, h)),
            *[vec(a) for a in lams],
            vec(subln_g[None]),
        ],
        out_specs=pl.BlockSpec((None, tq, v_dim), lambda b, h, i: (b, i, h)),
        out_shape=jax.ShapeDtypeStruct((bsz, seq, width), BF16),
        compiler_params=_cparams("parallel", "parallel", "parallel"),
        name="diff_attention",
    )(q, kt, v, *lams, subln_g[None])


def _merge_kernel(x_ref, yh_ref, ya_ref, gh_ref, ga_ref, wuh_ref, wua_ref, wo_ref, g_ref, wr_ref, br_ref,
                  xo_ref, n_ref, aff_ref):
    m = x_ref.shape[0] // MERGE_ROW_SPLIT
    for j in range(MERGE_ROW_SPLIT):
        r = pl.ds(j * m, m)
        mh = jnp.dot(yh_ref[r, :].astype(BF16), wuh_ref[...], preferred_element_type=F32)
        ma = jnp.dot(ya_ref[r, :].astype(BF16), wua_ref[...], preferred_element_type=F32)
        merged = jax.nn.sigmoid(gh_ref[r, :]) * mh + jax.nn.sigmoid(ga_ref[r, :]) * ma
        x = x_ref[r, :] + jnp.dot(merged.astype(BF16), wo_ref[...], preferred_element_type=F32)
        xo_ref[r, :] = x
        n = x * lax.rsqrt(jnp.mean(x * x, axis=-1, keepdims=True) + NORM_EPS) * g_ref[...]
        n_ref[r, :] = n.astype(BF16)
        logits = lax.dot_general(wr_ref[...], n, (((1,), (1,)), ((), ())), preferred_element_type=F32,
                                 precision=lax.Precision.HIGHEST) + br_ref[...]
        e = jnp.exp(logits - jnp.max(logits, axis=0, keepdims=True))
        aff_ref[:, pl.ds(j * m, m)] = e / jnp.sum(e, axis=0, keepdims=True)


def _merge(x2, yh2, ya2, p2, col_gate, wuh, wua, wo, g, wr_t, br, bsz, seq, tm=512):
    t, d = x2.shape
    c = yh2.shape[1]
    e = wr_t.shape[0]
    jg = col_gate // d
    ns = seq // tm
    const = lambda a: pl.BlockSpec(a.shape, lambda i: (0,) * a.ndim)
    return pl.pallas_call(
        _merge_kernel,
        grid=(t // tm,),
        in_specs=[
            pl.BlockSpec((tm, d), lambda i: (i, 0)),
            pl.BlockSpec((tm, c), lambda i: (i, 0)),
            pl.BlockSpec((tm, ya2.shape[1]), lambda i: (i, 0)),
            pl.BlockSpec((tm, d), lambda i: (i, jg)),
            pl.BlockSpec((tm, d), lambda i: (i, jg + 1)),
            const(wuh), const(wua), const(wo), const(g), const(wr_t), const(br),
        ],
        out_specs=[
            pl.BlockSpec((tm, d), lambda i: (i, 0)),
            pl.BlockSpec((tm, d), lambda i: (i, 0)),
            pl.BlockSpec((None, e, tm), lambda i: (i // ns, 0, i % ns)),
        ],
        out_shape=[
            jax.ShapeDtypeStruct((t, d), F32),
            jax.ShapeDtypeStruct((t, d), BF16),
            jax.ShapeDtypeStruct((bsz, e, seq), F32),
        ],
        compiler_params=_cparams("parallel"),
        name="merge_router",
    )(x2, yh2, ya2, p2, p2, wuh, wua, wo, g, wr_t, br)


def _select_kernel(aff_ref, pos_ref, *, cap):
    a = aff_ref[...]
    rows, seq = a.shape
    as_f32 = lambda b: lax.bitcast_convert_type(b, F32)
    count = lambda m: jnp.sum(jnp.where(m, 1.0, 0.0), axis=-1, keepdims=True)
    thr = jnp.zeros((rows, 1), jnp.int32)
    for bit in range(30, -1, -1):
        cand = thr | (1 << bit)
        thr = jnp.where(count(a >= as_f32(cand)) >= cap, cand, thr)
    gt = a >= as_f32(thr + 1)
    eq = (a >= as_f32(thr)) & jnp.logical_not(gt)
    need = cap - count(gt)
    tri = jnp.where(lax.broadcasted_iota(jnp.int32, (LANES, LANES), 0)
                    <= lax.broadcasted_iota(jnp.int32, (LANES, LANES), 1), 1.0, 0.0).astype(BF16)

    def exclusive_cumsum(mask):
        ones = jnp.where(mask, 1.0, 0.0)
        carry = jnp.zeros((rows, 1), F32)
        chunks = []
        for j in range(seq // LANES):
            blk = ones[:, j * LANES:(j + 1) * LANES]
            incl = jnp.dot(blk.astype(BF16), tri, preferred_element_type=F32)
            chunks.append(incl - blk + carry)
            carry = carry + jnp.sum(blk, axis=-1, keepdims=True)
        return jnp.concatenate(chunks, axis=1)

    sel = gt | (eq & (exclusive_cumsum(eq) < need))
    pos_ref[...] = jnp.where(sel, exclusive_cumsum(sel), -1.0).astype(jnp.int32)


def _select(aff_rows, cap):
    return pl.pallas_call(
        functools.partial(_select_kernel, cap=cap),
        out_shape=jax.ShapeDtypeStruct(aff_rows.shape, jnp.int32),
        compiler_params=pltpu.CompilerParams(vmem_limit_bytes=VMEM_LIMIT_V7X),
        name="expert_select",
    )(aff_rows)


def _gather_kernel(starts_ref, pos_ref, aff_ref, n_ref, o_ref, gate_ref, *, win):
    b = pl.program_id(0)
    i = pl.program_id(1)
    e, cap, _ = o_ref.shape
    ts = n_ref.shape[0]

    @pl.when(i == 0)
    def _():
        o_ref[...] = jnp.zeros_like(o_ref)
        gate_ref[...] = jnp.zeros_like(gate_ref)

    slot0 = lax.broadcasted_iota(jnp.int32, (win, ts), 0)
    pos = pos_ref[...]
    aff = aff_ref[...]

    def first_row(x):
        lo = starts_ref[b, x, i]
        return pl.multiple_of(jnp.minimum((lo // BF16_ROWS) * BF16_ROWS, cap - win), BF16_ROWS)

    def add_gates(x, dst, match):
        picked = jnp.sum(jnp.where(match, aff[x:x + 1, :], 0.0), axis=-1, keepdims=True)
        gate_ref[x, dst, :] = gate_ref[x, dst, :] + picked

    for x0 in range(0, e, GATHER_GROUP):
        group = range(x0, min(x0 + GATHER_GROUP, e))
        onehots = []
        for x in group:
            base = first_row(x)
            match = (slot0 + base) == pos[x:x + 1, :]
            add_gates(x, pl.ds(base, win), match)
            onehots.append(match.astype(BF16))
        picked = jnp.dot(jnp.concatenate(onehots, axis=0), n_ref[...], preferred_element_type=F32)
        for k, x in enumerate(group):
            dst = pl.ds(first_row(x), win)
            o_ref[x, dst, :] = o_ref[x, dst, :] + picked[k * win:(k + 1) * win].astype(BF16)

    for x in range(e):
        base = first_row(x)
        hi = starts_ref[b, x, i + 1]

        def extra(k, carry, base=base, x=x):
            want = base + k * win
            row = pl.multiple_of(jnp.minimum(want, cap - win), BF16_ROWS)
            slots = slot0 + row
            match = jnp.logical_and(slots == pos[x:x + 1, :], slots >= want)
            dst = pl.ds(row, win)
            add_gates(x, dst, match)
            o_ref[x, dst, :] = o_ref[x, dst, :] + jnp.dot(match.astype(BF16), n_ref[...],
                                                          preferred_element_type=F32).astype(BF16)
            return carry

        windows = (jnp.maximum(hi - base, 1) + win - 1) // win
        lax.fori_loop(1, windows, extra, 0)


def _gather(pos, aff, starts, n3, cap, ts=COMBINE_TILE, win=128):
    bsz, e, seq = pos.shape
    d = n3.shape[-1]
    assert cap % BF16_ROWS == 0 and win % BF16_ROWS == 0 and win <= cap
    tile = pl.BlockSpec((None, e, ts), lambda b, i, st: (b, 0, i))
    grid_spec = pltpu.PrefetchScalarGridSpec(
        num_scalar_prefetch=1,
        grid=(bsz, seq // ts),
        in_specs=[tile, tile, pl.BlockSpec((None, ts, d), lambda b, i, st: (b, i, 0))],
        out_specs=[
            pl.BlockSpec((None, e, cap, d), lambda b, i, st: (b, 0, 0, 0)),
            pl.BlockSpec((None, e, cap, 1), lambda b, i, st: (b, 0, 0, 0)),
        ],
    )
    return pl.pallas_call(
        functools.partial(_gather_kernel, win=win),
        grid_spec=grid_spec,
        out_shape=[
            jax.ShapeDtypeStruct((bsz, e, cap, d), BF16),
            jax.ShapeDtypeStruct((bsz, e, cap, 1), F32),
        ],
        compiler_params=_cparams("parallel", "arbitrary"),
        name="expert_gather",
    )(starts, pos, aff, n3)


def _expert_kernel(x_ref, gate_ref, wg_ref, wu_ref, wd_ref, o_ref, acc_sc):
    s = pl.program_id(1)
    bg = pl.program_id(2)
    rows, cap, d = x_ref.shape

    @pl.when(s == 0)
    def _():
        acc_sc[bg] = jnp.zeros((rows * cap, d), F32)

    x = x_ref[...].reshape(rows * cap, d)
    wg = wg_ref[...].astype(BF16)
    wu = wu_ref[...].astype(BF16)
    wd = wd_ref[...].astype(BF16)
    m = rows * cap // EXPERT_ROW_SPLIT
    parts = []
    for j in range(EXPERT_ROW_SPLIT):
        xj = x[j * m:(j + 1) * m]
        parts.append((jnp.dot(xj, wg, preferred_element_type=F32), jnp.dot(xj, wu, preferred_element_type=F32)))
    ys = []
    for g, u in parts:
        h = (g * jax.nn.sigmoid(g) * u).astype(BF16)
        ys.append(jnp.dot(h, wd, preferred_element_type=F32))
    total = acc_sc[bg] + jnp.concatenate(ys, axis=0)
    acc_sc[bg] = total
    o_ref[...] = (total * gate_ref[...].reshape(rows * cap, 1)).astype(BF16).reshape(rows, cap, d)


def _experts(xg, gate, wg4, wu4, wd4, layer, f_slices=2, rows=1):
    bsz, e, cap, d = xg.shape
    f = wg4.shape[-1]
    assert f_slices >= 2 and f % f_slices == 0 and bsz % rows == 0
    fs = f // f_slices
    last = f_slices - 1
    groups = bsz // rows
    out_idx = lambda x, s, bg: (jnp.where(s == last, bg, 0), x, 0, 0)
    return pl.pallas_call(
        _expert_kernel,
        grid=(e, f_slices, groups),
        in_specs=[
            pl.BlockSpec((rows, None, cap, d), lambda x, s, bg: (bg, x, 0, 0)),
            pl.BlockSpec((rows, None, cap, 1), lambda x, s, bg: (bg, x, 0, 0)),
            pl.BlockSpec((None, None, d, fs), lambda x, s, bg: (layer, x, 0, s)),
            pl.BlockSpec((None, None, d, fs), lambda x, s, bg: (layer, x, 0, s)),
            pl.BlockSpec((None, None, fs, d), lambda x, s, bg: (layer, x, s, 0)),
        ],
        out_specs=pl.BlockSpec((rows, None, cap, d), out_idx),
        out_shape=jax.ShapeDtypeStruct((bsz, e, cap, d), BF16),
        scratch_shapes=[pltpu.VMEM((groups, rows * cap, d), F32)],
        compiler_params=_cparams("arbitrary", "arbitrary", "arbitrary"),
        name="expert_ffn",
    )(xg, gate, wg4, wu4, wd4)


def _combine_kernel(starts_ref, x_ref, pos_ref, ye_ref, g_ref, o_ref, stage_sc, *, final, win):
    b = pl.program_id(0)
    i = pl.program_id(1)
    ts = x_ref.shape[0]
    e, cap, _ = ye_ref.shape
    lane = lax.broadcasted_iota(jnp.int32, (ts, win), 1)
    pos = pos_ref[...]

    def first_row(x):
        lo = starts_ref[b, x, i]
        return pl.multiple_of(jnp.minimum((lo // BF16_ROWS) * BF16_ROWS, cap - win), BF16_ROWS)

    onehots = []
    for x in range(e):
        base = first_row(x)
        stage_sc[pl.ds(x * win, win), :] = ye_ref[x, pl.ds(base, win), :]
        onehots.append((pos[:, x:x + 1] - base == lane).astype(BF16))
    o_ref[...] = x_ref[...] + jnp.dot(jnp.concatenate(onehots, axis=1), stage_sc[...],
                                      preferred_element_type=F32)

    for x in range(e):
        base = first_row(x)
        hi = starts_ref[b, x, i + 1]
        col = pos[:, x:x + 1]

        def extra(k, carry, base=base, col=col, x=x):
            want = base + k * win
            row = pl.multiple_of(jnp.minimum(want, cap - win), BF16_ROWS)
            onehot = jnp.logical_and(col - row == lane, col >= want).astype(BF16)
            o_ref[...] += jnp.dot(onehot, ye_ref[x, pl.ds(row, win), :], preferred_element_type=F32)
            return carry

        windows = (jnp.maximum(hi - base, 1) + win - 1) // win
        lax.fori_loop(1, windows, extra, 0)

    if final:
        acc = o_ref[...]
        o_ref[...] = acc * lax.rsqrt(jnp.mean(acc * acc, axis=-1, keepdims=True) + NORM_EPS) * g_ref[...]


def _combine(x3, pos_t, starts, ye, g, final, ts=512, win=128):
    bsz, seq, d = x3.shape
    e, cap = ye.shape[1], ye.shape[2]
    assert cap % BF16_ROWS == 0 and win % BF16_ROWS == 0 and win <= cap
    grid_spec = pltpu.PrefetchScalarGridSpec(
        num_scalar_prefetch=1,
        grid=(bsz, seq // ts),
        in_specs=[
            pl.BlockSpec((None, ts, d), lambda b, i, st: (b, i, 0)),
            pl.BlockSpec((None, ts, e), lambda b, i, st: (b, i, 0)),
            _single((None, e, cap, d), lambda b, i, st: (b, 0, 0, 0)),
            pl.BlockSpec((1, d), lambda b, i, st: (0, 0)),
        ],
        out_specs=pl.BlockSpec((None, ts, d), lambda b, i, st: (b, i, 0)),
        scratch_shapes=[pltpu.VMEM((e * win, d), BF16)],
    )
    return pl.pallas_call(
        functools.partial(_combine_kernel, final=final, win=win),
        grid_spec=grid_spec,
        out_shape=jax.ShapeDtypeStruct((bsz, seq, d), F32),
        compiler_params=_cparams("parallel", "parallel"),
        name="expert_combine",
    )(starts, x3, pos_t, ye, g)


def kernel(x, norm_mix, w_in, b_in, hy_conv_w, hy_conv_b, hy_ffn_w1, hy_ffn_b1, hy_ffn_f1, hy_ffn_w2, hy_ffn_b2, hy_ffn_f2, hy_ffn_w3, hy_bias, lambda_q1, lambda_k1, lambda_q2, lambda_k2, subln_g, w_up_hyena, w_up_attn, w_out, norm_ffn, w_router, b_router, w_e_gate, w_e_up, w_e_down, norm_final):
    bsz, seq, d = x.shape
    depth = w_in.shape[0]
    orders, c = hy_bias.shape[1], hy_bias.shape[2]
    head_dim = lambda_q1.shape[1]
    v_width = w_up_attn.shape[1]
    qk_width = v_width
    e = w_router.shape[2]
    cap = EC_FACTOR * seq // e
    col_q = (orders + 1) * c
    col_gate = col_q + 2 * qk_width + v_width
    assert orders == 2 and bsz % 2 == 0 and col_gate % d == 0

    consts = tuple(jnp.asarray(a, F32).astype(BF16) for a in _dft_constants(seq))

    xs = x.reshape(bsz * seq, d)
    out = None
    for l in range(depth):
        p2 = _inproj(xs, norm_mix[l][None], w_in[l].astype(BF16), b_in[l][None])
        p3 = p2.reshape(bsz, seq, -1)

        uc = _shortconv(p3, hy_conv_w[l], hy_conv_b[l][None], c)
        hfull = _filters(hy_ffn_w1[l], hy_ffn_b1[l], hy_ffn_f1[l], hy_ffn_w2[l], hy_ffn_b2[l], hy_ffn_f2[l],
                        hy_ffn_w3[l], hy_bias[l], seq)
        hspec = _spectrum(hfull, consts[4], consts[1])
        z = _longconv(uc, 0, uc, 1, hspec, 0, consts, natural_out=False)
        y_hy = _longconv(z, 0, uc, 2, hspec, 1, consts, natural_out=True)

        q_r, k_t, v_b = _rope(p2, bsz, seq, qk_width, v_width, head_dim, col_q)
        lam_init = 0.8 - 0.6 * math.exp(-0.3 * l)
        y_da = _attention(q_r.reshape(bsz, seq, qk_width), k_t, v_b.reshape(bsz, seq, v_width),
                          lambda_q1[l], lambda_k1[l], lambda_q2[l], lambda_k2[l], subln_g[l], head_dim, lam_init)

        xs, n2, aff = _merge(xs, y_hy.reshape(bsz * seq, c), y_da.reshape(bsz * seq, v_width), p2, col_gate,
                             w_up_hyena[l].astype(BF16), w_up_attn[l].astype(BF16), w_out[l].astype(BF16),
                             norm_ffn[l][None], w_router[l].T, b_router[l][:, None], bsz, seq)

        pos = _select(aff.reshape(bsz * e, seq), cap).reshape(bsz, e, seq)
        tiles = seq // COMBINE_TILE
        counts = jnp.sum((pos >= 0).reshape(bsz, e, tiles, COMBINE_TILE), axis=-1, dtype=jnp.int32)
        starts = jnp.concatenate([jnp.zeros((bsz, e, 1), jnp.int32), jnp.cumsum(counts, axis=-1)], axis=-1)
        xg, gate = _gather(pos, aff, starts, n2.reshape(bsz, seq, d), cap)
        ye = _experts(xg, gate, w_e_gate, w_e_up, w_e_down, l)
        final = l == depth - 1
        out = _combine(xs.reshape(bsz, seq, d), pos.transpose(0, 2, 1), starts, ye, norm_final[None], final,
                       ts=COMBINE_TILE)
        xs = out.reshape(bsz * seq, d)
    return out
```

```python
import functools
import math

import numpy as np
import jax
import jax.numpy as jnp
from jax import lax
from jax.experimental import pallas as pl
from jax.experimental.pallas import tpu as pltpu

F32 = jnp.float32
BF16 = jnp.bfloat16

NORM_EPS = 1e-6
SUBLN_EPS = 1e-5
ROPE_THETA = 10000.0
HY_FAST_DECAY = 0.3
HY_SLOW_DECAY = 1.5
HY_TARGET = 1e-2
EC_FACTOR = 2

VMEM_LIMIT_V7X = 56 * 1024 * 1024
LANES = 128
BF16_ROWS = 16
COMBINE_TILE = 512
GATHER_GROUP = 16
MERGE_ROW_SPLIT = 1
ATTN_SUB_ROWS = 128

DFT_N1 = 64
DFT_N2 = 128
SUBLANES = 8
A_PITCH = 2 * DFT_N1 + SUBLANES
V_PITCH = 2 * DFT_N2 + SUBLANES
H_PITCH = DFT_N2 + SUBLANES
UNROLL_STAGE_A = 64
UNROLL_STAGE_B = 16
STAGE_B_COLS = 2
UNROLL_SHORTCONV = 8


def _cparams(*sem):
    return pltpu.CompilerParams(dimension_semantics=sem, vmem_limit_bytes=VMEM_LIMIT_V7X)


def _single(block_shape, index_map):
    return pl.BlockSpec(block_shape, index_map, pipeline_mode=pl.Buffered(1))


def _inproj_kernel(x_ref, g_ref, w_ref, b_ref, o_ref, n_sc):
    @pl.when(pl.program_id(1) == 0)
    def _():
        x = x_ref[...]
        n = x * lax.rsqrt(jnp.mean(x * x, axis=-1, keepdims=True) + NORM_EPS) * g_ref[...]
        n_sc[...] = n.astype(BF16)

    o_ref[...] = jnp.dot(n_sc[...], w_ref[...], preferred_element_type=F32) + b_ref[...]


def _inproj(x2, g, w_bf, b, tm=2048, tn=1024):
    t, d = x2.shape
    width = w_bf.shape[1]
    return pl.pallas_call(
        _inproj_kernel,
        grid=(t // tm, width // tn),
        in_specs=[
            pl.BlockSpec((tm, d), lambda i, j: (i, 0)),
            pl.BlockSpec((1, d), lambda i, j: (0, 0)),
            pl.BlockSpec((d, tn), lambda i, j: (0, j)),
            pl.BlockSpec((1, tn), lambda i, j: (0, j)),
        ],
        out_specs=pl.BlockSpec((tm, tn), lambda i, j: (i, j)),
        out_shape=jax.ShapeDtypeStruct((t, width), F32),
        scratch_shapes=[pltpu.VMEM((tm, d), BF16)],
        compiler_params=_cparams("parallel", "arbitrary"),
        name="inproj",
    )(x2, g, w_bf, b)


def _shortconv_kernel(u_ref, w_ref, b_ref, o_ref, pad_sc, *, seq, n2, nh):
    zero_row = jnp.zeros((1, LANES), F32)
    for q in range(nh):
        base = q * H_PITCH + SUBLANES
        pad_sc[pl.ds(base, n2), :] = u_ref[pl.ds(q * n2, n2), :]
        pad_sc[pl.ds(base - 1, 1), :] = u_ref[pl.ds(q * n2 - 1, 1), :] if q > 0 else zero_row
        pad_sc[pl.ds(base + n2, 1), :] = u_ref[pl.ds((q + 1) * n2, 1), :] if q < nh - 1 else zero_row
    w = w_ref[...]
    bias = b_ref[...]

    column = lambda t: pad_sc[pl.ds(SUBLANES - 1 + t, nh, stride=H_PITCH), :]

    def body(r, taps):
        prev, cur = taps
        nxt = column(r + 2)
        o_ref[pl.ds(pl.multiple_of(r * nh, nh), nh), :] = prev * w[0:1] + cur * w[1:2] + nxt * w[2:3] + bias
        return cur, nxt

    lax.fori_loop(0, n2, body, (column(0), column(1)), unroll=UNROLL_SHORTCONV)


def _shortconv(p3, conv_w, conv_b, c):
    bsz, seq, _ = p3.shape
    parts = conv_w.shape[1] // c
    cb_per_part = c // LANES
    nh = seq // DFT_N2
    return pl.pallas_call(
        functools.partial(_shortconv_kernel, seq=seq, n2=DFT_N2, nh=nh),
        grid=(bsz, parts * cb_per_part),
        in_specs=[
            pl.BlockSpec((None, seq, LANES), lambda b, j: (b, 0, j)),
            pl.BlockSpec((3, LANES), lambda b, j: (0, j)),
            pl.BlockSpec((1, LANES), lambda b, j: (0, j)),
        ],
        out_specs=pl.BlockSpec((None, None, None, seq, LANES),
                               lambda b, j: (j // cb_per_part, b, j % cb_per_part, 0, 0)),
        out_shape=jax.ShapeDtypeStruct((parts, bsz, cb_per_part, seq, LANES), F32),
        scratch_shapes=[pltpu.VMEM((nh * H_PITCH + SUBLANES, LANES), F32)],
        compiler_params=_cparams("parallel", "parallel"),
        name="shortconv",
    )(p3, conv_w, conv_b)


def _filter_kernel(w1t_ref, w1c_ref, w1s_ref, b1_ref, f1_ref, w2_ref, b2_ref, f2_ref, w3_ref, bias_ref, o_ref,
                   *, seq, tl, c, bands):
    hi = lax.Precision.HIGHEST
    lag = lambda idx: jnp.where(idx < seq, idx, 2 * seq - idx).astype(F32)
    row = lax.broadcasted_iota(jnp.int32, (tl, 1), 0) + pl.program_id(0) * tl
    t = lag(row) / (seq - 1.0)
    pos = lag(lax.broadcasted_iota(jnp.int32, (1, tl), 1) + pl.program_id(0) * tl)
    t_l = pos / (seq - 1.0)
    w_l = (2.0 * math.pi) * pos / float(seq)
    band = lax.broadcasted_iota(jnp.int32, (bands, 1), 0).astype(F32)
    fr = 1e-4 + band * ((bands - 1 - 1e-4) / (bands - 1))
    ang = fr * w_l
    pre = (w1t_ref[...] * t_l
           + jnp.dot(w1c_ref[...], jnp.cos(ang), preferred_element_type=F32, precision=hi)
           - jnp.dot(w1s_ref[...], jnp.sin(ang), preferred_element_type=F32, precision=hi)
           + b1_ref[...])
    h = jnp.sin(f1_ref[...] * pre)
    h = jnp.sin(f2_ref[...] * (jnp.dot(w2_ref[...], h, preferred_element_type=F32, precision=hi) + b2_ref[...]))
    h = jnp.dot(h.T, w3_ref[...], preferred_element_type=F32, precision=hi)
    min_decay = math.log(HY_TARGET) / HY_FAST_DECAY
    max_decay = math.log(HY_TARGET) / HY_SLOW_DECAY
    ch = lax.broadcasted_iota(jnp.int32, (1, c), 1).astype(F32)
    deltas = jnp.abs(min_decay + ch * ((max_decay - min_decay) / (c - 1)))
    decay = jnp.exp(-t * deltas)
    orders = o_ref.shape[0]
    for o in range(orders):
        taps = h[:, o * c:(o + 1) * c] * decay
        taps = jnp.where(row == 0, taps + bias_ref[o:o + 1, :], taps)
        taps = jnp.where(row == seq, 0.0, taps)
        for blk in range(tl // DFT_N2):
            for cb in range(c // LANES):
                o_ref[o, cb, pl.ds(blk * H_PITCH, DFT_N2), :] = taps[blk * DFT_N2:(blk + 1) * DFT_N2,
                                                                   cb * LANES:(cb + 1) * LANES]
                o_ref[o, cb, pl.ds(blk * H_PITCH + DFT_N2, H_PITCH - DFT_N2), :] = jnp.zeros(
                    (H_PITCH - DFT_N2, LANES), F32)


def _filters(w1, b1, f1, w2, b2, f2, w3, bias, seq, tl=512):
    emb, ffn = w1.shape
    bands = (emb - 1) // 2
    orders, c = bias.shape
    tiles = seq // tl
    w3_dir = w3.reshape(ffn, orders, 2, c).transpose(2, 0, 1, 3).reshape(2, ffn, orders * c)
    full = lambda a: pl.BlockSpec(a.shape, lambda i: (0,) * a.ndim)
    col = lambda a: a[:, None]
    args = (w1[0:1].T, w1[1:1 + bands].T, w1[1 + bands:].T, col(b1), col(f1), w2.T, col(b2), col(f2), w3_dir, bias)
    in_specs = [full(a) for a in args]
    in_specs[8] = pl.BlockSpec((None, ffn, orders * c), lambda i: (i // tiles, 0, 0))
    return pl.pallas_call(
        functools.partial(_filter_kernel, seq=seq, tl=tl, c=c, bands=bands),
        grid=(2 * tiles,),
        in_specs=in_specs,
        out_specs=pl.BlockSpec((orders, c // LANES, tl // DFT_N2 * H_PITCH, LANES), lambda i: (0, 0, i, 0)),
        out_shape=jax.ShapeDtypeStruct((orders, c // LANES, 2 * seq // DFT_N2 * H_PITCH, LANES), F32),
        compiler_params=_cparams("parallel"),
        name="hyena_filters",
    )(*args)


def _stack_complex(m):
    return np.block([[m.real, -m.imag], [m.imag, m.real]])


@functools.lru_cache(maxsize=None)
def _dft_constants(seq):
    n1, n2 = DFT_N1, DFT_N2
    n = n1 * n2
    assert n == 2 * seq
    nh = seq // n2
    k1 = np.arange(n1)[:, None].astype(np.float64)
    q = np.arange(nh)[None, :].astype(np.float64)
    qf = np.arange(n1)[None, :].astype(np.float64)
    ma = np.empty((n2, 2 * n1, 2 * nh), np.float64)
    mai = np.empty((n2, 2 * nh, 2 * n1), np.float64)
    maf = np.empty((n2, 2 * n1, n1), np.float64)
    for r in range(n2):
        e = np.exp(-2j * np.pi * (q * k1 / n1 + r * k1 / n))
        ma[r] = _stack_complex(e)
        mai[r] = _stack_complex(np.conj(e).T / n)
        ef = np.exp(-2j * np.pi * (qf * k1 / n1 + r * k1 / n))
        maf[r] = np.concatenate([ef.real, ef.imag], axis=0)
    kk = np.arange(n2)[:, None].astype(np.float64)
    rr = np.arange(n2)[None, :].astype(np.float64)
    f = np.exp(-2j * np.pi * kk * rr / n2)
    mb = _stack_complex(f)
    mbi = _stack_complex(np.conj(f).T)
    return tuple(np.asarray(a, np.float32) for a in (ma, mb, mbi, mai, maf))


def _stage_a(load_x, ma_ref, a_sc):
    def body(r, carry):
        a_sc[pl.ds(pl.multiple_of(r * A_PITCH, SUBLANES), 2 * DFT_N1), :] = jnp.dot(
            ma_ref[r], load_x(r), preferred_element_type=F32)
        return carry

    lax.fori_loop(0, DFT_N2, body, 0, unroll=UNROLL_STAGE_A)


def _load_a_columns(a_sc, k1):
    cols = []
    for j in range(STAGE_B_COLS):
        ar = a_sc[pl.ds(k1 + j, DFT_N2, stride=A_PITCH), :]
        ai = a_sc[pl.ds(DFT_N1 + k1 + j, DFT_N2, stride=A_PITCH), :]
        cols.append(jnp.concatenate([ar, ai], axis=0))
    return jnp.concatenate(cols, axis=1).astype(BF16)


def _spectrum_kernel(h_ref, maf_ref, mb_ref, o_ref, a_sc):
    rows = 2 * DFT_N2

    def load_x(r):
        return h_ref[pl.ds(r, DFT_N1, stride=H_PITCH), :].astype(BF16)

    _stage_a(load_x, maf_ref, a_sc)

    cb = o_ref.shape[-1]

    def body(kp, carry):
        k1 = kp * STAGE_B_COLS
        z = jnp.dot(mb_ref[...], _load_a_columns(a_sc, k1), preferred_element_type=F32)
        for j in range(STAGE_B_COLS):
            o_ref[pl.ds(pl.multiple_of((k1 + j) * rows, rows), rows), :] = z[:, j * cb:(j + 1) * cb]
        return carry

    lax.fori_loop(0, DFT_N1 // STAGE_B_COLS, body, 0, unroll=UNROLL_STAGE_B)


def _spectrum(hfull, maf, mb):
    orders, cbs, n, _ = hfull.shape
    rows = 2 * DFT_N1 * DFT_N2
    return pl.pallas_call(
        _spectrum_kernel,
        grid=(orders, cbs),
        in_specs=[
            pl.BlockSpec((None, None, n, LANES), lambda o, j: (o, j, 0, 0)),
            _single(maf.shape, lambda o, j: (0, 0, 0)),
            _single(mb.shape, lambda o, j: (0, 0)),
        ],
        out_specs=pl.BlockSpec((None, None, rows, LANES), lambda o, j: (o, j, 0, 0)),
        out_shape=jax.ShapeDtypeStruct((orders, cbs, rows, LANES), F32),
        scratch_shapes=[pltpu.VMEM((DFT_N2 * A_PITCH, LANES), F32)],
        compiler_params=_cparams("parallel", "parallel"),
        name="hyena_spectrum",
    )(hfull, maf, mb)


def _longconv_kernel(z_ref, gate_ref, h_ref, ma_ref, mb_ref, mbi_ref, mai_ref, o_ref, a_sc, v_sc,
                     *, nh, natural_out):
    def load_x(r):
        src = pl.ds(pl.multiple_of(r * nh, nh), nh)
        return jnp.concatenate([z_ref[0, src, :], z_ref[1, src, :]], axis=0).astype(BF16)

    _stage_a(load_x, ma_ref, a_sc)

    rows = 2 * DFT_N2

    cb = o_ref.shape[-1]

    def freq(kp, carry):
        k1 = kp * STAGE_B_COLS
        zf = jnp.dot(mb_ref[...], _load_a_columns(a_sc, k1), preferred_element_type=F32)
        ys = []
        for j in range(STAGE_B_COLS):
            base = pl.multiple_of((k1 + j) * rows, rows)
            hr = h_ref[pl.ds(base, DFT_N2), :]
            hi = h_ref[pl.ds(base + DFT_N2, DFT_N2), :]
            zr = zf[:DFT_N2, j * cb:(j + 1) * cb]
            zi = zf[DFT_N2:, j * cb:(j + 1) * cb]
            ys.append(jnp.concatenate([zr * hr - zi * hi, zr * hi + zi * hr], axis=0))
        v = jnp.dot(mbi_ref[...], jnp.concatenate(ys, axis=1).astype(BF16), preferred_element_type=F32)
        for j in range(STAGE_B_COLS):
            v_sc[pl.ds(pl.multiple_of((k1 + j) * V_PITCH, SUBLANES), rows), :] = v[:, j * cb:(j + 1) * cb]
        return carry

    lax.fori_loop(0, DFT_N1 // STAGE_B_COLS, freq, 0, unroll=UNROLL_STAGE_B)

    def back(r, carry):
        vr = v_sc[pl.ds(r, DFT_N1, stride=V_PITCH), :]
        vi = v_sc[pl.ds(DFT_N2 + r, DFT_N1, stride=V_PITCH), :]
        y = jnp.dot(mai_ref[r], jnp.concatenate([vr, vi], axis=0).astype(BF16), preferred_element_type=F32)
        src = pl.ds(pl.multiple_of(r * nh, nh), nh)
        dst = pl.ds(r, nh, stride=DFT_N2) if natural_out else src
        o_ref[0, dst, :] = y[:nh] * gate_ref[0, src, :]
        o_ref[1, dst, :] = y[nh:] * gate_ref[1, src, :]
        return carry

    lax.fori_loop(0, DFT_N2, back, 0, unroll=UNROLL_STAGE_A)


def _longconv(z4, z_part, gate4, gate_part, h3, order, consts, natural_out):
    _, bsz, cbs, seq, _ = z4.shape
    ma, mb, mbi, mai = consts[:4]
    nh = seq // DFT_N2
    rows = 2 * DFT_N1 * DFT_N2
    pair = lambda part: (lambda j, p: (part, p, j, 0, 0))
    if natural_out:
        out_spec = pl.BlockSpec((None, 2, seq, LANES), lambda j, p: (0, p, 0, j))
        out_shape = jax.ShapeDtypeStruct((1, bsz, seq, cbs * LANES), F32)
    else:
        out_spec = pl.BlockSpec((None, 2, None, seq, LANES), pair(0))
        out_shape = jax.ShapeDtypeStruct((1, bsz, cbs, seq, LANES), F32)
    return pl.pallas_call(
        functools.partial(_longconv_kernel, nh=nh, natural_out=natural_out),
        grid=(cbs, bsz // 2),
        in_specs=[
            pl.BlockSpec((None, 2, None, seq, LANES), pair(z_part)),
            pl.BlockSpec((None, 2, None, seq, LANES), pair(gate_part)),
            _single((None, None, rows, LANES), lambda j, p: (order, j, 0, 0)),
            _single(ma.shape, lambda j, p: (0, 0, 0)),
            _single(mb.shape, lambda j, p: (0, 0)),
            _single(mbi.shape, lambda j, p: (0, 0)),
            _single(mai.shape, lambda j, p: (0, 0, 0)),
        ],
        out_specs=out_spec,
        out_shape=out_shape,
        scratch_shapes=[pltpu.VMEM((DFT_N2 * A_PITCH, LANES), F32), pltpu.VMEM((DFT_N1 * V_PITCH, LANES), F32)],
        compiler_params=_cparams("parallel", "parallel"),
        name="hyena_longconv",
    )(z4, gate4, h3, ma, mb, mbi, mai)


def _rope_kernel(q_ref, k_ref, v_ref, cos_ref, sin_ref, qo_ref, kt_ref, vo_ref, *, half, scale):
    cos = cos_ref[...]
    sin = sin_ref[...]
    lane = lax.broadcasted_iota(jnp.int32, (1, LANES), 1)
    first_half = (lane % (2 * half)) < half

    def rot(x):
        outs = []
        for j in range(x.shape[1] // LANES):
            xb = x[:, j * LANES:(j + 1) * LANES]
            partner = jnp.where(first_half, pltpu.roll(xb, LANES - half, axis=1), pltpu.roll(xb, half, axis=1))
            outs.append(xb * cos + partner * sin)
        return jnp.concatenate(outs, axis=1)

    qo_ref[...] = (rot(q_ref[...]) * scale).astype(BF16)
    kt_ref[...] = rot(k_ref[...]).T.astype(BF16)
    vo_ref[...] = v_ref[...].astype(BF16)


def _rope(p2, bsz, seq, qk_width, v_width, head_dim, col_q, tm=512):
    t = p2.shape[0]
    assert qk_width == v_width and col_q % qk_width == 0
    jq = col_q // qk_width
    half = head_dim // 2
    inv = ROPE_THETA ** (-jnp.arange(half, dtype=F32) * 2.0 / head_dim)
    ang = jnp.arange(seq, dtype=F32)[:, None] * inv[None, :]
    cos, sin = jnp.cos(ang), jnp.sin(ang)
    reps = LANES // head_dim
    cos_t = jnp.tile(jnp.concatenate([cos, cos], axis=1), (1, reps))
    sin_t = jnp.tile(jnp.concatenate([-sin, sin], axis=1), (1, reps))
    ns = seq // tm
    return pl.pallas_call(
        functools.partial(_rope_kernel, half=half, scale=head_dim ** -0.5 * math.log2(math.e)),
        grid=(t // tm,),
        in_specs=[
            pl.BlockSpec((tm, qk_width), lambda i: (i, jq)),
            pl.BlockSpec((tm, qk_width), lambda i: (i, jq + 1)),
            pl.BlockSpec((tm, v_width), lambda i: (i, jq + 2)),
            pl.BlockSpec((tm, LANES), lambda i: (i % ns, 0)),
            pl.BlockSpec((tm, LANES), lambda i: (i % ns, 0)),
        ],
        out_specs=[
            pl.BlockSpec((tm, qk_width), lambda i: (i, 0)),
            pl.BlockSpec((None, qk_width, tm), lambda i: (i // ns, 0, i % ns)),
            pl.BlockSpec((tm, v_width), lambda i: (i, 0)),
        ],
        out_shape=[
            jax.ShapeDtypeStruct((t, qk_width), BF16),
            jax.ShapeDtypeStruct((bsz, qk_width, seq), BF16),
            jax.ShapeDtypeStruct((t, v_width), BF16),
        ],
        compiler_params=_cparams("parallel"),
        name="rope",
    )(p2, p2, p2, cos_t, sin_t)


def _attn_kernel(q_ref, kt_ref, v_ref, lq1_ref, lk1_ref, lq2_ref, lk2_ref, g_ref, o_ref, *, head_dim, lam_init):
    lam = (jnp.exp(jnp.sum(lq1_ref[...] * lk1_ref[...], axis=-1, keepdims=True))
           - jnp.exp(jnp.sum(lq2_ref[...] * lk2_ref[...], axis=-1, keepdims=True)) + lam_init)
    lane = lax.broadcasted_iota(jnp.int32, (1, q_ref.shape[1]), 1)
    sub = ATTN_SUB_ROWS
    nsub = q_ref.shape[0] // sub

    def scores(j):
        q = q_ref[pl.ds(j * sub, sub), :]
        zero = jnp.zeros_like(q)
        qq = jnp.concatenate([jnp.where(lane < head_dim, q, zero), jnp.where(lane >= head_dim, q, zero)], axis=0)
        return jnp.dot(qq, kt_ref[...], preferred_element_type=F32)

    v = v_ref[...]
    vd = v.shape[1]
    v_aug = jnp.concatenate([v, jnp.where(lane == 0, 1.0, 0.0).astype(BF16) + jnp.zeros_like(v)], axis=1)

    def weights(s):
        return (jnp.exp2(s - jnp.max(s, axis=-1, keepdims=True)).astype(BF16),)

    def emit(j, e):
        r = jnp.dot(e, v_aug, preferred_element_type=F32)
        o = (r[:sub, :vd] / r[:sub, vd:vd + 1]) - lam * (r[sub:, :vd] / r[sub:, vd:vd + 1])
        o = o * lax.rsqrt(jnp.mean(o * o, axis=-1, keepdims=True) + SUBLN_EPS) * g_ref[...]
        o_ref[pl.ds(j * sub, sub), :] = (o * (1.0 - lam_init)).astype(o_ref.dtype)

    s_of, a_of = {}, {}
    for t in range(nsub + 2):
        if t < nsub:
            s_of[t] = scores(t)
        if 0 <= t - 1 < nsub:
            a_of[t - 1] = weights(s_of.pop(t - 1))
        if 0 <= t - 2 < nsub:
            emit(t - 2, *a_of.pop(t - 2))


def _attention(q, kt, v, lq1, lk1, lq2, lk2, subln_g, head_dim, lam_init, tq=512):
    bsz, seq, width = q.shape
    v_dim = subln_g.shape[-1]
    assert v_dim == 2 * head_dim == LANES
    heads = width // v_dim
    vec = lambda a: pl.BlockSpec((1, a.shape[-1]), lambda b, h, i: (0, 0))
    lams = [a[None] for a in (lq1, lk1, lq2, lk2)]
    return pl.pallas_call(
        functools.partial(_attn_kernel, head_dim=head_dim, lam_init=lam_init),
        grid=(bsz, heads, seq // tq),
        in_specs=[
            pl.BlockSpec((None, tq, v_dim), lambda b, h, i: (b, i, h)),
            pl.BlockSpec((None, v_dim, seq), lambda b, h, i: (b, h, 0)),
            pl.BlockSpec((None, seq, v_dim), lambda b, h, i: (b, 0, h)),
            *[vec(a) for a in lams],
            vec(subln_g[None]),
        ],
        out_specs=pl.BlockSpec((None, tq, v_dim), lambda b, h, i: (b, i, h)),
        out_shape=jax.ShapeDtypeStruct((bsz, seq, width), BF16),
        compiler_params=_cparams("parallel", "parallel", "parallel"),
        name="diff_attention",
    )(q, kt, v, *lams, subln_g[None])


def _merge_kernel(x_ref, yh_ref, ya_ref, gh_ref, ga_ref, wuh_ref, wua_ref, wo_ref, g_ref, wr_ref, br_ref,
                  xo_ref, n_ref, aff_ref):
    m = x_ref.shape[0] // MERGE_ROW_SPLIT
    for j in range(MERGE_ROW_SPLIT):
        r = pl.ds(j * m, m)
        mh = jnp.dot(yh_ref[r, :].astype(BF16), wuh_ref[...], preferred_element_type=F32)
        ma = jnp.dot(ya_ref[r, :].astype(BF16), wua_ref[...], preferred_element_type=F32)
        merged = jax.nn.sigmoid(gh_ref[r, :]) * mh + jax.nn.sigmoid(ga_ref[r, :]) * ma
        x = x_ref[r, :] + jnp.dot(merged.astype(BF16), wo_ref[...], preferred_element_type=F32)
        xo_ref[r, :] = x
        n = x * lax.rsqrt(jnp.mean(x * x, axis=-1, keepdims=True) + NORM_EPS) * g_ref[...]
        n_ref[r, :] = n.astype(BF16)
        logits = lax.dot_general(wr_ref[...], n, (((1,), (1,)), ((), ())), preferred_element_type=F32,
                                 precision=lax.Precision.HIGHEST) + br_ref[...]
        e = jnp.exp(logits - jnp.max(logits, axis=0, keepdims=True))
        aff_ref[:, pl.ds(j * m, m)] = e / jnp.sum(e, axis=0, keepdims=True)


def _merge(x2, yh2, ya2, p2, col_gate, wuh, wua, wo, g, wr_t, br, bsz, seq, tm=512):
    t, d = x2.shape
    c = yh2.shape[1]
    e = wr_t.shape[0]
    jg = col_gate // d
    ns = seq // tm
    const = lambda a: pl.BlockSpec(a.shape, lambda i: (0,) * a.ndim)
    return pl.pallas_call(
        _merge_kernel,
        grid=(t // tm,),
        in_specs=[
            pl.BlockSpec((tm, d), lambda i: (i, 0)),
            pl.BlockSpec((tm, c), lambda i: (i, 0)),
            pl.BlockSpec((tm, ya2.shape[1]), lambda i: (i, 0)),
            pl.BlockSpec((tm, d), lambda i: (i, jg)),
            pl.BlockSpec((tm, d), lambda i: (i, jg + 1)),
            const(wuh), const(wua), const(wo), const(g), const(wr_t), const(br),
        ],
        out_specs=[
            pl.BlockSpec((tm, d), lambda i: (i, 0)),
            pl.BlockSpec((tm, d), lambda i: (i, 0)),
            pl.BlockSpec((None, e, tm), lambda i: (i // ns, 0, i % ns)),
        ],
        out_shape=[
            jax.ShapeDtypeStruct((t, d), F32),
            jax.ShapeDtypeStruct((t, d), BF16),
            jax.ShapeDtypeStruct((bsz, e, seq), F32),
        ],
        compiler_params=_cparams("parallel"),
        name="merge_router",
    )(x2, yh2, ya2, p2, p2, wuh, wua, wo, g, wr_t, br)


def _select_kernel(aff_ref, pos_ref, *, cap):
    a = aff_ref[...]
    rows, seq = a.shape
    as_f32 = lambda b: lax.bitcast_convert_type(b, F32)
    count = lambda m: jnp.sum(jnp.where(m, 1.0, 0.0), axis=-1, keepdims=True)
    thr = jnp.zeros((rows, 1), jnp.int32)
    for bit in range(30, -1, -1):
        cand = thr | (1 << bit)
        thr = jnp.where(count(a >= as_f32(cand)) >= cap, cand, thr)
    gt = a >= as_f32(thr + 1)
    eq = (a >= as_f32(thr)) & jnp.logical_not(gt)
    need = cap - count(gt)
    tri = jnp.where(lax.broadcasted_iota(jnp.int32, (LANES, LANES), 0)
                    <= lax.broadcasted_iota(jnp.int32, (LANES, LANES), 1), 1.0, 0.0).astype(BF16)

    def exclusive_cumsum(mask):
        ones = jnp.where(mask, 1.0, 0.0)
        carry = jnp.zeros((rows, 1), F32)
        chunks = []
        for j in range(seq // LANES):
            blk = ones[:, j * LANES:(j + 1) * LANES]
            incl = jnp.dot(blk.astype(BF16), tri, preferred_element_type=F32)
            chunks.append(incl - blk + carry)
            carry = carry + jnp.sum(blk, axis=-1, keepdims=True)
        return jnp.concatenate(chunks, axis=1)

    sel = gt | (eq & (exclusive_cumsum(eq) < need))
    pos_ref[...] = jnp.where(sel, exclusive_cumsum(sel), -1.0).astype(jnp.int32)


def _select(aff_rows, cap):
    return pl.pallas_call(
        functools.partial(_select_kernel, cap=cap),
        out_shape=jax.ShapeDtypeStruct(aff_rows.shape, jnp.int32),
        compiler_params=pltpu.CompilerParams(vmem_limit_bytes=VMEM_LIMIT_V7X),
        name="expert_select",
    )(aff_rows)


def _gather_kernel(starts_ref, pos_ref, aff_ref, n_ref, o_ref, gate_ref, *, win):
    b = pl.program_id(0)
    i = pl.program_id(1)
    e, cap, _ = o_ref.shape
    ts = n_ref.shape[0]

    @pl.when(i == 0)
    def _():
        o_ref[...] = jnp.zeros_like(o_ref)
        gate_ref[...] = jnp.zeros_like(gate_ref)

    slot0 = lax.broadcasted_iota(jnp.int32, (win, ts), 0)
    pos = pos_ref[...]
    aff = aff_ref[...]

    def first_row(x):
        lo = starts_ref[b, x, i]
        return pl.multiple_of(jnp.minimum((lo // BF16_ROWS) * BF16_ROWS, cap - win), BF16_ROWS)

    def add_gates(x, dst, match):
        picked = jnp.sum(jnp.where(match, aff[x:x + 1, :], 0.0), axis=-1, keepdims=True)
        gate_ref[x, dst, :] = gate_ref[x, dst, :] + picked

    for x0 in range(0, e, GATHER_GROUP):
        group = range(x0, min(x0 + GATHER_GROUP, e))
        onehots = []
        for x in group:
            base = first_row(x)
            match = (slot0 + base) == pos[x:x + 1, :]
            add_gates(x, pl.ds(base, win), match)
            onehots.append(match.astype(BF16))
        picked = jnp.dot(jnp.concatenate(onehots, axis=0), n_ref[...], preferred_element_type=F32)
        for k, x in enumerate(group):
            dst = pl.ds(first_row(x), win)
            o_ref[x, dst, :] = o_ref[x, dst, :] + picked[k * win:(k + 1) * win].astype(BF16)

    for x in range(e):
        base = first_row(x)
        hi = starts_ref[b, x, i + 1]

        def extra(k, carry, base=base, x=x):
            want = base + k * win
            row = pl.multiple_of(jnp.minimum(want, cap - win), BF16_ROWS)
            slots = slot0 + row
            match = jnp.logical_and(slots == pos[x:x + 1, :], slots >= want)
            dst = pl.ds(row, win)
            add_gates(x, dst, match)
            o_ref[x, dst, :] = o_ref[x, dst, :] + jnp.dot(match.astype(BF16), n_ref[...],
                                                          preferred_element_type=F32).astype(BF16)
            return carry

        windows = (jnp.maximum(hi - base, 1) + win - 1) // win
        lax.fori_loop(1, windows, extra, 0)


def _gather(pos, aff, starts, n3, cap, ts=COMBINE_TILE, win=128):
    bsz, e, seq = pos.shape
    d = n3.shape[-1]
    assert cap % BF16_ROWS == 0 and win % BF16_ROWS == 0 and win <= cap
    tile = pl.BlockSpec((None, e, ts), lambda b, i, st: (b, 0, i))
    grid_spec = pltpu.PrefetchScalarGridSpec(
        num_scalar_prefetch=1,
        grid=(bsz, seq // ts),
        in_specs=[tile, tile, pl.BlockSpec((None, ts, d), lambda b, i, st: (b, i, 0))],
        out_specs=[
            pl.BlockSpec((None, e, cap, d), lambda b, i, st: (b, 0, 0, 0)),
            pl.BlockSpec((None, e, cap, 1), lambda b, i, st: (b, 0, 0, 0)),
        ],
    )
    return pl.pallas_call(
        functools.partial(_gather_kernel, win=win),
        grid_spec=grid_spec,
        out_shape=[
            jax.ShapeDtypeStruct((bsz, e, cap, d), BF16),
            jax.ShapeDtypeStruct((bsz, e, cap, 1), F32),
        ],
        compiler_params=_cparams("parallel", "arbitrary"),
        name="expert_gather",
    )(starts, pos, aff, n3)


def _expert_kernel(x_ref, gate_ref, wg_ref, wu_ref, wd_ref, o_ref, acc_sc):
    bsz = x_ref.shape[0]

    @pl.when(pl.program_id(1) == 0)
    def _():
        acc_sc[...] = jnp.zeros_like(acc_sc)

    wg = wg_ref[...].astype(BF16)
    wu = wu_ref[...].astype(BF16)
    wd = wd_ref[...].astype(BF16)
    for b in range(bsz):
        x = x_ref[b]
        g = jnp.dot(x, wg, preferred_element_type=F32)
        u = jnp.dot(x, wu, preferred_element_type=F32)
        h = (g * jax.nn.sigmoid(g) * u).astype(BF16)
        total = acc_sc[b] + jnp.dot(h, wd, preferred_element_type=F32)
        acc_sc[b] = total
        o_ref[b] = (total * gate_ref[b]).astype(BF16)


def _experts(xg, gate, wg4, wu4, wd4, layer, f_slices=4):
    bsz, e, cap, d = xg.shape
    f = wg4.shape[-1]
    assert f % f_slices == 0
    fs = f // f_slices
    return pl.pallas_call(
        _expert_kernel,
        grid=(e, f_slices),
        in_specs=[
            pl.BlockSpec((bsz, None, cap, d), lambda x, s: (0, x, 0, 0)),
            pl.BlockSpec((bsz, None, cap, 1), lambda x, s: (0, x, 0, 0)),
            pl.BlockSpec((None, None, d, fs), lambda x, s: (layer, x, 0, s)),
            pl.BlockSpec((None, None, d, fs), lambda x, s: (layer, x, 0, s)),
            pl.BlockSpec((None, None, fs, d), lambda x, s: (layer, x, s, 0)),
        ],
        out_specs=pl.BlockSpec((bsz, None, cap, d), lambda x, s: (0, x, 0, 0)),
        out_shape=jax.ShapeDtypeStruct((bsz, e, cap, d), BF16),
        scratch_shapes=[pltpu.VMEM((bsz, cap, d), F32)],
        compiler_params=_cparams("parallel", "arbitrary"),
        name="expert_ffn",
    )(xg, gate, wg4, wu4, wd4)


def _combine_kernel(starts_ref, x_ref, pos_ref, ye_ref, g_ref, o_ref, stage_sc, *, final, win):
    b = pl.program_id(0)
    i = pl.program_id(1)
    ts = x_ref.shape[0]
    e, cap, _ = ye_ref.shape
    lane = lax.broadcasted_iota(jnp.int32, (ts, win), 1)
    pos = pos_ref[...]

    def first_row(x):
        lo = starts_ref[b, x, i]
        return pl.multiple_of(jnp.minimum((lo // BF16_ROWS) * BF16_ROWS, cap - win), BF16_ROWS)

    onehots = []
    for x in range(e):
        base = first_row(x)
        stage_sc[pl.ds(x * win, win), :] = ye_ref[x, pl.ds(base, win), :]
        onehots.append((pos[:, x:x + 1] - base == lane).astype(BF16))
    o_ref[...] = x_ref[...] + jnp.dot(jnp.concatenate(onehots, axis=1), stage_sc[...],
                                      preferred_element_type=F32)

    for x in range(e):
        base = first_row(x)
        hi = starts_ref[b, x, i + 1]
        col = pos[:, x:x + 1]

        def extra(k, carry, base=base, col=col, x=x):
            want = base + k * win
            row = pl.multiple_of(jnp.minimum(want, cap - win), BF16_ROWS)
            onehot = jnp.logical_and(col - row == lane, col >= want).astype(BF16)
            o_ref[...] += jnp.dot(onehot, ye_ref[x, pl.ds(row, win), :], preferred_element_type=F32)
            return carry

        windows = (jnp.maximum(hi - base, 1) + win - 1) // win
        lax.fori_loop(1, windows, extra, 0)

    if final:
        acc = o_ref[...]
        o_ref[...] = acc * lax.rsqrt(jnp.mean(acc * acc, axis=-1, keepdims=True) + NORM_EPS) * g_ref[...]


def _combine(x3, pos_t, starts, ye, g, final, ts=512, win=128):
    bsz, seq, d = x3.shape
    e, cap = ye.shape[1], ye.shape[2]
    assert cap % BF16_ROWS == 0 and win % BF16_ROWS == 0 and win <= cap
    grid_spec = pltpu.PrefetchScalarGridSpec(
        num_scalar_prefetch=1,
        grid=(bsz, seq // ts),
        in_specs=[
            pl.BlockSpec((None, ts, d), lambda b, i, st: (b, i, 0)),
            pl.BlockSpec((None, ts, e), lambda b, i, st: (b, i, 0)),
            _single((None, e, cap, d), lambda b, i, st: (b, 0, 0, 0)),
            pl.BlockSpec((1, d), lambda b, i, st: (0, 0)),
        ],
        out_specs=pl.BlockSpec((None, ts, d), lambda b, i, st: (b, i, 0)),
        scratch_shapes=[pltpu.VMEM((e * win, d), BF16)],
    )
    return pl.pallas_call(
        functools.partial(_combine_kernel, final=final, win=win),
        grid_spec=grid_spec,
        out_shape=jax.ShapeDtypeStruct((bsz, seq, d), F32),
        compiler_params=_cparams("parallel", "parallel"),
        name="expert_combine",
    )(starts, x3, pos_t, ye, g)


def kernel(x, norm_mix, w_in, b_in, hy_conv_w, hy_conv_b, hy_ffn_w1, hy_ffn_b1, hy_ffn_f1, hy_ffn_w2, hy_ffn_b2, hy_ffn_f2, hy_ffn_w3, hy_bias, lambda_q1, lambda_k1, lambda_q2, lambda_k2, subln_g, w_up_hyena, w_up_attn, w_out, norm_ffn, w_router, b_router, w_e_gate, w_e_up, w_e_down, norm_final):
    bsz, seq, d = x.shape
    depth = w_in.shape[0]
    orders, c = hy_bias.shape[1], hy_bias.shape[2]
    head_dim = lambda_q1.shape[1]
    v_width = w_up_attn.shape[1]
    qk_width = v_width
    e = w_router.shape[2]
    cap = EC_FACTOR * seq // e
    col_q = (orders + 1) * c
    col_gate = col_q + 2 * qk_width + v_width
    assert orders == 2 and bsz % 2 == 0 and col_gate % d == 0

    consts = tuple(jnp.asarray(a, F32).astype(BF16) for a in _dft_constants(seq))

    xs = x.reshape(bsz * seq, d)
    out = None
    for l in range(depth):
        p2 = _inproj(xs, norm_mix[l][None], w_in[l].astype(BF16), b_in[l][None])
        p3 = p2.reshape(bsz, seq, -1)

        uc = _shortconv(p3, hy_conv_w[l], hy_conv_b[l][None], c)
        hfull = _filters(hy_ffn_w1[l], hy_ffn_b1[l], hy_ffn_f1[l], hy_ffn_w2[l], hy_ffn_b2[l], hy_ffn_f2[l],
                        hy_ffn_w3[l], hy_bias[l], seq)
        hspec = _spectrum(hfull, consts[4], consts[1])
        z = _longconv(uc, 0, uc, 1, hspec, 0, consts, natural_out=False)
        y_hy = _longconv(z, 0, uc, 2, hspec, 1, consts, natural_out=True)

        q_r, k_t, v_b = _rope(p2, bsz, seq, qk_width, v_width, head_dim, col_q)
        lam_init = 0.8 - 0.6 * math.exp(-0.3 * l)
        y_da = _attention(q_r.reshape(bsz, seq, qk_width), k_t, v_b.reshape(bsz, seq, v_width),
                          lambda_q1[l], lambda_k1[l], lambda_q2[l], lambda_k2[l], subln_g[l], head_dim, lam_init)

        xs, n2, aff = _merge(xs, y_hy.reshape(bsz * seq, c), y_da.reshape(bsz * seq, v_width), p2, col_gate,
                             w_up_hyena[l].astype(BF16), w_up_attn[l].astype(BF16), w_out[l].astype(BF16),
                             norm_ffn[l][None], w_router[l].T, b_router[l][:, None], bsz, seq)

        pos = _select(aff.reshape(bsz * e, seq), cap).reshape(bsz, e, seq)
        tiles = seq // COMBINE_TILE
        counts = jnp.sum((pos >= 0).reshape(bsz, e, tiles, COMBINE_TILE), axis=-1, dtype=jnp.int32)
        starts = jnp.concatenate([jnp.zeros((bsz, e, 1), jnp.int32), jnp.cumsum(counts, axis=-1)], axis=-1)
        xg, gate = _gather(pos, aff, starts, n2.reshape(bsz, seq, d), cap)
        ye = _experts(xg, gate, w_e_gate, w_e_up, w_e_down, l)
        final = l == depth - 1
        out = _combine(xs.reshape(bsz, seq, d), pos.transpose(0, 2, 1), starts, ye, norm_final[None], final,
                       ts=COMBINE_TILE)
        xs = out.reshape(bsz * seq, d)
    return out
```

```python
import functools
import math

import numpy as np
import jax
import jax.numpy as jnp
from jax import lax
from jax.experimental import pallas as pl
from jax.experimental.pallas import tpu as pltpu

F32 = jnp.float32
BF16 = jnp.bfloat16

NORM_EPS = 1e-6
SUBLN_EPS = 1e-5
ROPE_THETA = 10000.0
HY_FAST_DECAY = 0.3
HY_SLOW_DECAY = 1.5
HY_TARGET = 1e-2
EC_FACTOR = 2

VMEM_LIMIT_V7X = 56 * 1024 * 1024
LANES = 128
BF16_ROWS = 16
COMBINE_TILE = 512
GATHER_GROUP = 16
MERGE_ROW_SPLIT = 1
ATTN_SUB_ROWS = 128

DFT_N1 = 64
DFT_N2 = 128
SUBLANES = 8
A_PITCH = 2 * DFT_N1 + SUBLANES
V_PITCH = 2 * DFT_N2 + SUBLANES
H_PITCH = DFT_N2 + SUBLANES
UNROLL_STAGE_A = 64
UNROLL_STAGE_B = 16
STAGE_B_COLS = 2
UNROLL_SHORTCONV = 8


def _cparams(*sem):
    return pltpu.CompilerParams(dimension_semantics=sem, vmem_limit_bytes=VMEM_LIMIT_V7X)


def _single(block_shape, index_map):
    return pl.BlockSpec(block_shape, index_map, pipeline_mode=pl.Buffered(1))


def _inproj_kernel(x_ref, g_ref, w_ref, b_ref, o_ref, n_sc):
    @pl.when(pl.program_id(1) == 0)
    def _():
        x = x_ref[...]
        n = x * lax.rsqrt(jnp.mean(x * x, axis=-1, keepdims=True) + NORM_EPS) * g_ref[...]
        n_sc[...] = n.astype(BF16)

    o_ref[...] = jnp.dot(n_sc[...], w_ref[...], preferred_element_type=F32) + b_ref[...]


def _inproj(x2, g, w_bf, b, tm=2048, tn=1024):
    t, d = x2.shape
    width = w_bf.shape[1]
    return pl.pallas_call(
        _inproj_kernel,
        grid=(t // tm, width // tn),
        in_specs=[
            pl.BlockSpec((tm, d), lambda i, j: (i, 0)),
            pl.BlockSpec((1, d), lambda i, j: (0, 0)),
            pl.BlockSpec((d, tn), lambda i, j: (0, j)),
            pl.BlockSpec((1, tn), lambda i, j: (0, j)),
        ],
        out_specs=pl.BlockSpec((tm, tn), lambda i, j: (i, j)),
        out_shape=jax.ShapeDtypeStruct((t, width), F32),
        scratch_shapes=[pltpu.VMEM((tm, d), BF16)],
        compiler_params=_cparams("parallel", "arbitrary"),
        name="inproj",
    )(x2, g, w_bf, b)


def _shortconv_kernel(u_ref, w_ref, b_ref, o_ref, pad_sc, *, seq, n2, nh):
    zero_row = jnp.zeros((1, LANES), F32)
    for q in range(nh):
        base = q * H_PITCH + SUBLANES
        pad_sc[pl.ds(base, n2), :] = u_ref[pl.ds(q * n2, n2), :]
        pad_sc[pl.ds(base - 1, 1), :] = u_ref[pl.ds(q * n2 - 1, 1), :] if q > 0 else zero_row
        pad_sc[pl.ds(base + n2, 1), :] = u_ref[pl.ds((q + 1) * n2, 1), :] if q < nh - 1 else zero_row
    w = w_ref[...]
    bias = b_ref[...]

    column = lambda t: pad_sc[pl.ds(SUBLANES - 1 + t, nh, stride=H_PITCH), :]

    def body(r, taps):
        prev, cur = taps
        nxt = column(r + 2)
        o_ref[pl.ds(pl.multiple_of(r * nh, nh), nh), :] = prev * w[0:1] + cur * w[1:2] + nxt * w[2:3] + bias
        return cur, nxt

    lax.fori_loop(0, n2, body, (column(0), column(1)), unroll=UNROLL_SHORTCONV)


def _shortconv(p3, conv_w, conv_b, c):
    bsz, seq, _ = p3.shape
    parts = conv_w.shape[1] // c
    cb_per_part = c // LANES
    nh = seq // DFT_N2
    return pl.pallas_call(
        functools.partial(_shortconv_kernel, seq=seq, n2=DFT_N2, nh=nh),
        grid=(bsz, parts * cb_per_part),
        in_specs=[
            pl.BlockSpec((None, seq, LANES), lambda b, j: (b, 0, j)),
            pl.BlockSpec((3, LANES), lambda b, j: (0, j)),
            pl.BlockSpec((1, LANES), lambda b, j: (0, j)),
        ],
        out_specs=pl.BlockSpec((None, None, None, seq, LANES),
                               lambda b, j: (j // cb_per_part, b, j % cb_per_part, 0, 0)),
        out_shape=jax.ShapeDtypeStruct((parts, bsz, cb_per_part, seq, LANES), F32),
        scratch_shapes=[pltpu.VMEM((nh * H_PITCH + SUBLANES, LANES), F32)],
        compiler_params=_cparams("parallel", "parallel"),
        name="shortconv",
    )(p3, conv_w, conv_b)


def _filter_kernel(w1t_ref, w1c_ref, w1s_ref, b1_ref, f1_ref, w2_ref, b2_ref, f2_ref, w3_ref, bias_ref, o_ref,
                   *, seq, tl, c, bands):
    hi = lax.Precision.HIGHEST
    lag = lambda idx: jnp.where(idx < seq, idx, 2 * seq - idx).astype(F32)
    row = lax.broadcasted_iota(jnp.int32, (tl, 1), 0) + pl.program_id(0) * tl
    t = lag(row) / (seq - 1.0)
    pos = lag(lax.broadcasted_iota(jnp.int32, (1, tl), 1) + pl.program_id(0) * tl)
    t_l = pos / (seq - 1.0)
    w_l = (2.0 * math.pi) * pos / float(seq)
    band = lax.broadcasted_iota(jnp.int32, (bands, 1), 0).astype(F32)
    fr = 1e-4 + band * ((bands - 1 - 1e-4) / (bands - 1))
    ang = fr * w_l
    pre = (w1t_ref[...] * t_l
           + jnp.dot(w1c_ref[...], jnp.cos(ang), preferred_element_type=F32, precision=hi)
           - jnp.dot(w1s_ref[...], jnp.sin(ang), preferred_element_type=F32, precision=hi)
           + b1_ref[...])
    h = jnp.sin(f1_ref[...] * pre)
    h = jnp.sin(f2_ref[...] * (jnp.dot(w2_ref[...], h, preferred_element_type=F32, precision=hi) + b2_ref[...]))
    h = jnp.dot(h.T, w3_ref[...], preferred_element_type=F32, precision=hi)
    min_decay = math.log(HY_TARGET) / HY_FAST_DECAY
    max_decay = math.log(HY_TARGET) / HY_SLOW_DECAY
    ch = lax.broadcasted_iota(jnp.int32, (1, c), 1).astype(F32)
    deltas = jnp.abs(min_decay + ch * ((max_decay - min_decay) / (c - 1)))
    decay = jnp.exp(-t * deltas)
    orders = o_ref.shape[0]
    for o in range(orders):
        taps = h[:, o * c:(o + 1) * c] * decay
        taps = jnp.where(row == 0, taps + bias_ref[o:o + 1, :], taps)
        taps = jnp.where(row == seq, 0.0, taps)
        for blk in range(tl // DFT_N2):
            for cb in range(c // LANES):
                o_ref[o, cb, pl.ds(blk * H_PITCH, DFT_N2), :] = taps[blk * DFT_N2:(blk + 1) * DFT_N2,
                                                                   cb * LANES:(cb + 1) * LANES]
                o_ref[o, cb, pl.ds(blk * H_PITCH + DFT_N2, H_PITCH - DFT_N2), :] = jnp.zeros(
                    (H_PITCH - DFT_N2, LANES), F32)


def _filters(w1, b1, f1, w2, b2, f2, w3, bias, seq, tl=512):
    emb, ffn = w1.shape
    bands = (emb - 1) // 2
    orders, c = bias.shape
    tiles = seq // tl
    w3_dir = w3.reshape(ffn, orders, 2, c).transpose(2, 0, 1, 3).reshape(2, ffn, orders * c)
    full = lambda a: pl.BlockSpec(a.shape, lambda i: (0,) * a.ndim)
    col = lambda a: a[:, None]
    args = (w1[0:1].T, w1[1:1 + bands].T, w1[1 + bands:].T, col(b1), col(f1), w2.T, col(b2), col(f2), w3_dir, bias)
    in_specs = [full(a) for a in args]
    in_specs[8] = pl.BlockSpec((None, ffn, orders * c), lambda i: (i // tiles, 0, 0))
    return pl.pallas_call(
        functools.partial(_filter_kernel, seq=seq, tl=tl, c=c, bands=bands),
        grid=(2 * tiles,),
        in_specs=in_specs,
        out_specs=pl.BlockSpec((orders, c // LANES, tl // DFT_N2 * H_PITCH, LANES), lambda i: (0, 0, i, 0)),
        out_shape=jax.ShapeDtypeStruct((orders, c // LANES, 2 * seq // DFT_N2 * H_PITCH, LANES), F32),
        compiler_params=_cparams("parallel"),
        name="hyena_filters",
    )(*args)


def _stack_complex(m):
    return np.block([[m.real, -m.imag], [m.imag, m.real]])


@functools.lru_cache(maxsize=None)
def _dft_constants(seq):
    n1, n2 = DFT_N1, DFT_N2
    n = n1 * n2
    assert n == 2 * seq
    nh = seq // n2
    k1 = np.arange(n1)[:, None].astype(np.float64)
    q = np.arange(nh)[None, :].astype(np.float64)
    qf = np.arange(n1)[None, :].astype(np.float64)
    ma = np.empty((n2, 2 * n1, 2 * nh), np.float64)
    mai = np.empty((n2, 2 * nh, 2 * n1), np.float64)
    maf = np.empty((n2, 2 * n1, n1), np.float64)
    for r in range(n2):
        e = np.exp(-2j * np.pi * (q * k1 / n1 + r * k1 / n))
        ma[r] = _stack_complex(e)
        mai[r] = _stack_complex(np.conj(e).T / n)
        ef = np.exp(-2j * np.pi * (qf * k1 / n1 + r * k1 / n))
        maf[r] = np.concatenate([ef.real, ef.imag], axis=0)
    kk = np.arange(n2)[:, None].astype(np.float64)
    rr = np.arange(n2)[None, :].astype(np.float64)
    f = np.exp(-2j * np.pi * kk * rr / n2)
    mb = _stack_complex(f)
    mbi = _stack_complex(np.conj(f).T)
    return tuple(np.asarray(a, np.float32) for a in (ma, mb, mbi, mai, maf))


def _stage_a(load_x, ma_ref, a_sc):
    def body(r, carry):
        a_sc[pl.ds(pl.multiple_of(r * A_PITCH, SUBLANES), 2 * DFT_N1), :] = jnp.dot(
            ma_ref[r], load_x(r), preferred_element_type=F32)
        return carry

    lax.fori_loop(0, DFT_N2, body, 0, unroll=UNROLL_STAGE_A)


def _load_a_columns(a_sc, k1):
    cols = []
    for j in range(STAGE_B_COLS):
        ar = a_sc[pl.ds(k1 + j, DFT_N2, stride=A_PITCH), :]
        ai = a_sc[pl.ds(DFT_N1 + k1 + j, DFT_N2, stride=A_PITCH), :]
        cols.append(jnp.concatenate([ar, ai], axis=0))
    return jnp.concatenate(cols, axis=1).astype(BF16)


def _spectrum_kernel(h_ref, maf_ref, mb_ref, o_ref, a_sc):
    rows = 2 * DFT_N2

    def load_x(r):
        return h_ref[pl.ds(r, DFT_N1, stride=H_PITCH), :].astype(BF16)

    _stage_a(load_x, maf_ref, a_sc)

    cb = o_ref.shape[-1]

    def body(kp, carry):
        k1 = kp * STAGE_B_COLS
        z = jnp.dot(mb_ref[...], _load_a_columns(a_sc, k1), preferred_element_type=F32)
        for j in range(STAGE_B_COLS):
            o_ref[pl.ds(pl.multiple_of((k1 + j) * rows, rows), rows), :] = z[:, j * cb:(j + 1) * cb]
        return carry

    lax.fori_loop(0, DFT_N1 // STAGE_B_COLS, body, 0, unroll=UNROLL_STAGE_B)


def _spectrum(hfull, maf, mb):
    orders, cbs, n, _ = hfull.shape
    rows = 2 * DFT_N1 * DFT_N2
    return pl.pallas_call(
        _spectrum_kernel,
        grid=(orders, cbs),
        in_specs=[
            pl.BlockSpec((None, None, n, LANES), lambda o, j: (o, j, 0, 0)),
            _single(maf.shape, lambda o, j: (0, 0, 0)),
            _single(mb.shape, lambda o, j: (0, 0)),
        ],
        out_specs=pl.BlockSpec((None, None, rows, LANES), lambda o, j: (o, j, 0, 0)),
        out_shape=jax.ShapeDtypeStruct((orders, cbs, rows, LANES), F32),
        scratch_shapes=[pltpu.VMEM((DFT_N2 * A_PITCH, LANES), F32)],
        compiler_params=_cparams("parallel", "parallel"),
        name="hyena_spectrum",
    )(hfull, maf, mb)


def _longconv_kernel(z_ref, gate_ref, h_ref, ma_ref, mb_ref, mbi_ref, mai_ref, o_ref, a_sc, v_sc,
                     *, nh, natural_out):
    def load_x(r):
        src = pl.ds(pl.multiple_of(r * nh, nh), nh)
        return jnp.concatenate([z_ref[0, src, :], z_ref[1, src, :]], axis=0).astype(BF16)

    _stage_a(load_x, ma_ref, a_sc)

    rows = 2 * DFT_N2

    cb = o_ref.shape[-1]

    def freq(kp, carry):
        k1 = kp * STAGE_B_COLS
        zf = jnp.dot(mb_ref[...], _load_a_columns(a_sc, k1), preferred_element_type=F32)
        ys = []
        for j in range(STAGE_B_COLS):
            base = pl.multiple_of((k1 + j) * rows, rows)
            hr = h_ref[pl.ds(base, DFT_N2), :]
            hi = h_ref[pl.ds(base + DFT_N2, DFT_N2), :]
            zr = zf[:DFT_N2, j * cb:(j + 1) * cb]
            zi = zf[DFT_N2:, j * cb:(j + 1) * cb]
            ys.append(jnp.concatenate([zr * hr - zi * hi, zr * hi + zi * hr], axis=0))
        v = jnp.dot(mbi_ref[...], jnp.concatenate(ys, axis=1).astype(BF16), preferred_element_type=F32)
        for j in range(STAGE_B_COLS):
            v_sc[pl.ds(pl.multiple_of((k1 + j) * V_PITCH, SUBLANES), rows), :] = v[:, j * cb:(j + 1) * cb]
        return carry

    lax.fori_loop(0, DFT_N1 // STAGE_B_COLS, freq, 0, unroll=UNROLL_STAGE_B)

    def back(r, carry):
        vr = v_sc[pl.ds(r, DFT_N1, stride=V_PITCH), :]
        vi = v_sc[pl.ds(DFT_N2 + r, DFT_N1, stride=V_PITCH), :]
        y = jnp.dot(mai_ref[r], jnp.concatenate([vr, vi], axis=0).astype(BF16), preferred_element_type=F32)
        src = pl.ds(pl.multiple_of(r * nh, nh), nh)
        dst = pl.ds(r, nh, stride=DFT_N2) if natural_out else src
        o_ref[0, dst, :] = y[:nh] * gate_ref[0, src, :]
        o_ref[1, dst, :] = y[nh:] * gate_ref[1, src, :]
        return carry

    lax.fori_loop(0, DFT_N2, back, 0, unroll=UNROLL_STAGE_A)


def _longconv(z4, z_part, gate4, gate_part, h3, order, consts, natural_out):
    _, bsz, cbs, seq, _ = z4.shape
    ma, mb, mbi, mai = consts[:4]
    nh = seq // DFT_N2
    rows = 2 * DFT_N1 * DFT_N2
    pair = lambda part: (lambda j, p: (part, p, j, 0, 0))
    if natural_out:
        out_spec = pl.BlockSpec((None, 2, seq, LANES), lambda j, p: (0, p, 0, j))
        out_shape = jax.ShapeDtypeStruct((1, bsz, seq, cbs * LANES), F32)
    else:
        out_spec = pl.BlockSpec((None, 2, None, seq, LANES), pair(0))
        out_shape = jax.ShapeDtypeStruct((1, bsz, cbs, seq, LANES), F32)
    return pl.pallas_call(
        functools.partial(_longconv_kernel, nh=nh, natural_out=natural_out),
        grid=(cbs, bsz // 2),
        in_specs=[
            pl.BlockSpec((None, 2, None, seq, LANES), pair(z_part)),
            pl.BlockSpec((None, 2, None, seq, LANES), pair(gate_part)),
            _single((None, None, rows, LANES), lambda j, p: (order, j, 0, 0)),
            _single(ma.shape, lambda j, p: (0, 0, 0)),
            _single(mb.shape, lambda j, p: (0, 0)),
            _single(mbi.shape, lambda j, p: (0, 0)),
            _single(mai.shape, lambda j, p: (0, 0, 0)),
        ],
        out_specs=out_spec,
        out_shape=out_shape,
        scratch_shapes=[pltpu.VMEM((DFT_N2 * A_PITCH, LANES), F32), pltpu.VMEM((DFT_N1 * V_PITCH, LANES), F32)],
        compiler_params=_cparams("parallel", "parallel"),
        name="hyena_longconv",
    )(z4, gate4, h3, ma, mb, mbi, mai)


def _rope_kernel(q_ref, k_ref, v_ref, cos_ref, sin_ref, qo_ref, kt_ref, vo_ref, *, half, scale):
    cos = cos_ref[...]
    sin = sin_ref[...]
    lane = lax.broadcasted_iota(jnp.int32, (1, LANES), 1)
    first_half = (lane % (2 * half)) < half

    def rot(x):
        outs = []
        for j in range(x.shape[1] // LANES):
            xb = x[:, j * LANES:(j + 1) * LANES]
            partner = jnp.where(first_half, pltpu.roll(xb, LANES - half, axis=1), pltpu.roll(xb, half, axis=1))
            outs.append(xb * cos + partner * sin)
        return jnp.concatenate(outs, axis=1)

    qo_ref[...] = (rot(q_ref[...]) * scale).astype(BF16)
    kt_ref[...] = rot(k_ref[...]).T.astype(BF16)
    vo_ref[...] = v_ref[...].astype(BF16)


def _rope(p2, bsz, seq, qk_width, v_width, head_dim, col_q, tm=512):
    t = p2.shape[0]
    assert qk_width == v_width and col_q % qk_width == 0
    jq = col_q // qk_width
    half = head_dim // 2
    inv = ROPE_THETA ** (-jnp.arange(half, dtype=F32) * 2.0 / head_dim)
    ang = jnp.arange(seq, dtype=F32)[:, None] * inv[None, :]
    cos, sin = jnp.cos(ang), jnp.sin(ang)
    reps = LANES // head_dim
    cos_t = jnp.tile(jnp.concatenate([cos, cos], axis=1), (1, reps))
    sin_t = jnp.tile(jnp.concatenate([-sin, sin], axis=1), (1, reps))
    ns = seq // tm
    return pl.pallas_call(
        functools.partial(_rope_kernel, half=half, scale=head_dim ** -0.5 * math.log2(math.e)),
        grid=(t // tm,),
        in_specs=[
            pl.BlockSpec((tm, qk_width), lambda i: (i, jq)),
            pl.BlockSpec((tm, qk_width), lambda i: (i, jq + 1)),
            pl.BlockSpec((tm, v_width), lambda i: (i, jq + 2)),
            pl.BlockSpec((tm, LANES), lambda i: (i % ns, 0)),
            pl.BlockSpec((tm, LANES), lambda i: (i % ns, 0)),
        ],
        out_specs=[
            pl.BlockSpec((tm, qk_width), lambda i: (i, 0)),
            pl.BlockSpec((None, qk_width, tm), lambda i: (i // ns, 0, i % ns)),
            pl.BlockSpec((tm, v_width), lambda i: (i, 0)),
        ],
        out_shape=[
            jax.ShapeDtypeStruct((t, qk_width), BF16),
            jax.ShapeDtypeStruct((bsz, qk_width, seq), BF16),
            jax.ShapeDtypeStruct((t, v_width), BF16),
        ],
        compiler_params=_cparams("parallel"),
        name="rope",
    )(p2, p2, p2, cos_t, sin_t)


def _attn_kernel(q_ref, kt_ref, v_ref, lq1_ref, lk1_ref, lq2_ref, lk2_ref, g_ref, o_ref, *, head_dim, lam_init):
    lam = (jnp.exp(jnp.sum(lq1_ref[...] * lk1_ref[...], axis=-1, keepdims=True))
           - jnp.exp(jnp.sum(lq2_ref[...] * lk2_ref[...], axis=-1, keepdims=True)) + lam_init)
    lane = lax.broadcasted_iota(jnp.int32, (1, q_ref.shape[1]), 1)
    sub = ATTN_SUB_ROWS
    nsub = q_ref.shape[0] // sub

    def scores(j):
        q = q_ref[pl.ds(j * sub, sub), :]
        zero = jnp.zeros_like(q)
        qq = jnp.concatenate([jnp.where(lane < head_dim, q, zero), jnp.where(lane >= head_dim, q, zero)], axis=0)
        return jnp.dot(qq, kt_ref[...], preferred_element_type=F32)

    v = v_ref[...]
    vd = v.shape[1]
    v_aug = jnp.concatenate([v, jnp.where(lane == 0, 1.0, 0.0).astype(BF16) + jnp.zeros_like(v)], axis=1)

    def weights(s):
        return (jnp.exp2(s - jnp.max(s, axis=-1, keepdims=True)).astype(BF16),)

    def emit(j, e):
        r = jnp.dot(e, v_aug, preferred_element_type=F32)
        o = (r[:sub, :vd] / r[:sub, vd:vd + 1]) - lam * (r[sub:, :vd] / r[sub:, vd:vd + 1])
        o = o * lax.rsqrt(jnp.mean(o * o, axis=-1, keepdims=True) + SUBLN_EPS) * g_ref[...]
        o_ref[pl.ds(j * sub, sub), :] = (o * (1.0 - lam_init)).astype(o_ref.dtype)

    s_of, a_of = {}, {}
    for t in range(nsub + 2):
        if t < nsub:
            s_of[t] = scores(t)
        if 0 <= t - 1 < nsub:
            a_of[t - 1] = weights(s_of.pop(t - 1))
        if 0 <= t - 2 < nsub:
            emit(t - 2, *a_of.pop(t - 2))


def _attention(q, kt, v, lq1, lk1, lq2, lk2, subln_g, head_dim, lam_init, tq=512):
    bsz, seq, width = q.shape
    v_dim = subln_g.shape[-1]
    assert v_dim == 2 * head_dim == LANES
    heads = width // v_dim
    vec = lambda a: pl.BlockSpec((1, a.shape[-1]), lambda b, h, i: (0, 0))
    lams = [a[None] for a in (lq1, lk1, lq2, lk2)]
    return pl.pallas_call(
        functools.partial(_attn_kernel, head_dim=head_dim, lam_init=lam_init),
        grid=(bsz, heads, seq // tq),
        in_specs=[
            pl.BlockSpec((None, tq, v_dim), lambda b, h, i: (b, i, h)),
            pl.BlockSpec((None, v_dim, seq), lambda b, h, i: (b, h, 0)),
            pl.BlockSpec((None, seq, v_dim), lambda b, h, i: (b, 0, h)),
            *[vec(a) for a in lams],
            vec(subln_g[None]),
        ],
        out_specs=pl.BlockSpec((None, tq, v_dim), lambda b, h, i: (b, i, h)),
        out_shape=jax.ShapeDtypeStruct((bsz, seq, width), BF16),
        compiler_params=_cparams("parallel", "parallel", "parallel"),
        name="diff_attention",
    )(q, kt, v, *lams, subln_g[None])


def _merge_kernel(x_ref, yh_ref, ya_ref, gh_ref, ga_ref, wuh_ref, wua_ref, wo_ref, g_ref, wr_ref, br_ref,
                  xo_ref, n_ref, aff_ref):
    m = x_ref.shape[0] // MERGE_ROW_SPLIT
    for j in range(MERGE_ROW_SPLIT):
        r = pl.ds(j * m, m)
        mh = jnp.dot(yh_ref[r, :].astype(BF16), wuh_ref[...], preferred_element_type=F32)
        ma = jnp.dot(ya_ref[r, :].astype(BF16), wua_ref[...], preferred_element_type=F32)
        merged = jax.nn.sigmoid(gh_ref[r, :]) * mh + jax.nn.sigmoid(ga_ref[r, :]) * ma
        x = x_ref[r, :] + jnp.dot(merged.astype(BF16), wo_ref[...], preferred_element_type=F32)
        xo_ref[r, :] = x
        n = x * lax.rsqrt(jnp.mean(x * x, axis=-1, keepdims=True) + NORM_EPS) * g_ref[...]
        n_ref[r, :] = n.astype(BF16)
        logits = lax.dot_general(wr_ref[...], n, (((1,), (1,)), ((), ())), preferred_element_type=F32,
                                 precision=lax.Precision.HIGHEST) + br_ref[...]
        e = jnp.exp(logits - jnp.max(logits, axis=0, keepdims=True))
        aff_ref[:, pl.ds(j * m, m)] = e / jnp.sum(e, axis=0, keepdims=True)


def _merge(x2, yh2, ya2, p2, col_gate, wuh, wua, wo, g, wr_t, br, bsz, seq, tm=512):
    t, d = x2.shape
    c = yh2.shape[1]
    e = wr_t.shape[0]
    jg = col_gate // d
    ns = seq // tm
    const = lambda a: pl.BlockSpec(a.shape, lambda i: (0,) * a.ndim)
    return pl.pallas_call(
        _merge_kernel,
        grid=(t // tm,),
        in_specs=[
            pl.BlockSpec((tm, d), lambda i: (i, 0)),
            pl.BlockSpec((tm, c), lambda i: (i, 0)),
            pl.BlockSpec((tm, ya2.shape[1]), lambda i: (i, 0)),
            pl.BlockSpec((tm, d), lambda i: (i, jg)),
            pl.BlockSpec((tm, d), lambda i: (i, jg + 1)),
            const(wuh), const(wua), const(wo), const(g), const(wr_t), const(br),
        ],
        out_specs=[
            pl.BlockSpec((tm, d), lambda i: (i, 0)),
            pl.BlockSpec((tm, d), lambda i: (i, 0)),
            pl.BlockSpec((None, e, tm), lambda i: (i // ns, 0, i % ns)),
        ],
        out_shape=[
            jax.ShapeDtypeStruct((t, d), F32),
            jax.ShapeDtypeStruct((t, d), BF16),
            jax.ShapeDtypeStruct((bsz, e, seq), F32),
        ],
        compiler_params=_cparams("parallel"),
        name="merge_router",
    )(x2, yh2, ya2, p2, p2, wuh, wua, wo, g, wr_t, br)


def _select_kernel(aff_ref, pos_ref, *, cap):
    a = aff_ref[...]
    rows, seq = a.shape
    as_f32 = lambda b: lax.bitcast_convert_type(b, F32)
    count = lambda m: jnp.sum(jnp.where(m, 1.0, 0.0), axis=-1, keepdims=True)
    thr = jnp.zeros((rows, 1), jnp.int32)
    for bit in range(30, -1, -1):
        cand = thr | (1 << bit)
        thr = jnp.where(count(a >= as_f32(cand)) >= cap, cand, thr)
    gt = a >= as_f32(thr + 1)
    eq = (a >= as_f32(thr)) & jnp.logical_not(gt)
    need = cap - count(gt)
    tri = jnp.where(lax.broadcasted_iota(jnp.int32, (LANES, LANES), 0)
                    <= lax.broadcasted_iota(jnp.int32, (LANES, LANES), 1), 1.0, 0.0).astype(BF16)

    def exclusive_cumsum(mask):
        ones = jnp.where(mask, 1.0, 0.0)
        carry = jnp.zeros((rows, 1), F32)
        chunks = []
        for j in range(seq // LANES):
            blk = ones[:, j * LANES:(j + 1) * LANES]
            incl = jnp.dot(blk.astype(BF16), tri, preferred_element_type=F32)
            chunks.append(incl - blk + carry)
            carry = carry + jnp.sum(blk, axis=-1, keepdims=True)
        return jnp.concatenate(chunks, axis=1)

    sel = gt | (eq & (exclusive_cumsum(eq) < need))
    pos_ref[...] = jnp.where(sel, exclusive_cumsum(sel), -1.0).astype(jnp.int32)


def _select(aff_rows, cap):
    return pl.pallas_call(
        functools.partial(_select_kernel, cap=cap),
        out_shape=jax.ShapeDtypeStruct(aff_rows.shape, jnp.int32),
        compiler_params=pltpu.CompilerParams(vmem_limit_bytes=VMEM_LIMIT_V7X),
        name="expert_select",
    )(aff_rows)


def _gather_kernel(starts_ref, pos_ref, aff_ref, n_ref, o_ref, gate_ref, *, win):
    b = pl.program_id(0)
    i = pl.program_id(1)
    e, cap, _ = o_ref.shape
    ts = n_ref.shape[0]

    @pl.when(i == 0)
    def _():
        o_ref[...] = jnp.zeros_like(o_ref)
        gate_ref[...] = jnp.zeros_like(gate_ref)

    slot0 = lax.broadcasted_iota(jnp.int32, (win, ts), 0)
    pos = pos_ref[...]
    aff = aff_ref[...]

    def first_row(x):
        lo = starts_ref[b, x, i]
        return pl.multiple_of(jnp.minimum((lo // BF16_ROWS) * BF16_ROWS, cap - win), BF16_ROWS)

    def add_gates(x, dst, match):
        picked = jnp.sum(jnp.where(match, aff[x:x + 1, :], 0.0), axis=-1, keepdims=True)
        gate_ref[x, dst, :] = gate_ref[x, dst, :] + picked

    for x0 in range(0, e, GATHER_GROUP):
        group = range(x0, min(x0 + GATHER_GROUP, e))
        onehots = []
        for x in group:
            base = first_row(x)
            match = (slot0 + base) == pos[x:x + 1, :]
            add_gates(x, pl.ds(base, win), match)
            onehots.append(match.astype(BF16))
        picked = jnp.dot(jnp.concatenate(onehots, axis=0), n_ref[...], preferred_element_type=F32)
        for k, x in enumerate(group):
            dst = pl.ds(first_row(x), win)
            o_ref[x, dst, :] = o_ref[x, dst, :] + picked[k * win:(k + 1) * win].astype(BF16)

    for x in range(e):
        base = first_row(x)
        hi = starts_ref[b, x, i + 1]

        def extra(k, carry, base=base, x=x):
            want = base + k * win
            row = pl.multiple_of(jnp.minimum(want, cap - win), BF16_ROWS)
            slots = slot0 + row
            match = jnp.logical_and(slots == pos[x:x + 1, :], slots >= want)
            dst = pl.ds(row, win)
            add_gates(x, dst, match)
            o_ref[x, dst, :] = o_ref[x, dst, :] + jnp.dot(match.astype(BF16), n_ref[...],
                                                          preferred_element_type=F32).astype(BF16)
            return carry

        windows = (jnp.maximum(hi - base, 1) + win - 1) // win
        lax.fori_loop(1, windows, extra, 0)


def _gather(pos, aff, starts, n3, cap, ts=COMBINE_TILE, win=128):
    bsz, e, seq = pos.shape
    d = n3.shape[-1]
    assert cap % BF16_ROWS == 0 and win % BF16_ROWS == 0 and win <= cap
    tile = pl.BlockSpec((None, e, ts), lambda b, i, st: (b, 0, i))
    grid_spec = pltpu.PrefetchScalarGridSpec(
        num_scalar_prefetch=1,
        grid=(bsz, seq // ts),
        in_specs=[tile, tile, pl.BlockSpec((None, ts, d), lambda b, i, st: (b, i, 0))],
        out_specs=[
            pl.BlockSpec((None, e, cap, d), lambda b, i, st: (b, 0, 0, 0)),
            pl.BlockSpec((None, e, cap, 1), lambda b, i, st: (b, 0, 0, 0)),
        ],
    )
    return pl.pallas_call(
        functools.partial(_gather_kernel, win=win),
        grid_spec=grid_spec,
        out_shape=[
            jax.ShapeDtypeStruct((bsz, e, cap, d), BF16),
            jax.ShapeDtypeStruct((bsz, e, cap, 1), F32),
        ],
        compiler_params=_cparams("parallel", "arbitrary"),
        name="expert_gather",
    )(starts, pos, aff, n3)


def _expert_kernel(x_ref, gate_ref, wg_ref, wu_ref, wd_ref, o_ref, acc_sc):
    bsz = x_ref.shape[0]

    @pl.when(pl.program_id(1) == 0)
    def _():
        acc_sc[...] = jnp.zeros_like(acc_sc)

    wg = wg_ref[...].astype(BF16)
    wu = wu_ref[...].astype(BF16)
    wd = wd_ref[...].astype(BF16)
    for b in range(bsz):
        x = x_ref[b]
        g = jnp.dot(x, wg, preferred_element_type=F32)
        u = jnp.dot(x, wu, preferred_element_type=F32)
        h = (g * jax.nn.sigmoid(g) * u).astype(BF16)
        total = acc_sc[b] + jnp.dot(h, wd, preferred_element_type=F32)
        acc_sc[b] = total
        o_ref[b] = (total * gate_ref[b]).astype(BF16)


def _experts(xg, gate, wg4, wu4, wd4, layer, f_slices=4):
    bsz, e, cap, d = xg.shape
    f = wg4.shape[-1]
    assert f % f_slices == 0
    fs = f // f_slices
    return pl.pallas_call(
        _expert_kernel,
        grid=(e, f_slices),
        in_specs=[
            pl.BlockSpec((bsz, None, cap, d), lambda x, s: (0, x, 0, 0)),
            pl.BlockSpec((bsz, None, cap, 1), lambda x, s: (0, x, 0, 0)),
            pl.BlockSpec((None, None, d, fs), lambda x, s: (layer, x, 0, s)),
            pl.BlockSpec((None, None, d, fs), lambda x, s: (layer, x, 0, s)),
            pl.BlockSpec((None, None, fs, d), lambda x, s: (layer, x, s, 0)),
        ],
        out_specs=pl.BlockSpec((bsz, None, cap, d), lambda x, s: (0, x, 0, 0)),
        out_shape=jax.ShapeDtypeStruct((bsz, e, cap, d), BF16),
        scratch_shapes=[pltpu.VMEM((bsz, cap, d), F32)],
        compiler_params=_cparams("parallel", "arbitrary"),
        name="expert_ffn",
    )(xg, gate, wg4, wu4, wd4)


def _combine_kernel(starts_ref, x_ref, pos_ref, ye_ref, g_ref, o_ref, stage_sc, *, final, win):
    b = pl.program_id(0)
    i = pl.program_id(1)
    ts = x_ref.shape[0]
    e, cap, _ = ye_ref.shape
    lane = lax.broadcasted_iota(jnp.int32, (ts, win), 1)
    pos = pos_ref[...]

    def first_row(x):
        lo = starts_ref[b, x, i]
        return pl.multiple_of(jnp.minimum((lo // BF16_ROWS) * BF16_ROWS, cap - win), BF16_ROWS)

    onehots = []
    for x in range(e):
        base = first_row(x)
        stage_sc[pl.ds(x * win, win), :] = ye_ref[x, pl.ds(base, win), :]
        onehots.append((pos[:, x:x + 1] - base == lane).astype(BF16))
    o_ref[...] = x_ref[...] + jnp.dot(jnp.concatenate(onehots, axis=1), stage_sc[...],
                                      preferred_element_type=F32)

    for x in range(e):
        base = first_row(x)
        hi = starts_ref[b, x, i + 1]
        col = pos[:, x:x + 1]

        def extra(k, carry, base=base, col=col, x=x):
            want = base + k * win
            row = pl.multiple_of(jnp.minimum(want, cap - win), BF16_ROWS)
            onehot = jnp.logical_and(col - row == lane, col >= want).astype(BF16)
            o_ref[...] += jnp.dot(onehot, ye_ref[x, pl.ds(row, win), :], preferred_element_type=F32)
            return carry

        windows = (jnp.maximum(hi - base, 1) + win - 1) // win
        lax.fori_loop(1, windows, extra, 0)

    if final:
        acc = o_ref[...]
        o_ref[...] = acc * lax.rsqrt(jnp.mean(acc * acc, axis=-1, keepdims=True) + NORM_EPS) * g_ref[...]


def _combine(x3, pos_t, starts, ye, g, final, ts=512, win=128):
    bsz, seq, d = x3.shape
    e, cap = ye.shape[1], ye.shape[2]
    assert cap % BF16_ROWS == 0 and win % BF16_ROWS == 0 and win <= cap
    grid_spec = pltpu.PrefetchScalarGridSpec(
        num_scalar_prefetch=1,
        grid=(bsz, seq // ts),
        in_specs=[
            pl.BlockSpec((None, ts, d), lambda b, i, st: (b, i, 0)),
            pl.BlockSpec((None, ts, e), lambda b, i, st: (b, i, 0)),
            pl.BlockSpec((None, e, cap, d), lambda b, i, st: (b, 0, 0, 0)),
            pl.BlockSpec((1, d), lambda b, i, st: (0, 0)),
        ],
        out_specs=pl.BlockSpec((None, ts, d), lambda b, i, st: (b, i, 0)),
        scratch_shapes=[pltpu.VMEM((e * win, d), BF16)],
    )
    return pl.pallas_call(
        functools.partial(_combine_kernel, final=final, win=win),
        grid_spec=grid_spec,
        out_shape=jax.ShapeDtypeStruct((bsz, seq, d), F32),
        compiler_params=_cparams("parallel", "parallel"),
        name="expert_combine",
    )(starts, x3, pos_t, ye, g)


def kernel(x, norm_mix, w_in, b_in, hy_conv_w, hy_conv_b, hy_ffn_w1, hy_ffn_b1, hy_ffn_f1, hy_ffn_w2, hy_ffn_b2, hy_ffn_f2, hy_ffn_w3, hy_bias, lambda_q1, lambda_k1, lambda_q2, lambda_k2, subln_g, w_up_hyena, w_up_attn, w_out, norm_ffn, w_router, b_router, w_e_gate, w_e_up, w_e_down, norm_final):
    bsz, seq, d = x.shape
    depth = w_in.shape[0]
    orders, c = hy_bias.shape[1], hy_bias.shape[2]
    head_dim = lambda_q1.shape[1]
    v_width = w_up_attn.shape[1]
    qk_width = v_width
    e = w_router.shape[2]
    cap = EC_FACTOR * seq // e
    col_q = (orders + 1) * c
    col_gate = col_q + 2 * qk_width + v_width
    assert orders == 2 and bsz % 2 == 0 and col_gate % d == 0

    consts = tuple(jnp.asarray(a, F32).astype(BF16) for a in _dft_constants(seq))

    xs = x.reshape(bsz * seq, d)
    out = None
    for l in range(depth):
        p2 = _inproj(xs, norm_mix[l][None], w_in[l].astype(BF16), b_in[l][None])
        p3 = p2.reshape(bsz, seq, -1)

        uc = _shortconv(p3, hy_conv_w[l], hy_conv_b[l][None], c)
        hfull = _filters(hy_ffn_w1[l], hy_ffn_b1[l], hy_ffn_f1[l], hy_ffn_w2[l], hy_ffn_b2[l], hy_ffn_f2[l],
                        hy_ffn_w3[l], hy_bias[l], seq)
        hspec = _spectrum(hfull, consts[4], consts[1])
        z = _longconv(uc, 0, uc, 1, hspec, 0, consts, natural_out=False)
        y_hy = _longconv(z, 0, uc, 2, hspec, 1, consts, natural_out=True)

        q_r, k_t, v_b = _rope(p2, bsz, seq, qk_width, v_width, head_dim, col_q)
        lam_init = 0.8 - 0.6 * math.exp(-0.3 * l)
        y_da = _attention(q_r.reshape(bsz, seq, qk_width), k_t, v_b.reshape(bsz, seq, v_width),
                          lambda_q1[l], lambda_k1[l], lambda_q2[l], lambda_k2[l], subln_g[l], head_dim, lam_init)

        xs, n2, aff = _merge(xs, y_hy.reshape(bsz * seq, c), y_da.reshape(bsz * seq, v_width), p2, col_gate,
                             w_up_hyena[l].astype(BF16), w_up_attn[l].astype(BF16), w_out[l].astype(BF16),
                             norm_ffn[l][None], w_router[l].T, b_router[l][:, None], bsz, seq)

        pos = _select(aff.reshape(bsz * e, seq), cap).reshape(bsz, e, seq)
        tiles = seq // COMBINE_TILE
        counts = jnp.sum((pos >= 0).reshape(bsz, e, tiles, COMBINE_TILE), axis=-1, dtype=jnp.int32)
        starts = jnp.concatenate([jnp.zeros((bsz, e, 1), jnp.int32), jnp.cumsum(counts, axis=-1)], axis=-1)
        xg, gate = _gather(pos, aff, starts, n2.reshape(bsz, seq, d), cap)
        ye = _experts(xg, gate, w_e_gate, w_e_up, w_e_down, l)
        final = l == depth - 1
        out = _combine(xs.reshape(bsz, seq, d), pos.transpose(0, 2, 1), starts, ye, norm_final[None], final,
                       ts=COMBINE_TILE)
        xs = out.reshape(bsz * seq, d)
    return out
```

```python
import functools
import math

import numpy as np
import jax
import jax.numpy as jnp
from jax import lax
from jax.experimental import pallas as pl
from jax.experimental.pallas import tpu as pltpu

F32 = jnp.float32
BF16 = jnp.bfloat16

NORM_EPS = 1e-6
SUBLN_EPS = 1e-5
ROPE_THETA = 10000.0
HY_FAST_DECAY = 0.3
HY_SLOW_DECAY = 1.5
HY_TARGET = 1e-2
EC_FACTOR = 2

VMEM_LIMIT_V7X = 56 * 1024 * 1024
LANES = 128
BF16_ROWS = 16
COMBINE_TILE = 512
GATHER_GROUP = 16
ATTN_SUB_ROWS = 128

DFT_N1 = 64
DFT_N2 = 128
SUBLANES = 8
A_PITCH = 2 * DFT_N1 + SUBLANES
V_PITCH = 2 * DFT_N2 + SUBLANES
H_PITCH = DFT_N2 + SUBLANES
UNROLL_STAGE_A = 64
UNROLL_STAGE_B = 16
STAGE_B_COLS = 2
UNROLL_SHORTCONV = 8


def _cparams(*sem):
    return pltpu.CompilerParams(dimension_semantics=sem, vmem_limit_bytes=VMEM_LIMIT_V7X)


def _single(block_shape, index_map):
    return pl.BlockSpec(block_shape, index_map, pipeline_mode=pl.Buffered(1))


def _inproj_kernel(x_ref, g_ref, w_ref, b_ref, o_ref, n_sc):
    @pl.when(pl.program_id(1) == 0)
    def _():
        x = x_ref[...]
        n = x * lax.rsqrt(jnp.mean(x * x, axis=-1, keepdims=True) + NORM_EPS) * g_ref[...]
        n_sc[...] = n.astype(BF16)

    o_ref[...] = jnp.dot(n_sc[...], w_ref[...], preferred_element_type=F32) + b_ref[...]


def _inproj(x2, g, w_bf, b, tm=2048, tn=1024):
    t, d = x2.shape
    width = w_bf.shape[1]
    return pl.pallas_call(
        _inproj_kernel,
        grid=(t // tm, width // tn),
        in_specs=[
            pl.BlockSpec((tm, d), lambda i, j: (i, 0)),
            pl.BlockSpec((1, d), lambda i, j: (0, 0)),
            pl.BlockSpec((d, tn), lambda i, j: (0, j)),
            pl.BlockSpec((1, tn), lambda i, j: (0, j)),
        ],
        out_specs=pl.BlockSpec((tm, tn), lambda i, j: (i, j)),
        out_shape=jax.ShapeDtypeStruct((t, width), F32),
        scratch_shapes=[pltpu.VMEM((tm, d), BF16)],
        compiler_params=_cparams("parallel", "arbitrary"),
        name="inproj",
    )(x2, g, w_bf, b)


def _shortconv_kernel(u_ref, w_ref, b_ref, o_ref, pad_sc, *, seq, n2, nh):
    zero_row = jnp.zeros((1, LANES), F32)
    for q in range(nh):
        base = q * H_PITCH + SUBLANES
        pad_sc[pl.ds(base, n2), :] = u_ref[pl.ds(q * n2, n2), :]
        pad_sc[pl.ds(base - 1, 1), :] = u_ref[pl.ds(q * n2 - 1, 1), :] if q > 0 else zero_row
        pad_sc[pl.ds(base + n2, 1), :] = u_ref[pl.ds((q + 1) * n2, 1), :] if q < nh - 1 else zero_row
    w = w_ref[...]
    bias = b_ref[...]

    column = lambda t: pad_sc[pl.ds(SUBLANES - 1 + t, nh, stride=H_PITCH), :]

    def body(r, taps):
        prev, cur = taps
        nxt = column(r + 2)
        o_ref[pl.ds(pl.multiple_of(r * nh, nh), nh), :] = prev * w[0:1] + cur * w[1:2] + nxt * w[2:3] + bias
        return cur, nxt

    lax.fori_loop(0, n2, body, (column(0), column(1)), unroll=UNROLL_SHORTCONV)


def _shortconv(p3, conv_w, conv_b, c):
    bsz, seq, _ = p3.shape
    parts = conv_w.shape[1] // c
    cb_per_part = c // LANES
    nh = seq // DFT_N2
    return pl.pallas_call(
        functools.partial(_shortconv_kernel, seq=seq, n2=DFT_N2, nh=nh),
        grid=(bsz, parts * cb_per_part),
        in_specs=[
            pl.BlockSpec((None, seq, LANES), lambda b, j: (b, 0, j)),
            pl.BlockSpec((3, LANES), lambda b, j: (0, j)),
            pl.BlockSpec((1, LANES), lambda b, j: (0, j)),
        ],
        out_specs=pl.BlockSpec((None, None, None, seq, LANES),
                               lambda b, j: (j // cb_per_part, b, j % cb_per_part, 0, 0)),
        out_shape=jax.ShapeDtypeStruct((parts, bsz, cb_per_part, seq, LANES), F32),
        scratch_shapes=[pltpu.VMEM((nh * H_PITCH + SUBLANES, LANES), F32)],
        compiler_params=_cparams("parallel", "parallel"),
        name="shortconv",
    )(p3, conv_w, conv_b)


def _filter_kernel(w1t_ref, w1c_ref, w1s_ref, b1_ref, f1_ref, w2_ref, b2_ref, f2_ref, w3_ref, bias_ref, o_ref,
                   *, seq, tl, c, bands):
    hi = lax.Precision.HIGHEST
    lag = lambda idx: jnp.where(idx < seq, idx, 2 * seq - idx).astype(F32)
    row = lax.broadcasted_iota(jnp.int32, (tl, 1), 0) + pl.program_id(0) * tl
    t = lag(row) / (seq - 1.0)
    pos = lag(lax.broadcasted_iota(jnp.int32, (1, tl), 1) + pl.program_id(0) * tl)
    t_l = pos / (seq - 1.0)
    w_l = (2.0 * math.pi) * pos / float(seq)
    band = lax.broadcasted_iota(jnp.int32, (bands, 1), 0).astype(F32)
    fr = 1e-4 + band * ((bands - 1 - 1e-4) / (bands - 1))
    ang = fr * w_l
    pre = (w1t_ref[...] * t_l
           + jnp.dot(w1c_ref[...], jnp.cos(ang), preferred_element_type=F32, precision=hi)
           - jnp.dot(w1s_ref[...], jnp.sin(ang), preferred_element_type=F32, precision=hi)
           + b1_ref[...])
    h = jnp.sin(f1_ref[...] * pre)
    h = jnp.sin(f2_ref[...] * (jnp.dot(w2_ref[...], h, preferred_element_type=F32, precision=hi) + b2_ref[...]))
    h = jnp.dot(h.T, w3_ref[...], preferred_element_type=F32, precision=hi)
    min_decay = math.log(HY_TARGET) / HY_FAST_DECAY
    max_decay = math.log(HY_TARGET) / HY_SLOW_DECAY
    ch = lax.broadcasted_iota(jnp.int32, (1, c), 1).astype(F32)
    deltas = jnp.abs(min_decay + ch * ((max_decay - min_decay) / (c - 1)))
    decay = jnp.exp(-t * deltas)
    orders = o_ref.shape[0]
    for o in range(orders):
        taps = h[:, o * c:(o + 1) * c] * decay
        taps = jnp.where(row == 0, taps + bias_ref[o:o + 1, :], taps)
        taps = jnp.where(row == seq, 0.0, taps)
        for blk in range(tl // DFT_N2):
            for cb in range(c // LANES):
                o_ref[o, cb, pl.ds(blk * H_PITCH, DFT_N2), :] = taps[blk * DFT_N2:(blk + 1) * DFT_N2,
                                                                   cb * LANES:(cb + 1) * LANES]
                o_ref[o, cb, pl.ds(blk * H_PITCH + DFT_N2, H_PITCH - DFT_N2), :] = jnp.zeros(
                    (H_PITCH - DFT_N2, LANES), F32)


def _filters(w1, b1, f1, w2, b2, f2, w3, bias, seq, tl=512):
    emb, ffn = w1.shape
    bands = (emb - 1) // 2
    orders, c = bias.shape
    tiles = seq // tl
    w3_dir = w3.reshape(ffn, orders, 2, c).transpose(2, 0, 1, 3).reshape(2, ffn, orders * c)
    full = lambda a: pl.BlockSpec(a.shape, lambda i: (0,) * a.ndim)
    col = lambda a: a[:, None]
    args = (w1[0:1].T, w1[1:1 + bands].T, w1[1 + bands:].T, col(b1), col(f1), w2.T, col(b2), col(f2), w3_dir, bias)
    in_specs = [full(a) for a in args]
    in_specs[8] = pl.BlockSpec((None, ffn, orders * c), lambda i: (i // tiles, 0, 0))
    return pl.pallas_call(
        functools.partial(_filter_kernel, seq=seq, tl=tl, c=c, bands=bands),
        grid=(2 * tiles,),
        in_specs=in_specs,
        out_specs=pl.BlockSpec((orders, c // LANES, tl // DFT_N2 * H_PITCH, LANES), lambda i: (0, 0, i, 0)),
        out_shape=jax.ShapeDtypeStruct((orders, c // LANES, 2 * seq // DFT_N2 * H_PITCH, LANES), F32),
        compiler_params=_cparams("parallel"),
        name="hyena_filters",
    )(*args)


def _stack_complex(m):
    return np.block([[m.real, -m.imag], [m.imag, m.real]])


@functools.lru_cache(maxsize=None)
def _dft_constants(seq):
    n1, n2 = DFT_N1, DFT_N2
    n = n1 * n2
    assert n == 2 * seq
    nh = seq // n2
    k1 = np.arange(n1)[:, None].astype(np.float64)
    q = np.arange(nh)[None, :].astype(np.float64)
    qf = np.arange(n1)[None, :].astype(np.float64)
    ma = np.empty((n2, 2 * n1, 2 * nh), np.float64)
    mai = np.empty((n2, 2 * nh, 2 * n1), np.float64)
    maf = np.empty((n2, 2 * n1, n1), np.float64)
    for r in range(n2):
        e = np.exp(-2j * np.pi * (q * k1 / n1 + r * k1 / n))
        ma[r] = _stack_complex(e)
        mai[r] = _stack_complex(np.conj(e).T / n)
        ef = np.exp(-2j * np.pi * (qf * k1 / n1 + r * k1 / n))
        maf[r] = np.concatenate([ef.real, ef.imag], axis=0)
    kk = np.arange(n2)[:, None].astype(np.float64)
    rr = np.arange(n2)[None, :].astype(np.float64)
    f = np.exp(-2j * np.pi * kk * rr / n2)
    mb = _stack_complex(f)
    mbi = _stack_complex(np.conj(f).T)
    return tuple(np.asarray(a, np.float32) for a in (ma, mb, mbi, mai, maf))


def _stage_a(load_x, ma_ref, a_sc):
    def body(r, carry):
        a_sc[pl.ds(pl.multiple_of(r * A_PITCH, SUBLANES), 2 * DFT_N1), :] = jnp.dot(
            ma_ref[r], load_x(r), preferred_element_type=F32)
        return carry

    lax.fori_loop(0, DFT_N2, body, 0, unroll=UNROLL_STAGE_A)


def _load_a_columns(a_sc, k1):
    cols = []
    for j in range(STAGE_B_COLS):
        ar = a_sc[pl.ds(k1 + j, DFT_N2, stride=A_PITCH), :]
        ai = a_sc[pl.ds(DFT_N1 + k1 + j, DFT_N2, stride=A_PITCH), :]
        cols.append(jnp.concatenate([ar, ai], axis=0))
    return jnp.concatenate(cols, axis=1).astype(BF16)


def _spectrum_kernel(h_ref, maf_ref, mb_ref, o_ref, a_sc):
    rows = 2 * DFT_N2

    def load_x(r):
        return h_ref[pl.ds(r, DFT_N1, stride=H_PITCH), :].astype(BF16)

    _stage_a(load_x, maf_ref, a_sc)

    cb = o_ref.shape[-1]

    def body(kp, carry):
        k1 = kp * STAGE_B_COLS
        z = jnp.dot(mb_ref[...], _load_a_columns(a_sc, k1), preferred_element_type=F32)
        for j in range(STAGE_B_COLS):
            o_ref[pl.ds(pl.multiple_of((k1 + j) * rows, rows), rows), :] = z[:, j * cb:(j + 1) * cb]
        return carry

    lax.fori_loop(0, DFT_N1 // STAGE_B_COLS, body, 0, unroll=UNROLL_STAGE_B)


def _spectrum(hfull, maf, mb):
    orders, cbs, n, _ = hfull.shape
    rows = 2 * DFT_N1 * DFT_N2
    return pl.pallas_call(
        _spectrum_kernel,
        grid=(orders, cbs),
        in_specs=[
            pl.BlockSpec((None, None, n, LANES), lambda o, j: (o, j, 0, 0)),
            _single(maf.shape, lambda o, j: (0, 0, 0)),
            _single(mb.shape, lambda o, j: (0, 0)),
        ],
        out_specs=pl.BlockSpec((None, None, rows, LANES), lambda o, j: (o, j, 0, 0)),
        out_shape=jax.ShapeDtypeStruct((orders, cbs, rows, LANES), F32),
        scratch_shapes=[pltpu.VMEM((DFT_N2 * A_PITCH, LANES), F32)],
        compiler_params=_cparams("parallel", "parallel"),
        name="hyena_spectrum",
    )(hfull, maf, mb)


def _longconv_kernel(z_ref, gate_ref, h_ref, ma_ref, mb_ref, mbi_ref, mai_ref, o_ref, a_sc, v_sc,
                     *, nh, natural_out):
    def load_x(r):
        src = pl.ds(pl.multiple_of(r * nh, nh), nh)
        return jnp.concatenate([z_ref[0, src, :], z_ref[1, src, :]], axis=0).astype(BF16)

    _stage_a(load_x, ma_ref, a_sc)

    rows = 2 * DFT_N2

    cb = o_ref.shape[-1]

    def freq(kp, carry):
        k1 = kp * STAGE_B_COLS
        zf = jnp.dot(mb_ref[...], _load_a_columns(a_sc, k1), preferred_element_type=F32)
        ys = []
        for j in range(STAGE_B_COLS):
            base = pl.multiple_of((k1 + j) * rows, rows)
            hr = h_ref[pl.ds(base, DFT_N2), :]
            hi = h_ref[pl.ds(base + DFT_N2, DFT_N2), :]
            zr = zf[:DFT_N2, j * cb:(j + 1) * cb]
            zi = zf[DFT_N2:, j * cb:(j + 1) * cb]
            ys.append(jnp.concatenate([zr * hr - zi * hi, zr * hi + zi * hr], axis=0))
        v = jnp.dot(mbi_ref[...], jnp.concatenate(ys, axis=1).astype(BF16), preferred_element_type=F32)
        for j in range(STAGE_B_COLS):
            v_sc[pl.ds(pl.multiple_of((k1 + j) * V_PITCH, SUBLANES), rows), :] = v[:, j * cb:(j + 1) * cb]
        return carry

    lax.fori_loop(0, DFT_N1 // STAGE_B_COLS, freq, 0, unroll=UNROLL_STAGE_B)

    def back(r, carry):
        vr = v_sc[pl.ds(r, DFT_N1, stride=V_PITCH), :]
        vi = v_sc[pl.ds(DFT_N2 + r, DFT_N1, stride=V_PITCH), :]
        y = jnp.dot(mai_ref[r], jnp.concatenate([vr, vi], axis=0).astype(BF16), preferred_element_type=F32)
        src = pl.ds(pl.multiple_of(r * nh, nh), nh)
        dst = pl.ds(r, nh, stride=DFT_N2) if natural_out else src
        o_ref[0, dst, :] = y[:nh] * gate_ref[0, src, :]
        o_ref[1, dst, :] = y[nh:] * gate_ref[1, src, :]
        return carry

    lax.fori_loop(0, DFT_N2, back, 0, unroll=UNROLL_STAGE_A)


def _longconv(z4, z_part, gate4, gate_part, h3, order, consts, natural_out):
    _, bsz, cbs, seq, _ = z4.shape
    ma, mb, mbi, mai = consts[:4]
    nh = seq // DFT_N2
    rows = 2 * DFT_N1 * DFT_N2
    pair = lambda part: (lambda j, p: (part, p, j, 0, 0))
    if natural_out:
        out_spec = pl.BlockSpec((None, 2, seq, LANES), lambda j, p: (0, p, 0, j))
        out_shape = jax.ShapeDtypeStruct((1, bsz, seq, cbs * LANES), F32)
    else:
        out_spec = pl.BlockSpec((None, 2, None, seq, LANES), pair(0))
        out_shape = jax.ShapeDtypeStruct((1, bsz, cbs, seq, LANES), F32)
    return pl.pallas_call(
        functools.partial(_longconv_kernel, nh=nh, natural_out=natural_out),
        grid=(cbs, bsz // 2),
        in_specs=[
            pl.BlockSpec((None, 2, None, seq, LANES), pair(z_part)),
            pl.BlockSpec((None, 2, None, seq, LANES), pair(gate_part)),
            _single((None, None, rows, LANES), lambda j, p: (order, j, 0, 0)),
            _single(ma.shape, lambda j, p: (0, 0, 0)),
            _single(mb.shape, lambda j, p: (0, 0)),
            _single(mbi.shape, lambda j, p: (0, 0)),
            _single(mai.shape, lambda j, p: (0, 0, 0)),
        ],
        out_specs=out_spec,
        out_shape=out_shape,
        scratch_shapes=[pltpu.VMEM((DFT_N2 * A_PITCH, LANES), F32), pltpu.VMEM((DFT_N1 * V_PITCH, LANES), F32)],
        compiler_params=_cparams("parallel", "parallel"),
        name="hyena_longconv",
    )(z4, gate4, h3, ma, mb, mbi, mai)


def _rope_kernel(q_ref, k_ref, v_ref, cos_ref, sin_ref, qo_ref, kt_ref, vo_ref, *, half, scale):
    cos = cos_ref[...]
    sin = sin_ref[...]
    lane = lax.broadcasted_iota(jnp.int32, (1, LANES), 1)
    first_half = (lane % (2 * half)) < half

    def rot(x):
        outs = []
        for j in range(x.shape[1] // LANES):
            xb = x[:, j * LANES:(j + 1) * LANES]
            partner = jnp.where(first_half, pltpu.roll(xb, LANES - half, axis=1), pltpu.roll(xb, half, axis=1))
            outs.append(xb * cos + partner * sin)
        return jnp.concatenate(outs, axis=1)

    qo_ref[...] = (rot(q_ref[...]) * scale).astype(BF16)
    kt_ref[...] = rot(k_ref[...]).T.astype(BF16)
    vo_ref[...] = v_ref[...].astype(BF16)


def _rope(p2, bsz, seq, qk_width, v_width, head_dim, col_q, tm=512):
    t = p2.shape[0]
    assert qk_width == v_width and col_q % qk_width == 0
    jq = col_q // qk_width
    half = head_dim // 2
    inv = ROPE_THETA ** (-jnp.arange(half, dtype=F32) * 2.0 / head_dim)
    ang = jnp.arange(seq, dtype=F32)[:, None] * inv[None, :]
    cos, sin = jnp.cos(ang), jnp.sin(ang)
    reps = LANES // head_dim
    cos_t = jnp.tile(jnp.concatenate([cos, cos], axis=1), (1, reps))
    sin_t = jnp.tile(jnp.concatenate([-sin, sin], axis=1), (1, reps))
    ns = seq // tm
    return pl.pallas_call(
        functools.partial(_rope_kernel, half=half, scale=head_dim ** -0.5 * math.log2(math.e)),
        grid=(t // tm,),
        in_specs=[
            pl.BlockSpec((tm, qk_width), lambda i: (i, jq)),
            pl.BlockSpec((tm, qk_width), lambda i: (i, jq + 1)),
            pl.BlockSpec((tm, v_width), lambda i: (i, jq + 2)),
            pl.BlockSpec((tm, LANES), lambda i: (i % ns, 0)),
            pl.BlockSpec((tm, LANES), lambda i: (i % ns, 0)),
        ],
        out_specs=[
            pl.BlockSpec((tm, qk_width), lambda i: (i, 0)),
            pl.BlockSpec((None, qk_width, tm), lambda i: (i // ns, 0, i % ns)),
            pl.BlockSpec((tm, v_width), lambda i: (i, 0)),
        ],
        out_shape=[
            jax.ShapeDtypeStruct((t, qk_width), BF16),
            jax.ShapeDtypeStruct((bsz, qk_width, seq), BF16),
            jax.ShapeDtypeStruct((t, v_width), BF16),
        ],
        compiler_params=_cparams("parallel"),
        name="rope",
    )(p2, p2, p2, cos_t, sin_t)


def _attn_kernel(q_ref, kt_ref, v_ref, lq1_ref, lk1_ref, lq2_ref, lk2_ref, g_ref, o_ref, *, head_dim, lam_init):
    lam = (jnp.exp(jnp.sum(lq1_ref[...] * lk1_ref[...], axis=-1, keepdims=True))
           - jnp.exp(jnp.sum(lq2_ref[...] * lk2_ref[...], axis=-1, keepdims=True)) + lam_init)
    lane = lax.broadcasted_iota(jnp.int32, (1, q_ref.shape[1]), 1)
    sub = ATTN_SUB_ROWS
    nsub = q_ref.shape[0] // sub

    def scores(j):
        q = q_ref[pl.ds(j * sub, sub), :]
        zero = jnp.zeros_like(q)
        qq = jnp.concatenate([jnp.where(lane < head_dim, q, zero), jnp.where(lane >= head_dim, q, zero)], axis=0)
        return jnp.dot(qq, kt_ref[...], preferred_element_type=F32)

    v = v_ref[...]
    vd = v.shape[1]
    v_aug = jnp.concatenate([v, jnp.where(lane == 0, 1.0, 0.0).astype(BF16) + jnp.zeros_like(v)], axis=1)

    def weights(s):
        return (jnp.exp2(s - jnp.max(s, axis=-1, keepdims=True)).astype(BF16),)

    def emit(j, e):
        r = jnp.dot(e, v_aug, preferred_element_type=F32)
        o = (r[:sub, :vd] / r[:sub, vd:vd + 1]) - lam * (r[sub:, :vd] / r[sub:, vd:vd + 1])
        o = o * lax.rsqrt(jnp.mean(o * o, axis=-1, keepdims=True) + SUBLN_EPS) * g_ref[...]
        o_ref[pl.ds(j * sub, sub), :] = (o * (1.0 - lam_init)).astype(o_ref.dtype)

    s_of, a_of = {}, {}
    for t in range(nsub + 2):
        if t < nsub:
            s_of[t] = scores(t)
        if 0 <= t - 1 < nsub:
            a_of[t - 1] = weights(s_of.pop(t - 1))
        if 0 <= t - 2 < nsub:
            emit(t - 2, *a_of.pop(t - 2))


def _attention(q, kt, v, lq1, lk1, lq2, lk2, subln_g, head_dim, lam_init, tq=512):
    bsz, seq, width = q.shape
    v_dim = subln_g.shape[-1]
    assert v_dim == 2 * head_dim == LANES
    heads = width // v_dim
    vec = lambda a: pl.BlockSpec((1, a.shape[-1]), lambda b, h, i: (0, 0))
    lams = [a[None] for a in (lq1, lk1, lq2, lk2)]
    return pl.pallas_call(
        functools.partial(_attn_kernel, head_dim=head_dim, lam_init=lam_init),
        grid=(bsz, heads, seq // tq),
        in_specs=[
            pl.BlockSpec((None, tq, v_dim), lambda b, h, i: (b, i, h)),
            pl.BlockSpec((None, v_dim, seq), lambda b, h, i: (b, h, 0)),
            pl.BlockSpec((None, seq, v_dim), lambda b, h, i: (b, 0, h)),
            *[vec(a) for a in lams],
            vec(subln_g[None]),
        ],
        out_specs=pl.BlockSpec((None, tq, v_dim), lambda b, h, i: (b, i, h)),
        out_shape=jax.ShapeDtypeStruct((bsz, seq, width), BF16),
        compiler_params=_cparams("parallel", "parallel", "parallel"),
        name="diff_attention",
    )(q, kt, v, *lams, subln_g[None])


def _merge_kernel(x_ref, yh_ref, ya_ref, gh_ref, ga_ref, wuh_ref, wua_ref, wo_ref, g_ref, wr_ref, br_ref,
                  xo_ref, n_ref, aff_ref):
    mh = jnp.dot(yh_ref[...].astype(BF16), wuh_ref[...], preferred_element_type=F32)
    ma = jnp.dot(ya_ref[...].astype(BF16), wua_ref[...], preferred_element_type=F32)
    merged = jax.nn.sigmoid(gh_ref[...]) * mh + jax.nn.sigmoid(ga_ref[...]) * ma
    x = x_ref[...] + jnp.dot(merged.astype(BF16), wo_ref[...], preferred_element_type=F32)
    xo_ref[...] = x
    n = x * lax.rsqrt(jnp.mean(x * x, axis=-1, keepdims=True) + NORM_EPS) * g_ref[...]
    n_ref[...] = n.astype(BF16)
    logits = lax.dot_general(wr_ref[...], n, (((1,), (1,)), ((), ())), preferred_element_type=F32,
                             precision=lax.Precision.HIGHEST) + br_ref[...]
    e = jnp.exp(logits - jnp.max(logits, axis=0, keepdims=True))
    aff_ref[...] = e / jnp.sum(e, axis=0, keepdims=True)


def _merge(x2, yh2, ya2, p2, col_gate, wuh, wua, wo, g, wr_t, br, bsz, seq, tm=512):
    t, d = x2.shape
    c = yh2.shape[1]
    e = wr_t.shape[0]
    jg = col_gate // d
    ns = seq // tm
    const = lambda a: pl.BlockSpec(a.shape, lambda i: (0,) * a.ndim)
    return pl.pallas_call(
        _merge_kernel,
        grid=(t // tm,),
        in_specs=[
            pl.BlockSpec((tm, d), lambda i: (i, 0)),
            pl.BlockSpec((tm, c), lambda i: (i, 0)),
            pl.BlockSpec((tm, ya2.shape[1]), lambda i: (i, 0)),
            pl.BlockSpec((tm, d), lambda i: (i, jg)),
            pl.BlockSpec((tm, d), lambda i: (i, jg + 1)),
            const(wuh), const(wua), const(wo), const(g), const(wr_t), const(br),
        ],
        out_specs=[
            pl.BlockSpec((tm, d), lambda i: (i, 0)),
            pl.BlockSpec((tm, d), lambda i: (i, 0)),
            pl.BlockSpec((None, e, tm), lambda i: (i // ns, 0, i % ns)),
        ],
        out_shape=[
            jax.ShapeDtypeStruct((t, d), F32),
            jax.ShapeDtypeStruct((t, d), BF16),
            jax.ShapeDtypeStruct((bsz, e, seq), F32),
        ],
        compiler_params=_cparams("parallel"),
        name="merge_router",
    )(x2, yh2, ya2, p2, p2, wuh, wua, wo, g, wr_t, br)


def _select_kernel(aff_ref, pos_ref, *, cap):
    a = aff_ref[...]
    rows, seq = a.shape
    as_f32 = lambda b: lax.bitcast_convert_type(b, F32)
    count = lambda m: jnp.sum(jnp.where(m, 1.0, 0.0), axis=-1, keepdims=True)
    thr = jnp.zeros((rows, 1), jnp.int32)
    for bit in range(30, -1, -1):
        cand = thr | (1 << bit)
        thr = jnp.where(count(a >= as_f32(cand)) >= cap, cand, thr)
    gt = a >= as_f32(thr + 1)
    eq = (a >= as_f32(thr)) & jnp.logical_not(gt)
    need = cap - count(gt)
    tri = jnp.where(lax.broadcasted_iota(jnp.int32, (LANES, LANES), 0)
                    <= lax.broadcasted_iota(jnp.int32, (LANES, LANES), 1), 1.0, 0.0).astype(BF16)

    def exclusive_cumsum(mask):
        ones = jnp.where(mask, 1.0, 0.0)
        carry = jnp.zeros((rows, 1), F32)
        chunks = []
        for j in range(seq // LANES):
            blk = ones[:, j * LANES:(j + 1) * LANES]
            incl = jnp.dot(blk.astype(BF16), tri, preferred_element_type=F32)
            chunks.append(incl - blk + carry)
            carry = carry + jnp.sum(blk, axis=-1, keepdims=True)
        return jnp.concatenate(chunks, axis=1)

    sel = gt | (eq & (exclusive_cumsum(eq) < need))
    pos_ref[...] = jnp.where(sel, exclusive_cumsum(sel), -1.0).astype(jnp.int32)


def _select(aff_rows, cap):
    return pl.pallas_call(
        functools.partial(_select_kernel, cap=cap),
        out_shape=jax.ShapeDtypeStruct(aff_rows.shape, jnp.int32),
        compiler_params=pltpu.CompilerParams(vmem_limit_bytes=VMEM_LIMIT_V7X),
        name="expert_select",
    )(aff_rows)


def _gather_kernel(starts_ref, pos_ref, aff_ref, n_ref, o_ref, gate_ref, *, win):
    b = pl.program_id(0)
    i = pl.program_id(1)
    e, cap, _ = o_ref.shape
    ts = n_ref.shape[0]

    @pl.when(i == 0)
    def _():
        o_ref[...] = jnp.zeros_like(o_ref)
        gate_ref[...] = jnp.zeros_like(gate_ref)

    slot0 = lax.broadcasted_iota(jnp.int32, (win, ts), 0)
    pos = pos_ref[...]
    aff = aff_ref[...]

    def first_row(x):
        lo = starts_ref[b, x, i]
        return pl.multiple_of(jnp.minimum((lo // BF16_ROWS) * BF16_ROWS, cap - win), BF16_ROWS)

    def add_gates(x, dst, match):
        picked = jnp.sum(jnp.where(match, aff[x:x + 1, :], 0.0), axis=-1, keepdims=True)
        gate_ref[x, dst, :] = gate_ref[x, dst, :] + picked

    for x0 in range(0, e, GATHER_GROUP):
        group = range(x0, min(x0 + GATHER_GROUP, e))
        onehots = []
        for x in group:
            base = first_row(x)
            match = (slot0 + base) == pos[x:x + 1, :]
            add_gates(x, pl.ds(base, win), match)
            onehots.append(match.astype(BF16))
        picked = jnp.dot(jnp.concatenate(onehots, axis=0), n_ref[...], preferred_element_type=F32)
        for k, x in enumerate(group):
            dst = pl.ds(first_row(x), win)
            o_ref[x, dst, :] = o_ref[x, dst, :] + picked[k * win:(k + 1) * win].astype(BF16)

    for x in range(e):
        base = first_row(x)
        hi = starts_ref[b, x, i + 1]

        def extra(k, carry, base=base, x=x):
            want = base + k * win
            row = pl.multiple_of(jnp.minimum(want, cap - win), BF16_ROWS)
            slots = slot0 + row
            match = jnp.logical_and(slots == pos[x:x + 1, :], slots >= want)
            dst = pl.ds(row, win)
            add_gates(x, dst, match)
            o_ref[x, dst, :] = o_ref[x, dst, :] + jnp.dot(match.astype(BF16), n_ref[...],
                                                          preferred_element_type=F32).astype(BF16)
            return carry

        windows = (jnp.maximum(hi - base, 1) + win - 1) // win
        lax.fori_loop(1, windows, extra, 0)


def _gather(pos, aff, starts, n3, cap, ts=COMBINE_TILE, win=128):
    bsz, e, seq = pos.shape
    d = n3.shape[-1]
    assert cap % BF16_ROWS == 0 and win % BF16_ROWS == 0 and win <= cap
    tile = pl.BlockSpec((None, e, ts), lambda b, i, st: (b, 0, i))
    grid_spec = pltpu.PrefetchScalarGridSpec(
        num_scalar_prefetch=1,
        grid=(bsz, seq // ts),
        in_specs=[tile, tile, pl.BlockSpec((None, ts, d), lambda b, i, st: (b, i, 0))],
        out_specs=[
            pl.BlockSpec((None, e, cap, d), lambda b, i, st: (b, 0, 0, 0)),
            pl.BlockSpec((None, e, cap, 1), lambda b, i, st: (b, 0, 0, 0)),
        ],
    )
    return pl.pallas_call(
        functools.partial(_gather_kernel, win=win),
        grid_spec=grid_spec,
        out_shape=[
            jax.ShapeDtypeStruct((bsz, e, cap, d), BF16),
            jax.ShapeDtypeStruct((bsz, e, cap, 1), F32),
        ],
        compiler_params=_cparams("parallel", "arbitrary"),
        name="expert_gather",
    )(starts, pos, aff, n3)


def _expert_kernel(x_ref, gate_ref, wg_ref, wu_ref, wd_ref, o_ref, acc_sc):
    bsz = x_ref.shape[0]

    @pl.when(pl.program_id(1) == 0)
    def _():
        acc_sc[...] = jnp.zeros_like(acc_sc)

    wg = wg_ref[...].astype(BF16)
    wu = wu_ref[...].astype(BF16)
    wd = wd_ref[...].astype(BF16)
    for b in range(bsz):
        x = x_ref[b]
        g = jnp.dot(x, wg, preferred_element_type=F32)
        u = jnp.dot(x, wu, preferred_element_type=F32)
        h = (g * jax.nn.sigmoid(g) * u).astype(BF16)
        total = acc_sc[b] + jnp.dot(h, wd, preferred_element_type=F32)
        acc_sc[b] = total
        o_ref[b] = (total * gate_ref[b]).astype(BF16)


def _experts(xg, gate, wg4, wu4, wd4, layer, f_slices=4):
    bsz, e, cap, d = xg.shape
    f = wg4.shape[-1]
    assert f % f_slices == 0
    fs = f // f_slices
    return pl.pallas_call(
        _expert_kernel,
        grid=(e, f_slices),
        in_specs=[
            pl.BlockSpec((bsz, None, cap, d), lambda x, s: (0, x, 0, 0)),
            pl.BlockSpec((bsz, None, cap, 1), lambda x, s: (0, x, 0, 0)),
            pl.BlockSpec((None, None, d, fs), lambda x, s: (layer, x, 0, s)),
            pl.BlockSpec((None, None, d, fs), lambda x, s: (layer, x, 0, s)),
            pl.BlockSpec((None, None, fs, d), lambda x, s: (layer, x, s, 0)),
        ],
        out_specs=pl.BlockSpec((bsz, None, cap, d), lambda x, s: (0, x, 0, 0)),
        out_shape=jax.ShapeDtypeStruct((bsz, e, cap, d), BF16),
        scratch_shapes=[pltpu.VMEM((bsz, cap, d), F32)],
        compiler_params=_cparams("parallel", "arbitrary"),
        name="expert_ffn",
    )(xg, gate, wg4, wu4, wd4)


def _combine_kernel(starts_ref, x_ref, pos_ref, ye_ref, g_ref, o_ref, stage_sc, *, final, win):
    b = pl.program_id(0)
    i = pl.program_id(1)
    ts = x_ref.shape[0]
    e, cap, _ = ye_ref.shape
    lane = lax.broadcasted_iota(jnp.int32, (ts, win), 1)
    pos = pos_ref[...]

    def first_row(x):
        lo = starts_ref[b, x, i]
        return pl.multiple_of(jnp.minimum((lo // BF16_ROWS) * BF16_ROWS, cap - win), BF16_ROWS)

    onehots = []
    for x in range(e):
        base = first_row(x)
        stage_sc[pl.ds(x * win, win), :] = ye_ref[x, pl.ds(base, win), :]
        onehots.append((pos[:, x:x + 1] - base == lane).astype(BF16))
    o_ref[...] = x_ref[...] + jnp.dot(jnp.concatenate(onehots, axis=1), stage_sc[...],
                                      preferred_element_type=F32)

    for x in range(e):
        base = first_row(x)
        hi = starts_ref[b, x, i + 1]
        col = pos[:, x:x + 1]

        def extra(k, carry, base=base, col=col, x=x):
            want = base + k * win
            row = pl.multiple_of(jnp.minimum(want, cap - win), BF16_ROWS)
            onehot = jnp.logical_and(col - row == lane, col >= want).astype(BF16)
            o_ref[...] += jnp.dot(onehot, ye_ref[x, pl.ds(row, win), :], preferred_element_type=F32)
            return carry

        windows = (jnp.maximum(hi - base, 1) + win - 1) // win
        lax.fori_loop(1, windows, extra, 0)

    if final:
        acc = o_ref[...]
        o_ref[...] = acc * lax.rsqrt(jnp.mean(acc * acc, axis=-1, keepdims=True) + NORM_EPS) * g_ref[...]


def _combine(x3, pos_t, starts, ye, g, final, ts=512, win=128):
    bsz, seq, d = x3.shape
    e, cap = ye.shape[1], ye.shape[2]
    assert cap % BF16_ROWS == 0 and win % BF16_ROWS == 0 and win <= cap
    grid_spec = pltpu.PrefetchScalarGridSpec(
        num_scalar_prefetch=1,
        grid=(bsz, seq // ts),
        in_specs=[
            pl.BlockSpec((None, ts, d), lambda b, i, st: (b, i, 0)),
            pl.BlockSpec((None, ts, e), lambda b, i, st: (b, i, 0)),
            pl.BlockSpec((None, e, cap, d), lambda b, i, st: (b, 0, 0, 0)),
            pl.BlockSpec((1, d), lambda b, i, st: (0, 0)),
        ],
        out_specs=pl.BlockSpec((None, ts, d), lambda b, i, st: (b, i, 0)),
        scratch_shapes=[pltpu.VMEM((e * win, d), BF16)],
    )
    return pl.pallas_call(
        functools.partial(_combine_kernel, final=final, win=win),
        grid_spec=grid_spec,
        out_shape=jax.ShapeDtypeStruct((bsz, seq, d), F32),
        compiler_params=_cparams("parallel", "parallel"),
        name="expert_combine",
    )(starts, x3, pos_t, ye, g)


def kernel(x, norm_mix, w_in, b_in, hy_conv_w, hy_conv_b, hy_ffn_w1, hy_ffn_b1, hy_ffn_f1, hy_ffn_w2, hy_ffn_b2, hy_ffn_f2, hy_ffn_w3, hy_bias, lambda_q1, lambda_k1, lambda_q2, lambda_k2, subln_g, w_up_hyena, w_up_attn, w_out, norm_ffn, w_router, b_router, w_e_gate, w_e_up, w_e_down, norm_final):
    bsz, seq, d = x.shape
    depth = w_in.shape[0]
    orders, c = hy_bias.shape[1], hy_bias.shape[2]
    head_dim = lambda_q1.shape[1]
    v_width = w_up_attn.shape[1]
    qk_width = v_width
    e = w_router.shape[2]
    cap = EC_FACTOR * seq // e
    col_q = (orders + 1) * c
    col_gate = col_q + 2 * qk_width + v_width
    assert orders == 2 and bsz % 2 == 0 and col_gate % d == 0

    consts = tuple(jnp.asarray(a, F32).astype(BF16) for a in _dft_constants(seq))

    xs = x.reshape(bsz * seq, d)
    out = None
    for l in range(depth):
        p2 = _inproj(xs, norm_mix[l][None], w_in[l].astype(BF16), b_in[l][None])
        p3 = p2.reshape(bsz, seq, -1)

        uc = _shortconv(p3, hy_conv_w[l], hy_conv_b[l][None], c)
        hfull = _filters(hy_ffn_w1[l], hy_ffn_b1[l], hy_ffn_f1[l], hy_ffn_w2[l], hy_ffn_b2[l], hy_ffn_f2[l],
                        hy_ffn_w3[l], hy_bias[l], seq)
        hspec = _spectrum(hfull, consts[4], consts[1])
        z = _longconv(uc, 0, uc, 1, hspec, 0, consts, natural_out=False)
        y_hy = _longconv(z, 0, uc, 2, hspec, 1, consts, natural_out=True)

        q_r, k_t, v_b = _rope(p2, bsz, seq, qk_width, v_width, head_dim, col_q)
        lam_init = 0.8 - 0.6 * math.exp(-0.3 * l)
        y_da = _attention(q_r.reshape(bsz, seq, qk_width), k_t, v_b.reshape(bsz, seq, v_width),
                          lambda_q1[l], lambda_k1[l], lambda_q2[l], lambda_k2[l], subln_g[l], head_dim, lam_init)

        xs, n2, aff = _merge(xs, y_hy.reshape(bsz * seq, c), y_da.reshape(bsz * seq, v_width), p2, col_gate,
                             w_up_hyena[l].astype(BF16), w_up_attn[l].astype(BF16), w_out[l].astype(BF16),
                             norm_ffn[l][None], w_router[l].T, b_router[l][:, None], bsz, seq)

        pos = _select(aff.reshape(bsz * e, seq), cap).reshape(bsz, e, seq)
        tiles = seq // COMBINE_TILE
        counts = jnp.sum((pos >= 0).reshape(bsz, e, tiles, COMBINE_TILE), axis=-1, dtype=jnp.int32)
        starts = jnp.concatenate([jnp.zeros((bsz, e, 1), jnp.int32), jnp.cumsum(counts, axis=-1)], axis=-1)
        xg, gate = _gather(pos, aff, starts, n2.reshape(bsz, seq, d), cap)
        ye = _experts(xg, gate, w_e_gate, w_e_up, w_e_down, l)
        final = l == depth - 1
        out = _combine(xs.reshape(bsz, seq, d), pos.transpose(0, 2, 1), starts, ye, norm_final[None], final,
                       ts=COMBINE_TILE)
        xs = out.reshape(bsz * seq, d)
    return out
```
